```python
import jax
import jax.numpy as jnp
from jax import lax
import numpy as np

D_MODEL = 2048
BATCH = 16
SEQ = 2048
DEPTH = 4

GRID_W = 64
CTX_LEN = 256
HEAD_DIM = 128
NA_HEADS = D_MODEL // (4 * HEAD_DIM)
NA_W = NA_HEADS * HEAD_DIM
NA_WIN_R = 8
NA_WIN_C = 16
NA_QCB = 16
NA_KCB = 32
DF_HEADS = D_MODEL // (4 * HEAD_DIM)
DF_DIM = HEAD_DIM // 2
DF_W = DF_HEADS * HEAD_DIM
DF_QBLK = 128
RG_WIDTH = D_MODEL // 2
RG_BLOCKS = 8
RG_BW = RG_WIDTH // RG_BLOCKS
RG_CONV = 4
RG_C = 8.0
N_BRANCH = 3
MIX_W = NA_W + DF_W + RG_WIDTH
MIX_SIZES = (NA_W, NA_W, DF_W, DF_W, RG_WIDTH, NA_W, DF_W, RG_WIDTH)
KV_COLS = 2 * NA_W + 2 * DF_W + RG_WIDTH
MIX_COLS = KV_COLS + NA_W + DF_W + RG_WIDTH
IN_COLS = MIX_COLS + N_BRANCH * D_MODEL
N_GROUPS = 4
EXPERTS_PER_GROUP = 8
N_EXPERTS = N_GROUPS * EXPERTS_PER_GROUP
TOP_K = 2
D_EXPERT = D_MODEL // 4
MOE_BLK = 128
ROPE_BASE = 10000.0
EPS = 1e-6
NEG = -1e30

kernel_name = 'hybrid_na_diff_rglru_hmoe_dit'


def rmsnorm(x, g):
    xf = x.astype(jnp.float32)
    y = xf * lax.rsqrt(jnp.mean(xf * xf, axis=-1, keepdims=True) + EPS)
    return (y * g.astype(jnp.float32)).astype(x.dtype)


def _split(z, sizes):
    return jnp.split(z, np.cumsum(sizes)[:-1].tolist(), axis=-1)


def _heads(t, h):
    return t.reshape(*t.shape[:-1], h, t.shape[-1] // h)


def _df_heads(t, g):
    return rmsnorm(t.reshape(*t.shape[:-1], DF_HEADS, 2, DF_DIM), g)


def axial_rope_tables(n_tok):
    t = jnp.arange(n_tok, dtype=jnp.int32)
    pos = jnp.stack([t // GRID_W, t % GRID_W], axis=0).astype(jnp.float32)
    n_freq = DF_DIM // 4
    inv = ROPE_BASE ** (-jnp.arange(n_freq, dtype=jnp.float32) / n_freq)
    ang = pos[:, :, None] * inv
    return jnp.cos(ang)[:, :, None, None, :], jnp.sin(ang)[:, :, None, None, :]


def axial_rope(x, cos, sin):
    half = x.shape[-1] // 2

    def rot(u, cs, sn):
        u1, u2 = jnp.split(u, 2, axis=-1)
        return jnp.concatenate([u1 * cs - u2 * sn, u1 * sn + u2 * cs], axis=-1)

    out = jnp.concatenate([rot(x[..., :half], cos[0], sin[0]), rot(x[..., half:], cos[1], sin[1])], axis=-1)
    return out.astype(x.dtype)


def neighbourhood_attention(q, k, v, kc, vc, rpb):
    B, S, H, Dh = q.shape
    R = S // GRID_W
    kr = min(NA_WIN_R, R)
    ncb = GRID_W // NA_QCB
    scale = Dh ** -0.5
    rows = np.arange(R)
    row_start = np.clip(rows - NA_WIN_R // 2, 0, R - kr)
    dr = row_start[:, None] + np.arange(kr)[None, :] - rows[:, None] + (NA_WIN_R - 1)
    qcols = np.arange(GRID_W).reshape(ncb, NA_QCB)
    kcol_start = np.clip(np.arange(ncb) * NA_QCB - (NA_KCB - NA_QCB) // 2, 0, GRID_W - NA_KCB)
    key_cols = kcol_start[:, None] + np.arange(NA_KCB)[None, :]
    qwin = np.clip(qcols - NA_WIN_C // 2, 0, GRID_W - NA_WIN_C)[..., None]
    kcb = key_cols[:, None, :]
    col_ok = (kcb >= qwin) & (kcb < qwin + NA_WIN_C)
    dc = np.clip(kcb - qcols[..., None], 1 - NA_WIN_C, NA_WIN_C - 1) + (NA_WIN_C - 1)
    bias_cols = jnp.where(col_ok, rpb[:, :, dc], NEG)
    kg = k.reshape(B, R, GRID_W, H, Dh)
    vg = v.reshape(B, R, GRID_W, H, Dh)
    q_rows = q.reshape(B, R, GRID_W, H, Dh).swapaxes(0, 1)
    n_loc = kr * NA_KCB

    def row_step(args):
        q_r, rs, dr_r = args
        kb = lax.dynamic_slice_in_dim(kg, rs, kr, axis=1)[:, :, key_cols]
        vb = lax.dynamic_slice_in_dim(vg, rs, kr, axis=1)[:, :, key_cols]
        qb = q_r.reshape(B, ncb, NA_QCB, H, Dh)
        bias = bias_cols[:, dr_r].transpose(2, 0, 3, 1, 4)
        s_loc = jnp.einsum('bnqhd,binkhd->bnhqik', qb, kb).astype(jnp.float32) * scale + bias
        s_ctx = jnp.einsum('bnqhd,bchd->bnhqc', qb, kc).astype(jnp.float32) * scale
        p = jax.nn.softmax(jnp.concatenate([s_loc.reshape(B, ncb, H, NA_QCB, n_loc), s_ctx], axis=-1), axis=-1)
        p_loc = p[..., :n_loc].reshape(B, ncb, H, NA_QCB, kr, NA_KCB).astype(v.dtype)
        p_ctx = p[..., n_loc:].astype(v.dtype)
        o = jnp.einsum('bnhqik,binkhd->bnqhd', p_loc, vb) + jnp.einsum('bnhqc,bchd->bnqhd', p_ctx, vc)
        return o.reshape(B, GRID_W, H, Dh)

    out = lax.map(row_step, (q_rows, jnp.asarray(row_start, jnp.int32), jnp.asarray(dr, jnp.int32)))
    return out.swapaxes(0, 1).reshape(B, S, H * Dh)


def context_attention(q, k, v):
    s = jnp.einsum('bqhd,bkhd->bhqk', q, k).astype(jnp.float32) * q.shape[-1] ** -0.5
    p = jax.nn.softmax(s, axis=-1).astype(v.dtype)
    return jnp.einsum('bhqk,bkhd->bqhd', p, v).reshape(q.shape[0], q.shape[1], -1)


def diff_weights(s, lam, dtype):
    p = jax.nn.softmax(s, axis=-1)
    return (p[:, :, 0] - lam * p[:, :, 1]).astype(dtype)


def diff_attention_latent(q, k, v, kc, vc, lam):
    B, S, H, _, dd = q.shape
    scale = dd ** -0.5
    q_blocks = q.reshape(B, S // DF_QBLK, DF_QBLK, H, 2, dd).swapaxes(0, 1)

    def blk(qb):
        s = jnp.concatenate([jnp.einsum('bqhmd,bkhmd->bhmqk', qb, k),
                             jnp.einsum('bqhmd,bkhmd->bhmqk', qb, kc)], axis=-1).astype(jnp.float32) * scale
        w = diff_weights(s, lam, v.dtype)
        return jnp.einsum('bhqk,bkhe->bqhe', w[..., :S], v) + jnp.einsum('bhqk,bkhe->bqhe', w[..., S:], vc)

    o = lax.map(blk, q_blocks)
    return o.swapaxes(0, 1).reshape(B, S, H, -1)


def diff_attention_context(q, k, v, lam):
    s = jnp.einsum('bqhmd,bkhmd->bhmqk', q, k).astype(jnp.float32) * q.shape[-1] ** -0.5
    return jnp.einsum('bhqk,bkhe->bqhe', diff_weights(s, lam, v.dtype), v)


def diff_output(o, g, lam_init):
    return (rmsnorm(o, g) * (1.0 - lam_init)).reshape(o.shape[0], o.shape[1], -1)


def short_conv(u, w, b):
    out = lax.conv_general_dilated(u, w[:, None, :].astype(u.dtype), (1,),
                                   [(RG_CONV // 2, RG_CONV - 1 - RG_CONV // 2)],
                                   dimension_numbers=('NWC', 'WIO', 'NWC'), feature_group_count=u.shape[-1])
    return out + b


def blockdiag(u, w):
    nb, bw, _ = w.shape
    return jnp.einsum('blnd,nde->blne', u.reshape(*u.shape[:-1], nb, bw), w).reshape(u.shape)


def rg_coeffs(u, w_a, b_a, w_x, b_x, lam):
    r = jax.nn.sigmoid(blockdiag(u, w_a) + b_a).astype(jnp.float32)
    i = jax.nn.sigmoid(blockdiag(u, w_x) + b_x)
    log_a = -RG_C * r * jax.nn.softplus(-lam.astype(jnp.float32))
    a = jnp.exp(log_a)
    mult = jnp.sqrt(-jnp.expm1(2.0 * log_a))
    return a, mult * (i * u).astype(jnp.float32)


def _scan_combine(left, right):
    return left[0] * right[0], right[0] * left[1] + right[1]


def linear_scan(a, b, h0, reverse):
    if reverse:
        a, b = jnp.flip(a, 1), jnp.flip(b, 1)
    if h0 is not None:
        b = b.at[:, 0].add(a[:, 0] * h0)
    _, h = lax.associative_scan(_scan_combine, (a, b), axis=1)
    return jnp.flip(h, 1) if reverse else h


def branch_merge(ys, zg, wb, wo):
    g_a, g_b, g_c = jnp.split(jax.nn.sigmoid(zg), N_BRANCH, axis=-1)
    wb_a, wb_b, wb_c = jnp.split(wb, [NA_W, NA_W + DF_W], axis=0)
    m = g_a * (ys[0] @ wb_a) + g_b * (ys[1] @ wb_b) + g_c * (ys[2] @ wb_c)
    return m @ wo


def routed_experts(t, eid, wt, w1, w3, w2):
    n_tok, d = t.shape
    m = eid.shape[0]
    tok = jnp.arange(m, dtype=jnp.int32) // TOP_K
    order = jnp.argsort(eid, stable=True)
    se, stok, sw = eid[order], tok[order], wt[order]
    counts = jnp.bincount(eid, length=N_EXPERTS)
    starts = jnp.cumsum(counts) - counts
    pcounts = (counts + MOE_BLK - 1) // MOE_BLK * MOE_BLK
    pends = jnp.cumsum(pcounts)
    dest = (pends - pcounts)[se] + jnp.arange(m, dtype=jnp.int32) - starts[se]
    n_blk = (m + N_EXPERTS * (MOE_BLK - 1) + MOE_BLK - 1) // MOE_BLK
    n_pad = n_blk * MOE_BLK
    buf_tok = jnp.full((n_pad,), n_tok, jnp.int32).at[dest].set(stok)
    buf_w = jnp.zeros((n_pad,), t.dtype).at[dest].set(sw.astype(t.dtype))
    blk_e = jnp.minimum(jnp.searchsorted(pends, jnp.arange(n_blk) * MOE_BLK, side='right'), N_EXPERTS - 1)
    t_pad = jnp.concatenate([t, jnp.zeros((1, d), t.dtype)], axis=0)

    def block(args):
        idx, wb, e = args
        xb = t_pad[idx]
        hid = jax.nn.silu(xb @ w1[e]) * (xb @ w3[e])
        return (hid @ w2[e]) * wb[:, None]

    y = lax.map(block, (buf_tok.reshape(n_blk, MOE_BLK), buf_w.reshape(n_blk, MOE_BLK), blk_e))
    out = jnp.zeros((n_tok + 1, d), y.dtype).at[buf_tok].add(y.reshape(n_pad, d))
    return out[:n_tok]


def hierarchical_moe(t, w_rg, b_rg, w_re, b_re, w1, w3, w2):
    n = t.shape[0]
    g_logits = (t @ w_rg).astype(jnp.float32) + b_rg.astype(jnp.float32)
    g_prob = jax.nn.softmax(g_logits, axis=-1)
    grp = jnp.argmax(g_logits, axis=-1)
    p_grp = jnp.take_along_axis(g_prob, grp[:, None], axis=-1)
    e_logits = ((t @ w_re).astype(jnp.float32) + b_re.astype(jnp.float32)).reshape(n, N_GROUPS, EXPERTS_PER_GROUP)
    e_in = jnp.take_along_axis(e_logits, grp[:, None, None], axis=1)[:, 0]
    top_v, top_i = lax.top_k(e_in, TOP_K)
    w = jax.nn.softmax(top_v, axis=-1) * p_grp
    eid = (grp[:, None] * EXPERTS_PER_GROUP + top_i).astype(jnp.int32)
    return routed_experts(t, eid.reshape(-1), w.reshape(-1), w1, w3, w2)


def setup_inputs(seed: int = 0) -> dict:
    key = jax.random.key(seed)
    ks = iter(jax.random.split(key, 40))
    f32 = jnp.float32
    L, D = DEPTH, D_MODEL

    def nrm(shape, scale):
        return jax.random.normal(next(ks), shape, f32) * scale

    def gain(shape):
        return 1.0 + nrm(shape, 0.02)

    inp = {}
    inp['x'] = nrm((BATCH, SEQ, D), 1.0)
    inp['c'] = nrm((BATCH, D), 1.0)
    inp['ctx'] = nrm((BATCH, CTX_LEN, D), 1.0)
    inp['c_ctx'] = nrm((D,), 1.0)
    inp['w_mod'] = nrm((L, D, 6 * D), 0.5 * D ** -0.5)
    inp['b_mod'] = nrm((L, 6 * D), 0.01)
    inp['norm1_g'] = gain((L, D))
    inp['norm2_g'] = gain((L, D))
    inp['w_in'] = nrm((L, D, IN_COLS), D ** -0.5)
    inp['na_q_g'] = gain((L, HEAD_DIM))
    inp['na_k_g'] = gain((L, HEAD_DIM))
    inp['na_rpb'] = nrm((L, NA_HEADS, 2 * NA_WIN_R - 1, 2 * NA_WIN_C - 1), 0.1)
    inp['df_q_g'] = gain((L, DF_DIM))
    inp['df_k_g'] = gain((L, DF_DIM))
    inp['df_lam'] = nrm((L, 4, DF_DIM), 0.1)
    inp['df_sub_g'] = gain((L, 2 * DF_DIM))
    inp['rg_conv_w'] = nrm((L, RG_CONV, RG_WIDTH), RG_CONV ** -0.5)
    inp['rg_conv_b'] = nrm((L, RG_WIDTH), 0.01)
    inp['rg_w_a'] = nrm((L, 2, RG_BLOCKS, RG_BW, RG_BW), RG_BW ** -0.5)
    inp['rg_b_a'] = nrm((L, 2, RG_WIDTH), 0.01)
    inp['rg_w_x'] = nrm((L, 2, RG_BLOCKS, RG_BW, RG_BW), RG_BW ** -0.5)
    inp['rg_b_x'] = nrm((L, 2, RG_WIDTH), 0.01)
    a0 = jax.random.uniform(next(ks), (L, 2, RG_WIDTH), f32, 0.9, 0.999)
    root = a0 ** (1.0 / RG_C)
    inp['rg_lam'] = jnp.log(root) - jnp.log1p(-root)
    inp['w_branch'] = jnp.concatenate([nrm((L, NA_W, D), NA_W ** -0.5),
                                       nrm((L, DF_W, D), DF_W ** -0.5),
                                       nrm((L, RG_WIDTH, D), RG_WIDTH ** -0.5)], axis=1)
    inp['w_out'] = nrm((L, D, D), D ** -0.5)
    inp['w_router_g'] = nrm((L, D, N_GROUPS), D ** -0.5)
    inp['b_router_g'] = nrm((L, N_GROUPS), 0.01)
    inp['w_router_e'] = nrm((L, D, N_EXPERTS), D ** -0.5)
    inp['b_router_e'] = nrm((L, N_EXPERTS), 0.01)
    inp['w1'] = nrm((L, N_EXPERTS, D, D_EXPERT), D ** -0.5)
    inp['w3'] = nrm((L, N_EXPERTS, D, D_EXPERT), D ** -0.5)
    inp['w2'] = nrm((L, N_EXPERTS, D_EXPERT, D), D_EXPERT ** -0.5)
    return inp


def reference(x, c, ctx, c_ctx, w_mod, b_mod, norm1_g, norm2_g, w_in, na_q_g, na_k_g, na_rpb,
              df_q_g, df_k_g, df_lam, df_sub_g, rg_conv_w, rg_conv_b, rg_w_a, rg_b_a, rg_w_x, rg_b_x,
              rg_lam, w_branch, w_out, w_router_g, b_router_g, w_router_e, b_router_e, w1, w3, w2):
    B, S, D = x.shape
    C = ctx.shape[1]
    rope_cos, rope_sin = axial_rope_tables(S)
    silu_c = jax.nn.silu(c)
    silu_cc = jax.nn.silu(c_ctx)
    xc = ctx
    for l in range(DEPTH):
        need_ctx = l < DEPTH - 1
        lam_init = 0.8 - 0.6 * float(np.exp(-0.3 * l))
        mod = silu_c @ w_mod[l] + b_mod[l]
        sh1, sc1, gt1, sh2, sc2, gt2 = [m[:, None, :] for m in jnp.split(mod, 6, axis=-1)]
        n_cm = 6 if need_ctx else 2
        modc = jnp.split(silu_cc @ w_mod[l][:, :n_cm * D] + b_mod[l][:n_cm * D], n_cm)

        hx = rmsnorm(x, norm1_g[l]) * (1 + sc1) + sh1
        hc = rmsnorm(xc, norm1_g[l]) * (1 + modc[1]) + modc[0]
        n_cc = len(MIX_SIZES) if need_ctx else 5
        zx = _split(hx @ w_in[l][:, :MIX_COLS], MIX_SIZES)
        zc = _split(hc @ w_in[l][:, :sum(MIX_SIZES[:n_cc])], MIX_SIZES[:n_cc])

        na_kx = rmsnorm(_heads(zx[0], NA_HEADS), na_k_g[l])
        na_vx = _heads(zx[1], NA_HEADS)
        na_qx = rmsnorm(_heads(zx[5], NA_HEADS), na_q_g[l])
        na_kc = rmsnorm(_heads(zc[0], NA_HEADS), na_k_g[l])
        na_vc = _heads(zc[1], NA_HEADS)
        y_a = neighbourhood_attention(na_qx, na_kx, na_vx, na_kc, na_vc, na_rpb[l])

        lp = df_lam[l].astype(jnp.float32)
        lam = jnp.exp(jnp.sum(lp[0] * lp[1])) - jnp.exp(jnp.sum(lp[2] * lp[3])) + lam_init
        df_qx = axial_rope(_df_heads(zx[6], df_q_g[l]), rope_cos, rope_sin)
        df_kx = axial_rope(_df_heads(zx[2], df_k_g[l]), rope_cos, rope_sin)
        df_vx = _heads(zx[3], DF_HEADS)
        df_kc = _df_heads(zc[2], df_k_g[l])
        df_vc = _heads(zc[3], DF_HEADS)
        y_b = diff_output(diff_attention_latent(df_qx, df_kx, df_vx, df_kc, df_vc, lam), df_sub_g[l], lam_init)

        u_c = short_conv(zc[4], rg_conv_w[l], rg_conv_b[l])
        u_x = short_conv(zx[4], rg_conv_w[l], rg_conv_b[l])
        h_ctx = []
        h_lat = []
        for d, rev in ((0, False), (1, True)):
            a_c, b_c = rg_coeffs(u_c, rg_w_a[l, d], rg_b_a[l, d], rg_w_x[l, d], rg_b_x[l, d], rg_lam[l, d])
            hc_d = linear_scan(a_c, b_c, None, rev)
            h0 = hc_d[:, 0] if rev else hc_d[:, -1]
            a_x, b_x = rg_coeffs(u_x, rg_w_a[l, d], rg_b_a[l, d], rg_w_x[l, d], rg_b_x[l, d], rg_lam[l, d])
            h_ctx.append(hc_d)
            h_lat.append(linear_scan(a_x, b_x, h0, rev))
        y_c = (h_lat[0] + h_lat[1]).astype(zx[7].dtype) * jax.nn.gelu(zx[7], approximate=True)

        x_mid = x + gt1 * branch_merge((y_a, y_b, y_c), hx @ w_in[l][:, MIX_COLS:], w_branch[l], w_out[l])
        if need_ctx:
            na_qc = rmsnorm(_heads(zc[5], NA_HEADS), na_q_g[l])
            y_ac = context_attention(na_qc, na_kc, na_vc)
            df_qc = _df_heads(zc[6], df_q_g[l])
            y_bc = diff_output(diff_attention_context(df_qc, df_kc, df_vc, lam), df_sub_g[l], lam_init)
            y_cc = (h_ctx[0] + h_ctx[1]).astype(zc[7].dtype) * jax.nn.gelu(zc[7], approximate=True)
            xc_mid = xc + modc[2] * branch_merge((y_ac, y_bc, y_cc), hc @ w_in[l][:, MIX_COLS:], w_branch[l], w_out[l])
        x = x_mid

        hx2 = rmsnorm(x, norm2_g[l]) * (1 + sc2) + sh2
        moe_p = (w_router_g[l], b_router_g[l], w_router_e[l], b_router_e[l], w1[l], w3[l], w2[l])
        if need_ctx:
            hc2 = rmsnorm(xc_mid, norm2_g[l]) * (1 + modc[4]) + modc[3]
            toks = jnp.concatenate([hc2.reshape(B * C, D), hx2.reshape(B * S, D)], axis=0)
            f = hierarchical_moe(toks, *moe_p)
            xc = xc_mid + modc[5] * f[:B * C].reshape(B, C, D)
            fx = f[B * C:]
        else:
            fx = hierarchical_moe(hx2.reshape(B * S, D), *moe_p)
        x = x + gt2 * fx.reshape(B, S, D)
    return x
```

```python
import functools

import jax
import jax.numpy as jnp
import numpy as np
from jax import lax
from jax.experimental import pallas as pl
from jax.experimental.pallas import tpu as pltpu

F32 = jnp.float32
BF16 = jnp.bfloat16

D_MODEL = 2048
DEPTH = 4
GRID_W = 64
HEAD_DIM = 128
N_HEADS = 4
NA_WIN_R = 8
NA_WIN_C = 16
NA_QCB = 16
NA_KCB = 32
DF_DIM = 64
RG_WIDTH = 1024
RG_BLOCKS = 8
RG_BW = 128
RG_C = 8.0
N_GROUPS = 4
EXPERTS_PER_GROUP = 8
N_EXPERTS = 32
TOP_K = 2
D_EXPERT = 512
ROPE_BASE = 10000.0
EPS = 1e-6
NEG = -1e30

COL_NA_K, COL_NA_V, COL_DF_K, COL_DF_V, COL_RG_X = 0, 512, 1024, 1536, 2048
COL_NA_Q, COL_DF_Q, COL_RG_G, COL_GATE = 3072, 3584, 4096, 5120
KV_COLS = 3072
MIX_COLS = 5120
IN_COLS = MIX_COLS + 3 * D_MODEL

VMEM_LIMIT_BYTES = 56 * 1024 * 1024

NA_ROWS_PER_STEP = 4
NA_KEY_ROWS = NA_ROWS_PER_STEP + NA_WIN_R - 1
DF_QBLK = 256
MOE_TM = 256
SCAN_CHUNKS = 8


def _cparams(*sem):
    return pltpu.CompilerParams(dimension_semantics=sem, vmem_limit_bytes=VMEM_LIMIT_BYTES)


def _rms(x, g):
    return x * lax.rsqrt(jnp.mean(x * x, axis=-1, keepdims=True) + EPS) * g


def _mod_kernel(s_ref, w_ref, b_ref, o_ref):
    s = s_ref[...]
    a = (s * jax.nn.sigmoid(s)).astype(BF16)
    o_ref[0] = jnp.dot(a, w_ref[0].astype(BF16), preferred_element_type=F32) + b_ref[0]


def _modulation(rows, w_mod, b_mod, tn=1024):
    n_l, d, n = w_mod.shape
    r = rows.shape[0]
    return pl.pallas_call(
        _mod_kernel,
        grid=(n_l, n // tn),
        in_specs=[
            pl.BlockSpec((r, d), lambda l, j: (0, 0)),
            pl.BlockSpec((1, d, tn), lambda l, j: (l, 0, j)),
            pl.BlockSpec((1, 1, tn), lambda l, j: (l, 0, j)),
        ],
        out_specs=pl.BlockSpec((1, r, tn), lambda l, j: (l, 0, j)),
        out_shape=jax.ShapeDtypeStruct((n_l, r, n), F32),
        compiler_params=_cparams("parallel", "parallel"),
        name="modulation",
    )(rows, w_mod, b_mod.reshape(n_l, 1, n))


def _nmm_kernel(x_ref, g_ref, sc_ref, sh_ref, w_ref, o_ref, h_ref):
    @pl.when(pl.program_id(2) == 0)
    def _():
        h = _rms(x_ref[0], g_ref[...]) * (1.0 + sc_ref[0]) + sh_ref[0]
        h_ref[...] = h.astype(BF16)

    o_ref[0] = jnp.dot(h_ref[...], w_ref[...], preferred_element_type=F32)


def _norm_mod_matmul(x, g, sc, sh, w, tm, tn):
    b, t, d = x.shape
    n = w.shape[1]
    return pl.pallas_call(
        _nmm_kernel,
        grid=(b, t // tm, n // tn),
        in_specs=[
            pl.BlockSpec((1, tm, d), lambda bi, i, j: (bi, i, 0)),
            pl.BlockSpec((1, d), lambda bi, i, j: (0, 0)),
            pl.BlockSpec((1, 1, d), lambda bi, i, j: (bi, 0, 0)),
            pl.BlockSpec((1, 1, d), lambda bi, i, j: (bi, 0, 0)),
            pl.BlockSpec((d, tn), lambda bi, i, j: (0, j)),
        ],
        out_specs=pl.BlockSpec((1, tm, tn), lambda bi, i, j: (bi, i, j)),
        out_shape=jax.ShapeDtypeStruct((b, t, n), F32),
        scratch_shapes=[pltpu.VMEM((tm, d), BF16)],
        compiler_params=_cparams("parallel", "parallel", "arbitrary"),
        name="norm_mod_proj",
    )(x, g.reshape(1, d), sc, sh, w)


def _na_window_start(r0):
    lower = jnp.clip(r0 - NA_WIN_R // 2, 0, (2048 // GRID_W) - NA_WIN_R)
    return jnp.minimum(lower, (2048 // GRID_W) - NA_KEY_ROWS)


def _na_bias_indices(n_rows):
    rb, kw = NA_ROWS_PER_STEP, NA_KEY_ROWS
    rows = np.arange(n_rows)
    row_start = np.clip(rows - NA_WIN_R // 2, 0, n_rows - NA_WIN_R)
    qc = np.arange(GRID_W)
    kc = np.arange(GRID_W)
    qwin = np.clip(qc - NA_WIN_C // 2, 0, GRID_W - NA_WIN_C)
    kcol_start = np.clip((qc // NA_QCB) * NA_QCB - (NA_KCB - NA_QCB) // 2, 0, GRID_W - NA_KCB)
    col_ok = (kc[None, :] >= qwin[:, None]) & (kc[None, :] < qwin[:, None] + NA_WIN_C)
    col_ok &= (kc[None, :] >= kcol_start[:, None]) & (kc[None, :] < kcol_start[:, None] + NA_KCB)
    dc = np.clip(kc[None, :] - qc[:, None], 1 - NA_WIN_C, NA_WIN_C - 1) + (NA_WIN_C - 1)
    dc = np.where(col_ok, dc, 2 * NA_WIN_C - 1)
    n_steps = n_rows // rb
    dr_idx = np.zeros((n_steps, rb * GRID_W, kw * GRID_W), np.int32)
    dc_idx = np.full((n_steps, rb * GRID_W, kw * GRID_W), 2 * NA_WIN_C - 1, np.int32)
    for s in range(n_steps):
        r0 = s * rb
        ws = min(int(np.clip(r0 - NA_WIN_R // 2, 0, n_rows - NA_WIN_R)), n_rows - kw)
        for j in range(rb):
            r = r0 + j
            for i in range(kw):
                kr = ws + i
                if row_start[r] <= kr < row_start[r] + NA_WIN_R:
                    qs = slice(j * GRID_W, (j + 1) * GRID_W)
                    ks = slice(i * GRID_W, (i + 1) * GRID_W)
                    dr_idx[s, qs, ks] = kr - r + (NA_WIN_R - 1)
                    dc_idx[s, qs, ks] = dc
    return dr_idx, dc_idx


def _na_kernel(q_ref, k_ref, v_ref, kc_ref, vc_ref, bias_ref, gq_ref, gk_ref, o_ref, kn, vn, knc, vnc):
    step = pl.program_id(1)

    @pl.when(step == 0)
    def _():
        for h in range(N_HEADS):
            sl = slice(h * HEAD_DIM, (h + 1) * HEAD_DIM)
            kn[:, sl] = _rms(k_ref[0, :, sl], gk_ref[...]).astype(BF16)
            knc[:, sl] = _rms(kc_ref[0, :, sl], gk_ref[...]).astype(BF16)
        vn[...] = v_ref[0].astype(BF16)
        vnc[...] = vc_ref[0].astype(BF16)

    ws = _na_window_start(step * NA_ROWS_PER_STEP)
    start = pl.multiple_of(ws * GRID_W, GRID_W)
    nk = NA_KEY_ROWS * GRID_W
    scale = HEAD_DIM ** -0.5
    nt = (((1,), (1,)), ((), ()))
    for h in range(N_HEADS):
        sl = slice(h * HEAD_DIM, (h + 1) * HEAD_DIM)
        qn = (_rms(q_ref[0, :, sl], gq_ref[...]) * scale).astype(BF16)
        s_loc = lax.dot_general(qn, kn[pl.ds(start, nk), sl], nt, preferred_element_type=F32) + bias_ref[0, h]
        s_ctx = lax.dot_general(qn, knc[:, sl], nt, preferred_element_type=F32)
        m = jnp.maximum(jnp.max(s_loc, axis=-1, keepdims=True), jnp.max(s_ctx, axis=-1, keepdims=True))
        p_loc = jnp.exp(s_loc - m)
        p_ctx = jnp.exp(s_ctx - m)
        l = jnp.sum(p_loc, axis=-1, keepdims=True) + jnp.sum(p_ctx, axis=-1, keepdims=True)
        o = jnp.dot(p_loc.astype(BF16), vn[pl.ds(start, nk), sl], preferred_element_type=F32)
        o = o + jnp.dot(p_ctx.astype(BF16), vnc[:, sl], preferred_element_type=F32)
        o_ref[0, :, sl] = (o * (1.0 / l)).astype(BF16)


def _na_attention(zx, zc, bias, gq, gk):
    b, s, _ = zx.shape
    c = zc.shape[1]
    w = N_HEADS * HEAD_DIM
    tq = NA_ROWS_PER_STEP * GRID_W
    nk = NA_KEY_ROWS * GRID_W
    return pl.pallas_call(
        _na_kernel,
        grid=(b, s // tq),
        in_specs=[
            pl.BlockSpec((1, tq, w), lambda bi, i: (bi, i, COL_NA_Q // w)),
            pl.BlockSpec((1, s, w), lambda bi, i: (bi, 0, COL_NA_K // w)),
            pl.BlockSpec((1, s, w), lambda bi, i: (bi, 0, COL_NA_V // w)),
            pl.BlockSpec((1, c, w), lambda bi, i: (bi, 0, COL_NA_K // w)),
            pl.BlockSpec((1, c, w), lambda bi, i: (bi, 0, COL_NA_V // w)),
            pl.BlockSpec((1, N_HEADS, tq, nk), lambda bi, i: (i, 0, 0, 0)),
            pl.BlockSpec((1, HEAD_DIM), lambda bi, i: (0, 0)),
            pl.BlockSpec((1, HEAD_DIM), lambda bi, i: (0, 0)),
        ],
        out_specs=pl.BlockSpec((1, tq, w), lambda bi, i: (bi, i, 0)),
        out_shape=jax.ShapeDtypeStruct((b, s, w), BF16),
        scratch_shapes=[pltpu.VMEM((s, w), BF16), pltpu.VMEM((s, w), BF16),
                        pltpu.VMEM((c, w), BF16), pltpu.VMEM((c, w), BF16)],
        compiler_params=_cparams("parallel", "arbitrary"),
        name="na_attention",
    )(zx, zx, zx, zc, zc, bias, gq.reshape(1, HEAD_DIM), gk.reshape(1, HEAD_DIM))


def _na_ctx_kernel(q_ref, k_ref, v_ref, gq_ref, gk_ref, o_ref):
    scale = HEAD_DIM ** -0.5
    nt = (((1,), (1,)), ((), ()))
    for h in range(N_HEADS):
        sl = slice(h * HEAD_DIM, (h + 1) * HEAD_DIM)
        qn = (_rms(q_ref[0, :, sl], gq_ref[...]) * scale).astype(BF16)
        kn = _rms(k_ref[0, :, sl], gk_ref[...]).astype(BF16)
        s = lax.dot_general(qn, kn, nt, preferred_element_type=F32)
        p = jnp.exp(s - jnp.max(s, axis=-1, keepdims=True))
        l = jnp.sum(p, axis=-1, keepdims=True)
        o = jnp.dot(p.astype(BF16), v_ref[0, :, sl].astype(BF16), preferred_element_type=F32)
        o_ref[0, :, sl] = (o * (1.0 / l)).astype(BF16)


def _na_ctx_attention(zc, gq, gk):
    b, c, _ = zc.shape
    w = N_HEADS * HEAD_DIM
    return pl.pallas_call(
        _na_ctx_kernel,
        grid=(b,),
        in_specs=[
            pl.BlockSpec((1, c, w), lambda bi: (bi, 0, COL_NA_Q // w)),
            pl.BlockSpec((1, c, w), lambda bi: (bi, 0, COL_NA_K // w)),
            pl.BlockSpec((1, c, w), lambda bi: (bi, 0, COL_NA_V // w)),
            pl.BlockSpec((1, HEAD_DIM), lambda bi: (0, 0)),
            pl.BlockSpec((1, HEAD_DIM), lambda bi: (0, 0)),
        ],
        out_specs=pl.BlockSpec((1, c, w), lambda bi: (bi, 0, 0)),
        out_shape=jax.ShapeDtypeStruct((b, c, w), BF16),
        compiler_params=_cparams("parallel"),
        name="na_ctx_attention",
    )(zc, zc, zc, gq.reshape(1, HEAD_DIM), gk.reshape(1, HEAD_DIM))


def _rope_tables(n_tok):
    t = np.arange(n_tok)
    pos = np.stack([t // GRID_W, t % GRID_W], axis=0).astype(np.float32)
    n_freq = DF_DIM // 4
    inv = (np.float32(ROPE_BASE) ** (-np.arange(n_freq, dtype=np.float32) / n_freq)).astype(np.float32)
    ang = jnp.asarray(pos[:, :, None] * inv)
    cos, sin = jnp.cos(ang), jnp.sin(ang)
    cos64 = jnp.concatenate([cos[0], cos[0], cos[1], cos[1]], axis=-1)
    sin64 = jnp.concatenate([-sin[0], sin[0], -sin[1], sin[1]], axis=-1)
    return jnp.concatenate([cos64, cos64], axis=-1), jnp.concatenate([sin64, sin64], axis=-1)


def _df_kernel(lam_ref, q_ref, *refs, n_x, n_c, rope, out_scale):
    if n_x:
        kx_ref, vx_ref, kc_ref, vc_ref, cq_ref, sq_ref, ck_ref, sk_ref, gq_ref, gk_ref, gs_ref, o_ref, kn, vn = refs
    else:
        kc_ref, vc_ref, gq_ref, gk_ref, gs_ref, o_ref, kn, vn = refs
    lane = lax.broadcasted_iota(jnp.int32, (1, 2 * DF_DIM), 1)
    lo = lane < DF_DIM
    first = (lane % (DF_DIM // 2)) < (DF_DIM // 4)

    def norm64(x, g):
        x2 = x * x
        s0 = jnp.sum(jnp.where(lo, x2, 0.0), axis=-1, keepdims=True)
        s1 = jnp.sum(jnp.where(lo, 0.0, x2), axis=-1, keepdims=True)
        ms = jnp.where(lo, s0, s1) * (1.0 / DF_DIM)
        return x * lax.rsqrt(ms + EPS) * g

    def rot(x, cos, sin):
        partner = jnp.where(first, pltpu.roll(x, 2 * DF_DIM - DF_DIM // 4, 1), pltpu.roll(x, DF_DIM // 4, 1))
        return x * cos + partner * sin

    @pl.when(pl.program_id(2) == 0)
    def _():
        if n_x:
            kx = rot(norm64(kx_ref[0], gk_ref[...]), ck_ref[...], sk_ref[...])
            kn[0:n_x, :] = kx.astype(BF16)
            vn[0:n_x, :] = vx_ref[0].astype(BF16)
        kn[n_x:n_x + n_c, :] = norm64(kc_ref[0], gk_ref[...]).astype(BF16)
        vn[n_x:n_x + n_c, :] = vc_ref[0].astype(BF16)

    q = norm64(q_ref[0], gq_ref[...])
    if rope:
        q = rot(q, cq_ref[...], sq_ref[...])
    q = q * (DF_DIM ** -0.5)
    nt = (((1,), (1,)), ((), ()))
    k = kn[...]
    s0 = lax.dot_general(jnp.where(lo, q, 0.0).astype(BF16), k, nt, preferred_element_type=F32)
    s1 = lax.dot_general(jnp.where(lo, 0.0, q).astype(BF16), k, nt, preferred_element_type=F32)
    p0 = jnp.exp(s0 - jnp.max(s0, axis=-1, keepdims=True))
    p1 = jnp.exp(s1 - jnp.max(s1, axis=-1, keepdims=True))
    r0 = 1.0 / jnp.sum(p0, axis=-1, keepdims=True)
    r1 = lam_ref[0] / jnp.sum(p1, axis=-1, keepdims=True)
    wgt = (p0 * r0 - p1 * r1).astype(BF16)
    o = jnp.dot(wgt, vn[...], preferred_element_type=F32)
    o_ref[0] = (_rms(o, gs_ref[...]) * out_scale).astype(BF16)


def _df_attention(lam, zq, zx, zc, cos, sin, gq, gk, gs, out_scale):
    b, t, _ = zq.shape
    c = zc.shape[1]
    n_x = 0 if zx is None else zx.shape[1]
    w = 2 * DF_DIM
    tq = min(DF_QBLK, t)
    gq2 = jnp.concatenate([gq, gq]).reshape(1, w)
    gk2 = jnp.concatenate([gk, gk]).reshape(1, w)
    col = lambda base: (lambda bi, h, i: (bi, 0, base // w + h))
    vec = pl.BlockSpec((1, w), lambda bi, h, i: (0, 0))
    in_specs = [pl.BlockSpec(memory_space=pltpu.SMEM),
                pl.BlockSpec((1, tq, w), lambda bi, h, i: (bi, i, COL_DF_Q // w + h))]
    args = [lam.reshape(1), zq]
    if n_x:
        in_specs += [pl.BlockSpec((1, n_x, w), col(COL_DF_K)), pl.BlockSpec((1, n_x, w), col(COL_DF_V))]
        args += [zx, zx]
    in_specs += [pl.BlockSpec((1, c, w), col(COL_DF_K)), pl.BlockSpec((1, c, w), col(COL_DF_V))]
    args += [zc, zc]
    if n_x:
        in_specs += [pl.BlockSpec((tq, w), lambda bi, h, i: (i, 0)), pl.BlockSpec((tq, w), lambda bi, h, i: (i, 0)),
                     pl.BlockSpec((n_x, w), lambda bi, h, i: (0, 0)), pl.BlockSpec((n_x, w), lambda bi, h, i: (0, 0))]
        args += [cos, sin, cos, sin]
    in_specs += [vec, vec, vec]
    args += [gq2, gk2, gs.reshape(1, w)]
    return pl.pallas_call(
        functools.partial(_df_kernel, n_x=n_x, n_c=c, rope=bool(n_x), out_scale=out_scale),
        grid=(b, N_HEADS, t // tq),
        in_specs=in_specs,
        out_specs=pl.BlockSpec((1, tq, w), lambda bi, h, i: (bi, i, h)),
        out_shape=jax.ShapeDtypeStruct((b, t, N_HEADS * w), BF16),
        scratch_shapes=[pltpu.VMEM((n_x + c, w), BF16), pltpu.VMEM((n_x + c, w), BF16)],
        compiler_params=_cparams("parallel", "parallel", "arbitrary"),
        name="diff_attention" if n_x else "diff_ctx_attention",
    )(*args)


def _rg_kernel(*refs, n_x, n_c, ctx_out):
    if ctx_out:
        (ux_ref, uc_ref, gx_ref, gc_ref, cw_ref, cb_ref, wa_ref, ba_ref, wx_ref, bx_ref, lam_ref,
         ox_ref, oc_ref, a_s, b_s, p_s) = refs
    else:
        (ux_ref, uc_ref, gx_ref, cw_ref, cb_ref, wa_ref, ba_ref, wx_ref, bx_ref, lam_ref,
         ox_ref, a_s, b_s, p_s) = refs
    n_t = n_x + n_c
    clen = n_t // SCAN_CHUNKS

    def conv(z):
        n = z.shape[0]
        t = lax.broadcasted_iota(jnp.int32, (n, 1), 0)
        zm2 = jnp.where(t >= 2, pltpu.roll(z, 2, 0), 0.0)
        zm1 = jnp.where(t >= 1, pltpu.roll(z, 1, 0), 0.0)
        zp1 = jnp.where(t < n - 1, pltpu.roll(z, n - 1, 0), 0.0)
        return (zm2 * cw_ref[0:1, :] + zm1 * cw_ref[1:2, :] + z * cw_ref[2:3, :] + zp1 * cw_ref[3:4, :]
                + cb_ref[...])

    def coeffs(u, d, row0):
        ub = u.astype(BF16)
        r = jax.nn.sigmoid(jnp.dot(ub, wa_ref[d, 0], preferred_element_type=F32) + ba_ref[d:d + 1, :])
        gi = jax.nn.sigmoid(jnp.dot(ub, wx_ref[d, 0], preferred_element_type=F32) + bx_ref[d:d + 1, :])
        log_a = (-RG_C) * r * jax.nn.softplus(-lam_ref[d:d + 1, :])
        a = jnp.exp(log_a)
        a_s[d, row0:row0 + u.shape[0], :] = a
        b_s[d, row0:row0 + u.shape[0], :] = jnp.sqrt(-jnp.tanh(log_a) * (a * a + 1.0)) * (gi * u)

    u_c = conv(uc_ref[0])
    coeffs(u_c, 0, 0)
    coeffs(u_c, 1, n_x)
    u_x = conv(ux_ref[0])
    coeffs(u_x, 0, n_c)
    coeffs(u_x, 1, 0)

    def fwd_step(tau, carry):
        h, p = carry
        idx = pl.ds(tau, SCAN_CHUNKS, stride=clen)
        a = a_s[0, idx, :]
        h = a * h + b_s[0, idx, :]
        p = a * p
        b_s[0, idx, :] = h
        p_s[0, idx, :] = p
        return h, p

    def rev_step(i, carry):
        h, p = carry
        idx = pl.ds(clen - 1 - i, SCAN_CHUNKS, stride=clen)
        a = a_s[1, idx, :]
        h = a * h + b_s[1, idx, :]
        p = a * p
        b_s[1, idx, :] = h
        p_s[1, idx, :] = p
        return h, p

    zeros = jnp.zeros((SCAN_CHUNKS, RG_BW), F32)
    ones = jnp.ones((SCAN_CHUNKS, RG_BW), F32)
    h_f, p_f = lax.fori_loop(0, clen, fwd_step, (zeros, ones))
    h_r, p_r = lax.fori_loop(0, clen, rev_step, (zeros, ones))

    carry = jnp.zeros((1, RG_BW), F32)
    for ch in range(1, SCAN_CHUNKS):
        carry = p_f[ch - 1:ch, :] * carry + h_f[ch - 1:ch, :]
        rows = slice(ch * clen, (ch + 1) * clen)
        b_s[0, rows, :] = b_s[0, rows, :] + p_s[0, rows, :] * carry
    carry = jnp.zeros((1, RG_BW), F32)
    for ch in range(SCAN_CHUNKS - 2, -1, -1):
        carry = p_r[ch + 1:ch + 2, :] * carry + h_r[ch + 1:ch + 2, :]
        rows = slice(ch * clen, (ch + 1) * clen)
        b_s[1, rows, :] = b_s[1, rows, :] + p_s[1, rows, :] * carry

    gx = jax.nn.gelu(gx_ref[0], approximate=True)
    ox_ref[0] = ((b_s[0, n_c:n_t, :] + b_s[1, 0:n_x, :]) * gx).astype(BF16)
    if ctx_out:
        gc = jax.nn.gelu(gc_ref[0], approximate=True)
        oc_ref[0] = ((b_s[0, 0:n_c, :] + b_s[1, n_x:n_t, :]) * gc).astype(BF16)


def _rg_lru(zx, zc, conv_w, conv_b, w_a, b_a, w_x, b_x, lam, ctx_out):
    b, n_x, _ = zx.shape
    n_c = zc.shape[1]
    bw = RG_BW
    col = lambda base: (lambda bi, n: (bi, 0, base // bw + n))
    vec2 = pl.BlockSpec((2, bw), lambda bi, n: (0, n))
    wspec = pl.BlockSpec((2, 1, bw, bw), lambda bi, n: (0, n, 0, 0))
    in_specs = [pl.BlockSpec((1, n_x, bw), col(COL_RG_X)), pl.BlockSpec((1, n_c, bw), col(COL_RG_X)),
                pl.BlockSpec((1, n_x, bw), col(COL_RG_G))]
    args = [zx, zc, zx]
    if ctx_out:
        in_specs.append(pl.BlockSpec((1, n_c, bw), col(COL_RG_G)))
        args.append(zc)
    in_specs += [pl.BlockSpec((4, bw), lambda bi, n: (0, n)), pl.BlockSpec((1, bw), lambda bi, n: (0, n)),
                 wspec, vec2, wspec, vec2, vec2]
    args += [conv_w, conv_b.reshape(1, RG_WIDTH), w_a, b_a, w_x, b_x, lam]
    out_specs = [pl.BlockSpec((1, n_x, bw), lambda bi, n: (bi, 0, n))]
    out_shape = [jax.ShapeDtypeStruct((b, n_x, RG_WIDTH), BF16)]
    if ctx_out:
        out_specs.append(pl.BlockSpec((1, n_c, bw), lambda bi, n: (bi, 0, n)))
        out_shape.append(jax.ShapeDtypeStruct((b, n_c, RG_WIDTH), BF16))
    n_t = n_x + n_c
    outs = pl.pallas_call(
        functools.partial(_rg_kernel, n_x=n_x, n_c=n_c, ctx_out=ctx_out),
        grid=(b, RG_BLOCKS),
        in_specs=in_specs,
        out_specs=out_specs,
        out_shape=out_shape,
        scratch_shapes=[pltpu.VMEM((2, n_t, bw), F32), pltpu.VMEM((2, n_t, bw), F32), pltpu.VMEM((2, n_t, bw), F32)],
        compiler_params=_cparams("parallel", "parallel"),
        name="rg_lru",
    )(*args)
    return outs if ctx_out else (outs[0], None)


def _merge_kernel(ya_ref, yb_ref, yc_ref, ga_ref, gb_ref, gc_ref, wa_ref, wb_ref, wc_ref, o_ref):
    m = jax.nn.sigmoid(ga_ref[0]) * jnp.dot(ya_ref[0], wa_ref[...], preferred_element_type=F32)
    m = m + jax.nn.sigmoid(gb_ref[0]) * jnp.dot(yb_ref[0], wb_ref[...], preferred_element_type=F32)
    m = m + jax.nn.sigmoid(gc_ref[0]) * jnp.dot(yc_ref[0], wc_ref[...], preferred_element_type=F32)
    o_ref[0] = m.astype(BF16)


def _branch_merge(ya, yb, yc, z, w_branch, tm, tn):
    b, t, _ = ya.shape
    d = D_MODEL
    wa, wb, wc = w_branch[:512], w_branch[512:1024], w_branch[1024:]
    gate = lambda k: (lambda bi, i, j: (bi, i, (COL_GATE + k * d) // tn + j))
    return pl.pallas_call(
        _merge_kernel,
        grid=(b, t // tm, d // tn),
        in_specs=[
            pl.BlockSpec((1, tm, 512), lambda bi, i, j: (bi, i, 0)),
            pl.BlockSpec((1, tm, 512), lambda bi, i, j: (bi, i, 0)),
            pl.BlockSpec((1, tm, 1024), lambda bi, i, j: (bi, i, 0)),
            pl.BlockSpec((1, tm, tn), gate(0)),
            pl.BlockSpec((1, tm, tn), gate(1)),
            pl.BlockSpec((1, tm, tn), gate(2)),
            pl.BlockSpec((512, tn), lambda bi, i, j: (0, j)),
            pl.BlockSpec((512, tn), lambda bi, i, j: (0, j)),
            pl.BlockSpec((1024, tn), lambda bi, i, j: (0, j)),
        ],
        out_specs=pl.BlockSpec((1, tm, tn), lambda bi, i, j: (bi, i, j)),
        out_shape=jax.ShapeDtypeStruct((b, t, d), BF16),
        compiler_params=_cparams("parallel", "parallel", "arbitrary"),
        name="branch_merge",
    )(ya, yb, yc, z, z, z, wa, wb, wc)


def _resid_kernel(m_ref, w_ref, x_ref, gt_ref, o_ref):
    o_ref[0] = x_ref[0] + gt_ref[0] * jnp.dot(m_ref[0], w_ref[...], preferred_element_type=F32)


def _out_proj_residual(m, w_out, x, gt, tm, tn):
    b, t, d = x.shape
    return pl.pallas_call(
        _resid_kernel,
        grid=(b, t // tm, d // tn),
        in_specs=[
            pl.BlockSpec((1, tm, d), lambda bi, i, j: (bi, i, 0)),
            pl.BlockSpec((d, tn), lambda bi, i, j: (0, j)),
            pl.BlockSpec((1, tm, tn), lambda bi, i, j: (bi, i, j)),
            pl.BlockSpec((1, 1, tn), lambda bi, i, j: (bi, 0, j)),
        ],
        out_specs=pl.BlockSpec((1, tm, tn), lambda bi, i, j: (bi, i, j)),
        out_shape=jax.ShapeDtypeStruct((b, t, d), F32),
        compiler_params=_cparams("parallel", "parallel", "arbitrary"),
        name="out_proj_residual",
    )(m, w_out, x, gt)


def _router_kernel(x_ref, g_ref, sc_ref, sh_ref, wh_ref, wl_ref, br_ref, h_ref, eid_ref, wt_ref):
    h = _rms(x_ref[0], g_ref[...]) * (1.0 + sc_ref[0]) + sh_ref[0]
    h_ref[0] = h
    hh = h.astype(BF16)
    hl = (h - hh.astype(F32)).astype(BF16)
    logits = (jnp.dot(hh, wh_ref[...], preferred_element_type=F32)
              + jnp.dot(hl, wh_ref[...], preferred_element_type=F32)
              + jnp.dot(hh, wl_ref[...], preferred_element_type=F32)) + br_ref[...]
    lane = lax.broadcasted_iota(jnp.int32, logits.shape, 1)
    lane_f = lane.astype(F32)

    def first_argmax(v, valid):
        vm = jnp.where(valid, v, -jnp.inf)
        mx = jnp.max(vm, axis=-1, keepdims=True)
        idx = jnp.min(jnp.where(valid & (vm == mx), lane_f, 1e9), axis=-1, keepdims=True)
        return mx, idx.astype(jnp.int32)

    is_g = lane < N_GROUPS
    gmax, grp = first_argmax(logits, is_g)
    p_grp = 1.0 / jnp.sum(jnp.where(is_g, jnp.exp(logits - gmax), 0.0), axis=-1, keepdims=True)
    e_lo = N_GROUPS + grp * EXPERTS_PER_GROUP
    in_grp = (lane >= e_lo) & (lane < e_lo + EXPERTS_PER_GROUP)
    v0, i0 = first_argmax(logits, in_grp)
    v1, i1 = first_argmax(logits, in_grp & (lane != i0))
    e1 = jnp.exp(v1 - v0)
    w0 = p_grp / (1.0 + e1)
    w1 = p_grp * e1 / (1.0 + e1)
    eid_ref[0] = jnp.where(lane == 0, i0 - N_GROUPS, jnp.where(lane == 1, i1 - N_GROUPS, 0))
    wt_ref[0] = jnp.where(lane == 0, w0, jnp.where(lane == 1, w1, 0.0))


def _router(x, g, sc, sh, wr_hi, wr_lo, br, tm):
    b, t, d = x.shape
    row = pl.BlockSpec((1, tm, d), lambda bi, i: (bi, i, 0))
    mod = pl.BlockSpec((1, 1, d), lambda bi, i: (bi, 0, 0))
    small = pl.BlockSpec((1, tm, 128), lambda bi, i: (bi, i, 0))
    return pl.pallas_call(
        _router_kernel,
        grid=(b, t // tm),
        in_specs=[row, pl.BlockSpec((1, d), lambda bi, i: (0, 0)), mod, mod,
                  pl.BlockSpec((d, 128), lambda bi, i: (0, 0)), pl.BlockSpec((d, 128), lambda bi, i: (0, 0)),
                  pl.BlockSpec((1, 128), lambda bi, i: (0, 0))],
        out_specs=[row, small, small],
        out_shape=[jax.ShapeDtypeStruct((b, t, d), F32), jax.ShapeDtypeStruct((b, t, 128), jnp.int32),
                   jax.ShapeDtypeStruct((b, t, 128), F32)],
        compiler_params=_cparams("parallel", "parallel"),
        name="moe_router",
    )(x, g.reshape(1, d), sc, sh, wr_hi, wr_lo, br)


def _row_copy(src_hbm, dst_vmem, sem, src_row, dst_row):
    return pltpu.make_async_copy(src_hbm.at[pl.ds(src_row, 1)], dst_vmem.at[pl.ds(dst_row, 1)], sem)


def _expert_kernel(blk_e_ref, tok_ref, x_hbm, w1_ref, w3_ref, w2_ref, o_ref, xbuf, sem):
    del blk_e_ref
    tm = xbuf.shape[0]

    def issue(r, c):
        _row_copy(x_hbm, xbuf, sem, tok_ref[0, 0, r], r).start()
        return c

    def wait(r, c):
        _row_copy(x_hbm, xbuf, sem, 0, r).wait()
        return c

    lax.fori_loop(0, tm, issue, 0)
    lax.fori_loop(0, tm, wait, 0)
    xb = xbuf[...].astype(BF16)
    h1 = jnp.dot(xb, w1_ref[0], preferred_element_type=F32)
    h3 = jnp.dot(xb, w3_ref[0], preferred_element_type=F32)
    hid = (h1 * jax.nn.sigmoid(h1) * h3).astype(BF16)
    o_ref[...] = jnp.dot(hid, w2_ref[0], preferred_element_type=F32)


def _expert_blocks(blk_e, buf_tok, x_rows, w1, w3, w2):
    n_blk = blk_e.shape[0]
    tm = MOE_TM
    d = x_rows.shape[1]
    grid_spec = pltpu.PrefetchScalarGridSpec(
        num_scalar_prefetch=1,
        grid=(n_blk,),
        in_specs=[
            pl.BlockSpec((1, 1, tm), lambda i, e: (i, 0, 0), memory_space=pltpu.SMEM),
            pl.BlockSpec(memory_space=pl.ANY),
            pl.BlockSpec((1, d, D_EXPERT), lambda i, e: (e[i], 0, 0)),
            pl.BlockSpec((1, d, D_EXPERT), lambda i, e: (e[i], 0, 0)),
            pl.BlockSpec((1, D_EXPERT, d), lambda i, e: (e[i], 0, 0)),
        ],
        out_specs=pl.BlockSpec((tm, d), lambda i, e: (i, 0)),
        scratch_shapes=[pltpu.VMEM((tm, d), F32), pltpu.SemaphoreType.DMA(())],
    )
    return pl.pallas_call(
        _expert_kernel,
        grid_spec=grid_spec,
        out_shape=jax.ShapeDtypeStruct((n_blk * tm, d), F32),
        compiler_params=_cparams("arbitrary"),
        name="moe_experts",
    )(blk_e, buf_tok.reshape(n_blk, 1, tm), x_rows, w1, w3, w2)


def _combine_kernel(slot_ref, y_hbm, x_ref, gt_ref, wt_ref, o_ref, buf0, buf1, sem):
    tm = buf0.shape[0]

    def issue(r, c):
        _row_copy(y_hbm, buf0, sem, slot_ref[0, 0, 2 * r], r).start()
        _row_copy(y_hbm, buf1, sem, slot_ref[0, 0, 2 * r + 1], r).start()
        return c

    def wait(r, c):
        _row_copy(y_hbm, buf0, sem, 0, r).wait()
        _row_copy(y_hbm, buf1, sem, 0, r).wait()
        return c

    lax.fori_loop(0, tm, issue, 0)
    lax.fori_loop(0, tm, wait, 0)
    wt = wt_ref[...]
    f = buf0[...] * wt[:, 0:1] + buf1[...] * wt[:, 1:2]
    o_ref[...] = x_ref[...] + gt_ref[0] * f


def _combine(slots, y, x, gt, wt, tok0, tm):
    b, t, d = x.shape
    n_i = t // tm
    x2 = x.reshape(b * t, d)
    blk0 = tok0 // tm
    out = pl.pallas_call(
        _combine_kernel,
        grid=(b, n_i),
        in_specs=[
            pl.BlockSpec((1, 1, 2 * tm), lambda bi, i: (blk0 + bi * n_i + i, 0, 0), memory_space=pltpu.SMEM),
            pl.BlockSpec(memory_space=pl.ANY),
            pl.BlockSpec((tm, d), lambda bi, i: (bi * n_i + i, 0)),
            pl.BlockSpec((1, 1, d), lambda bi, i: (bi, 0, 0)),
            pl.BlockSpec((tm, 128), lambda bi, i: (blk0 + bi * n_i + i, 0)),
        ],
        out_specs=pl.BlockSpec((tm, d), lambda bi, i: (bi * n_i + i, 0)),
        out_shape=jax.ShapeDtypeStruct((b * t, d), F32),
        scratch_shapes=[pltpu.VMEM((tm, d), F32), pltpu.VMEM((tm, d), F32), pltpu.SemaphoreType.DMA(())],
        compiler_params=_cparams("arbitrary", "arbitrary"),
        name="moe_combine",
    )(slots.reshape(-1, 1, 2 * tm), y, x2, gt, wt)
    return out.reshape(b, t, d)


def _routing_tables(eid):
    m = eid.shape[0]
    tm = MOE_TM
    order = jnp.argsort(eid, stable=True).astype(jnp.int32)
    se = eid[order]
    counts = jnp.sum((eid[:, None] == jnp.arange(N_EXPERTS, dtype=jnp.int32)[None, :]).astype(jnp.int32), axis=0)
    starts = jnp.cumsum(counts) - counts
    pcounts = (counts + tm - 1) // tm * tm
    pends = jnp.cumsum(pcounts)
    pstarts = pends - pcounts
    dest = pstarts[se] + jnp.arange(m, dtype=jnp.int32) - starts[se]
    n_blk = (m + N_EXPERTS * (tm - 1) + tm - 1) // tm
    buf_tok = jnp.zeros((n_blk * tm,), jnp.int32).at[dest].set(order // TOP_K)
    slot = jnp.zeros((m,), jnp.int32).at[order].set(dest)
    blk_e = jnp.minimum(jnp.searchsorted(pends, jnp.arange(n_blk, dtype=jnp.int32) * tm, side='right'),
                        N_EXPERTS - 1).astype(jnp.int32)
    return blk_e, buf_tok, slot


def _moe(xs, gs, scs, shs, gts, norm_g, wr_hi, wr_lo, br, w1, w3, w2):
    hs, eids, wts = [], [], []
    for x, sc, sh in zip(xs, scs, shs):
        h, eid, wt = _router(x, norm_g, sc, sh, wr_hi, wr_lo, br, tm=512)
        hs.append(h.reshape(-1, D_MODEL))
        eids.append(eid.reshape(-1, 128)[:, :TOP_K])
        wts.append(wt.reshape(-1, 128))
    h_all = hs[0] if len(hs) == 1 else jnp.concatenate(hs, axis=0)
    eid_all = (eids[0] if len(eids) == 1 else jnp.concatenate(eids, axis=0)).reshape(-1)
    wt_all = wts[0] if len(wts) == 1 else jnp.concatenate(wts, axis=0)
    blk_e, buf_tok, slot = _routing_tables(eid_all)
    y = _expert_blocks(blk_e, buf_tok, h_all, w1, w3, w2)
    outs = []
    tok0 = 0
    for x, gt in zip(xs, gts):
        outs.append(_combine(slot, y, x, gt, wt_all, tok0, tm=256))
        tok0 += x.shape[0] * x.shape[1]
    return outs


def kernel(x, c, ctx, c_ctx, w_mod, b_mod, norm1_g, norm2_g, w_in, na_q_g, na_k_g, na_rpb, df_q_g, df_k_g, df_lam, df_sub_g, rg_conv_w, rg_conv_b, rg_w_a, rg_b_a, rg_w_x, rg_b_x, rg_lam, w_branch, w_out, w_router_g, b_router_g, w_router_e, b_router_e, w1, w3, w2):
    B, S, D = x.shape
    C = ctx.shape[1]
    n_rows = S // GRID_W
    rope_cos, rope_sin = _rope_tables(S)
    dr_idx, dc_idx = _na_bias_indices(n_rows)

    pad = (-(B + 1)) % 8
    rows = jnp.concatenate([c, c_ctx[None, :], jnp.zeros((pad, D), F32)], axis=0)
    mod_all = _modulation(rows, w_mod, b_mod)

    xc = ctx.reshape(1, B * C, D)
    for l in range(DEPTH):
        need_ctx = l < DEPTH - 1
        lam_init = 0.8 - 0.6 * float(np.exp(-0.3 * l))
        mod = mod_all[l]
        sh1, sc1, gt1, sh2, sc2, gt2 = [mod[:B, k * D:(k + 1) * D][:, None, :] for k in range(6)]
        csh1, csc1, cgt1, csh2, csc2, cgt2 = [mod[B:B + 1, k * D:(k + 1) * D][:, None, :] for k in range(6)]

        w_in_l = w_in[l].astype(BF16)
        zx = _norm_mod_matmul(x, norm1_g[l], sc1, sh1, w_in_l, tm=1024, tn=1024)
        n_cc = IN_COLS if need_ctx else KV_COLS
        zc = _norm_mod_matmul(xc, norm1_g[l], csc1, csh1, w_in_l[:, :n_cc], tm=1024, tn=1024).reshape(B, C, n_cc)

        rpb_pad = jnp.concatenate([na_rpb[l], jnp.full((N_HEADS, 2 * NA_WIN_R - 1, 1), NEG, F32)], axis=-1)
        bias = rpb_pad[:, dr_idx, dc_idx].transpose(1, 0, 2, 3)
        y_a = _na_attention(zx, zc, bias, na_q_g[l], na_k_g[l])

        lp = df_lam[l]
        lam = jnp.exp(jnp.sum(lp[0] * lp[1])) - jnp.exp(jnp.sum(lp[2] * lp[3])) + lam_init
        y_b = _df_attention(lam, zx, zx, zc, rope_cos, rope_sin, df_q_g[l], df_k_g[l], df_sub_g[l], 1.0 - lam_init)

        y_c, y_cc = _rg_lru(zx, zc, rg_conv_w[l], rg_conv_b[l], rg_w_a[l].astype(BF16), rg_b_a[l],
                            rg_w_x[l].astype(BF16), rg_b_x[l], rg_lam[l], need_ctx)

        wb_l = w_branch[l].astype(BF16)
        wo_l = w_out[l].astype(BF16)
        m_x = _branch_merge(y_a, y_b, y_c, zx, wb_l, tm=1024, tn=1024)
        x = _out_proj_residual(m_x, wo_l, x, gt1, tm=1024, tn=1024)
        if need_ctx:
            y_ac = _na_ctx_attention(zc, na_q_g[l], na_k_g[l])
            y_bc = _df_attention(lam, zc, None, zc, None, None, df_q_g[l], df_k_g[l], df_sub_g[l], 1.0 - lam_init)
            m_c = _branch_merge(y_ac, y_bc, y_cc, zc, wb_l, tm=C, tn=1024)
            xc = _out_proj_residual(m_c.reshape(1, B * C, D), wo_l, xc, cgt1, tm=1024, tn=1024)

        wr = jnp.concatenate([w_router_g[l], w_router_e[l],
                              jnp.zeros((D, 128 - N_GROUPS - N_EXPERTS), F32)], axis=1)
        wr_hi = wr.astype(BF16)
        wr_lo = (wr - wr_hi.astype(F32)).astype(BF16)
        br = jnp.concatenate([b_router_g[l], b_router_e[l],
                              jnp.zeros((128 - N_GROUPS - N_EXPERTS,), F32)]).reshape(1, 128)
        w1_l, w3_l, w2_l = w1[l].astype(BF16), w3[l].astype(BF16), w2[l].astype(BF16)
        if need_ctx:
            xc, x = _moe([xc, x], None, [csc2, sc2], [csh2, sh2], [cgt2, gt2], norm2_g[l],
                         wr_hi, wr_lo, br, w1_l, w3_l, w2_l)
        else:
            (x,) = _moe([x], None, [sc2], [sh2], [gt2], norm2_g[l], wr_hi, wr_lo, br, w1_l, w3_l, w2_l)
    return x
```

```python
import functools

import jax
import jax.numpy as jnp
import numpy as np
from jax import lax
from jax.experimental import pallas as pl
from jax.experimental.pallas import tpu as pltpu

F32 = jnp.float32
BF16 = jnp.bfloat16

D_MODEL = 2048
DEPTH = 4
GRID_W = 64
HEAD_DIM = 128
N_HEADS = 4
NA_WIN_R = 8
NA_WIN_C = 16
NA_QCB = 16
NA_KCB = 32
DF_DIM = 64
RG_WIDTH = 1024
RG_BLOCKS = 8
RG_BW = 128
RG_C = 8.0
N_GROUPS = 4
EXPERTS_PER_GROUP = 8
N_EXPERTS = 32
TOP_K = 2
D_EXPERT = 512
ROPE_BASE = 10000.0
EPS = 1e-6
NEG = -1e30
LOG2_E = 1.4426950408889634

COL_NA_K, COL_NA_V, COL_DF_K, COL_DF_V, COL_RG_X = 0, 512, 1024, 1536, 2048
COL_NA_Q, COL_DF_Q, COL_RG_G, COL_GATE = 3072, 3584, 4096, 5120
KV_COLS = 3072
MIX_COLS = 5120
IN_COLS = MIX_COLS + 3 * D_MODEL

VMEM_LIMIT_BYTES = 56 * 1024 * 1024

NA_ROWS_PER_STEP = 4
NA_KEY_ROWS = NA_ROWS_PER_STEP + NA_WIN_R - 1
DF_QBLK = 256
MOE_TM = 256
SCAN_CHUNKS = 8
SCAN_PAD_ROWS = 8


def _cparams(*sem):
    return pltpu.CompilerParams(dimension_semantics=sem, vmem_limit_bytes=VMEM_LIMIT_BYTES)


def _rms(x, g):
    return x * lax.rsqrt(jnp.mean(x * x, axis=-1, keepdims=True) + EPS) * g


def _mod_kernel(s_ref, w_ref, b_ref, o_ref):
    s = s_ref[...]
    a = (s * jax.nn.sigmoid(s)).astype(BF16)
    o_ref[0] = jnp.dot(a, w_ref[0].astype(BF16), preferred_element_type=F32) + b_ref[0]


def _modulation(rows, w_mod, b_mod, tn=1024):
    n_l, d, n = w_mod.shape
    r = rows.shape[0]
    return pl.pallas_call(
        _mod_kernel,
        grid=(n_l, n // tn),
        in_specs=[
            pl.BlockSpec((r, d), lambda l, j: (0, 0)),
            pl.BlockSpec((1, d, tn), lambda l, j: (l, 0, j)),
            pl.BlockSpec((1, 1, tn), lambda l, j: (l, 0, j)),
        ],
        out_specs=pl.BlockSpec((1, r, tn), lambda l, j: (l, 0, j)),
        out_shape=jax.ShapeDtypeStruct((n_l, r, n), F32),
        compiler_params=_cparams("parallel", "parallel"),
        name="modulation",
    )(rows, w_mod, b_mod.reshape(n_l, 1, n))


def _nmm_kernel(x_ref, g_ref, sc_ref, sh_ref, w_ref, o_ref, h_ref):
    @pl.when(pl.program_id(2) == 0)
    def _():
        h = _rms(x_ref[0], g_ref[...]) * (1.0 + sc_ref[0]) + sh_ref[0]
        h_ref[...] = h.astype(BF16)

    o_ref[0] = jnp.dot(h_ref[...], w_ref[...], preferred_element_type=F32)


def _norm_mod_matmul(x, g, sc, sh, w, tm, tn):
    b, t, d = x.shape
    n = w.shape[1]
    return pl.pallas_call(
        _nmm_kernel,
        grid=(b, t // tm, n // tn),
        in_specs=[
            pl.BlockSpec((1, tm, d), lambda bi, i, j: (bi, i, 0)),
            pl.BlockSpec((1, d), lambda bi, i, j: (0, 0)),
            pl.BlockSpec((1, 1, d), lambda bi, i, j: (bi, 0, 0)),
            pl.BlockSpec((1, 1, d), lambda bi, i, j: (bi, 0, 0)),
            pl.BlockSpec((d, tn), lambda bi, i, j: (0, j)),
        ],
        out_specs=pl.BlockSpec((1, tm, tn), lambda bi, i, j: (bi, i, j)),
        out_shape=jax.ShapeDtypeStruct((b, t, n), F32),
        scratch_shapes=[pltpu.VMEM((tm, d), BF16)],
        compiler_params=_cparams("parallel", "parallel", "arbitrary"),
        name="norm_mod_proj",
    )(x, g.reshape(1, d), sc, sh, w)


def _na_window_start(r0):
    lower = jnp.clip(r0 - NA_WIN_R // 2, 0, (2048 // GRID_W) - NA_WIN_R)
    return jnp.minimum(lower, (2048 // GRID_W) - NA_KEY_ROWS)


def _na_bias(rpb, n_rows):
    rb, kw, w = NA_ROWS_PER_STEP, NA_KEY_ROWS, GRID_W
    n_dr = 2 * NA_WIN_R - 1
    qc = np.arange(w)
    kc = np.arange(w)
    qwin = np.clip(qc - NA_WIN_C // 2, 0, w - NA_WIN_C)
    kcol_start = np.clip((qc // NA_QCB) * NA_QCB - (NA_KCB - NA_QCB) // 2, 0, w - NA_KCB)
    col_ok = (kc[None, :] >= qwin[:, None]) & (kc[None, :] < qwin[:, None] + NA_WIN_C)
    col_ok &= (kc[None, :] >= kcol_start[:, None]) & (kc[None, :] < kcol_start[:, None] + NA_KCB)
    edge = w - NA_WIN_C
    ext = jnp.concatenate([jnp.repeat(rpb[..., :1], edge, axis=-1), rpb, jnp.repeat(rpb[..., -1:], edge, axis=-1)],
                          axis=-1)
    toep = jnp.stack([ext[..., w - 1 - q:2 * w - 1 - q] for q in range(w)], axis=-2)
    toep = jnp.where(col_ok, toep, NEG)
    toep = jnp.concatenate([toep, jnp.full(toep.shape[:-3] + (1, w, w), NEG, F32)], axis=-3)
    rows = np.arange(n_rows)
    row_start = np.clip(rows - NA_WIN_R // 2, 0, n_rows - NA_WIN_R)
    n_steps = n_rows // rb
    blk = np.full((n_steps, rb, kw), n_dr, np.int32)
    for s in range(n_steps):
        ws = min(int(np.clip(s * rb - NA_WIN_R // 2, 0, n_rows - NA_WIN_R)), n_rows - kw)
        for j in range(rb):
            r = s * rb + j
            for i in range(kw):
                if row_start[r] <= ws + i < row_start[r] + NA_WIN_R:
                    blk[s, j, i] = ws + i - r + (NA_WIN_R - 1)
    lead = toep.shape[:-4]
    nl = len(lead)
    out = jnp.take(toep, jnp.asarray(blk.reshape(-1)), axis=-3)
    out = out.reshape(lead + (N_HEADS, n_steps, rb, kw, w, w))
    perm = tuple(range(nl)) + (nl + 1, nl, nl + 2, nl + 4, nl + 3, nl + 5)
    return out.transpose(perm).reshape(lead + (n_steps, N_HEADS, rb * w, kw * w))


def _na_kernel(q_ref, k_ref, v_ref, kc_ref, vc_ref, bias_ref, gq_ref, gk_ref, o_ref, kn, vn, knc, vnc):
    step = pl.program_id(1)

    @pl.when(step == 0)
    def _():
        for h in range(N_HEADS):
            sl = slice(h * HEAD_DIM, (h + 1) * HEAD_DIM)
            kn[:, sl] = _rms(k_ref[0, :, sl], gk_ref[...]).astype(BF16)
            knc[:, sl] = _rms(kc_ref[0, :, sl], gk_ref[...]).astype(BF16)
        vn[...] = v_ref[0].astype(BF16)
        vnc[...] = vc_ref[0].astype(BF16)

    ws = _na_window_start(step * NA_ROWS_PER_STEP)
    start = pl.multiple_of(ws * GRID_W, GRID_W)
    nk = NA_KEY_ROWS * GRID_W
    scale = HEAD_DIM ** -0.5
    nt = (((1,), (1,)), ((), ()))
    for h in range(N_HEADS):
        sl = slice(h * HEAD_DIM, (h + 1) * HEAD_DIM)
        qn = (_rms(q_ref[0, :, sl], gq_ref[...]) * scale).astype(BF16)
        s_loc = lax.dot_general(qn, kn[pl.ds(start, nk), sl], nt, preferred_element_type=F32) + bias_ref[0, h]
        s_ctx = lax.dot_general(qn, knc[:, sl], nt, preferred_element_type=F32)
        m = jnp.maximum(jnp.max(s_loc, axis=-1, keepdims=True), jnp.max(s_ctx, axis=-1, keepdims=True))
        p_loc = jnp.exp(s_loc - m)
        p_ctx = jnp.exp(s_ctx - m)
        l = jnp.sum(p_loc, axis=-1, keepdims=True) + jnp.sum(p_ctx, axis=-1, keepdims=True)
        o = jnp.dot(p_loc.astype(BF16), vn[pl.ds(start, nk), sl], preferred_element_type=F32)
        o = o + jnp.dot(p_ctx.astype(BF16), vnc[:, sl], preferred_element_type=F32)
        o_ref[0, :, sl] = (o * (1.0 / l)).astype(BF16)


def _na_attention(zx, zc, bias, gq, gk):
    b, s, _ = zx.shape
    c = zc.shape[1]
    w = N_HEADS * HEAD_DIM
    tq = NA_ROWS_PER_STEP * GRID_W
    nk = NA_KEY_ROWS * GRID_W
    return pl.pallas_call(
        _na_kernel,
        grid=(b, s // tq),
        in_specs=[
            pl.BlockSpec((1, tq, w), lambda bi, i: (bi, i, COL_NA_Q // w)),
            pl.BlockSpec((1, s, w), lambda bi, i: (bi, 0, COL_NA_K // w)),
            pl.BlockSpec((1, s, w), lambda bi, i: (bi, 0, COL_NA_V // w)),
            pl.BlockSpec((1, c, w), lambda bi, i: (bi, 0, COL_NA_K // w)),
            pl.BlockSpec((1, c, w), lambda bi, i: (bi, 0, COL_NA_V // w)),
            pl.BlockSpec((1, N_HEADS, tq, nk), lambda bi, i: (i, 0, 0, 0)),
            pl.BlockSpec((1, HEAD_DIM), lambda bi, i: (0, 0)),
            pl.BlockSpec((1, HEAD_DIM), lambda bi, i: (0, 0)),
        ],
        out_specs=pl.BlockSpec((1, tq, w), lambda bi, i: (bi, i, 0)),
        out_shape=jax.ShapeDtypeStruct((b, s, w), BF16),
        scratch_shapes=[pltpu.VMEM((s, w), BF16), pltpu.VMEM((s, w), BF16),
                        pltpu.VMEM((c, w), BF16), pltpu.VMEM((c, w), BF16)],
        compiler_params=_cparams("parallel", "arbitrary"),
        name="na_attention",
    )(zx, zx, zx, zc, zc, bias, gq.reshape(1, HEAD_DIM), gk.reshape(1, HEAD_DIM))


def _na_ctx_kernel(q_ref, k_ref, v_ref, gq_ref, gk_ref, o_ref):
    scale = HEAD_DIM ** -0.5
    nt = (((1,), (1,)), ((), ()))
    for h in range(N_HEADS):
        sl = slice(h * HEAD_DIM, (h + 1) * HEAD_DIM)
        qn = (_rms(q_ref[0, :, sl], gq_ref[...]) * scale).astype(BF16)
        kn = _rms(k_ref[0, :, sl], gk_ref[...]).astype(BF16)
        s = lax.dot_general(qn, kn, nt, preferred_element_type=F32)
        p = jnp.exp(s - jnp.max(s, axis=-1, keepdims=True))
        l = jnp.sum(p, axis=-1, keepdims=True)
        o = jnp.dot(p.astype(BF16), v_ref[0, :, sl].astype(BF16), preferred_element_type=F32)
        o_ref[0, :, sl] = (o * (1.0 / l)).astype(BF16)


def _na_ctx_attention(zc, gq, gk):
    b, c, _ = zc.shape
    w = N_HEADS * HEAD_DIM
    return pl.pallas_call(
        _na_ctx_kernel,
        grid=(b,),
        in_specs=[
            pl.BlockSpec((1, c, w), lambda bi: (bi, 0, COL_NA_Q // w)),
            pl.BlockSpec((1, c, w), lambda bi: (bi, 0, COL_NA_K // w)),
            pl.BlockSpec((1, c, w), lambda bi: (bi, 0, COL_NA_V // w)),
            pl.BlockSpec((1, HEAD_DIM), lambda bi: (0, 0)),
            pl.BlockSpec((1, HEAD_DIM), lambda bi: (0, 0)),
        ],
        out_specs=pl.BlockSpec((1, c, w), lambda bi: (bi, 0, 0)),
        out_shape=jax.ShapeDtypeStruct((b, c, w), BF16),
        compiler_params=_cparams("parallel"),
        name="na_ctx_attention",
    )(zc, zc, zc, gq.reshape(1, HEAD_DIM), gk.reshape(1, HEAD_DIM))


def _rope_tables(n_tok):
    t = np.arange(n_tok)
    pos = np.stack([t // GRID_W, t % GRID_W], axis=0).astype(np.float32)
    n_freq = DF_DIM // 4
    inv = (np.float32(ROPE_BASE) ** (-np.arange(n_freq, dtype=np.float32) / n_freq)).astype(np.float32)
    ang = jnp.asarray(pos[:, :, None] * inv)
    cos, sin = jnp.cos(ang), jnp.sin(ang)
    cos64 = jnp.concatenate([cos[0], cos[0], cos[1], cos[1]], axis=-1)
    sin64 = jnp.concatenate([-sin[0], sin[0], -sin[1], sin[1]], axis=-1)
    return jnp.concatenate([cos64, cos64], axis=-1), jnp.concatenate([sin64, sin64], axis=-1)


def _df_kernel(lam_ref, q_ref, *refs, n_x, n_c, rope, out_scale):
    if n_x:
        kx_ref, vx_ref, kc_ref, vc_ref, cq_ref, sq_ref, ck_ref, sk_ref, gq_ref, gk_ref, gs_ref, o_ref, kn, vn = refs
    else:
        kc_ref, vc_ref, gq_ref, gk_ref, gs_ref, o_ref, kn, vn = refs
    lane = lax.broadcasted_iota(jnp.int32, (1, 2 * DF_DIM), 1)
    lo = lane < DF_DIM
    first = (lane % (DF_DIM // 2)) < (DF_DIM // 4)

    def norm64(x, g):
        x2 = x * x
        s0 = jnp.sum(jnp.where(lo, x2, 0.0), axis=-1, keepdims=True)
        s1 = jnp.sum(jnp.where(lo, 0.0, x2), axis=-1, keepdims=True)
        ms = jnp.where(lo, s0, s1) * (1.0 / DF_DIM)
        return x * lax.rsqrt(ms + EPS) * g

    def rot(x, cos, sin):
        partner = jnp.where(first, pltpu.roll(x, 2 * DF_DIM - DF_DIM // 4, 1), pltpu.roll(x, DF_DIM // 4, 1))
        return x * cos + partner * sin

    @pl.when(pl.program_id(2) == 0)
    def _():
        if n_x:
            kx = rot(norm64(kx_ref[0], gk_ref[...]), ck_ref[...], sk_ref[...])
            kn[0:n_x, :] = kx.astype(BF16)
            vn[0:n_x, :] = vx_ref[0].astype(BF16)
        kn[n_x:n_x + n_c, :] = norm64(kc_ref[0], gk_ref[...]).astype(BF16)
        vn[n_x:n_x + n_c, :] = vc_ref[0].astype(BF16)

    q = norm64(q_ref[0], gq_ref[...])
    if rope:
        q = rot(q, cq_ref[...], sq_ref[...])
    q = q * (DF_DIM ** -0.5 * LOG2_E)
    nt = (((1,), (1,)), ((), ()))

    def softmax_v(qm):
        s = lax.dot_general(qm.astype(BF16), kn[...], nt, preferred_element_type=F32)
        p = jnp.exp2(s - jnp.max(s, axis=-1, keepdims=True))
        l = jnp.sum(p, axis=-1, keepdims=True)
        return jnp.dot(p.astype(BF16), vn[...], preferred_element_type=F32), l

    o0, l0 = softmax_v(jnp.where(lo, q, 0.0))
    o1, l1 = softmax_v(jnp.where(lo, 0.0, q))
    o = o0 * (1.0 / l0) - o1 * (lam_ref[0] / l1)
    o_ref[0] = (_rms(o, gs_ref[...]) * out_scale).astype(BF16)


def _df_attention(lam, zq, zx, zc, cos, sin, gq, gk, gs, out_scale):
    b, t, _ = zq.shape
    c = zc.shape[1]
    n_x = 0 if zx is None else zx.shape[1]
    w = 2 * DF_DIM
    tq = min(DF_QBLK, t)
    gq2 = jnp.concatenate([gq, gq]).reshape(1, w)
    gk2 = jnp.concatenate([gk, gk]).reshape(1, w)
    col = lambda base: (lambda bi, h, i: (bi, 0, base // w + h))
    vec = pl.BlockSpec((1, w), lambda bi, h, i: (0, 0))
    in_specs = [pl.BlockSpec(memory_space=pltpu.SMEM),
                pl.BlockSpec((1, tq, w), lambda bi, h, i: (bi, i, COL_DF_Q // w + h))]
    args = [lam.reshape(1), zq]
    if n_x:
        in_specs += [pl.BlockSpec((1, n_x, w), col(COL_DF_K)), pl.BlockSpec((1, n_x, w), col(COL_DF_V))]
        args += [zx, zx]
    in_specs += [pl.BlockSpec((1, c, w), col(COL_DF_K)), pl.BlockSpec((1, c, w), col(COL_DF_V))]
    args += [zc, zc]
    if n_x:
        in_specs += [pl.BlockSpec((tq, w), lambda bi, h, i: (i, 0)), pl.BlockSpec((tq, w), lambda bi, h, i: (i, 0)),
                     pl.BlockSpec((n_x, w), lambda bi, h, i: (0, 0)), pl.BlockSpec((n_x, w), lambda bi, h, i: (0, 0))]
        args += [cos, sin, cos, sin]
    in_specs += [vec, vec, vec]
    args += [gq2, gk2, gs.reshape(1, w)]
    return pl.pallas_call(
        functools.partial(_df_kernel, n_x=n_x, n_c=c, rope=bool(n_x), out_scale=out_scale),
        grid=(b, N_HEADS, t // tq),
        in_specs=in_specs,
        out_specs=pl.BlockSpec((1, tq, w), lambda bi, h, i: (bi, i, h)),
        out_shape=jax.ShapeDtypeStruct((b, t, N_HEADS * w), BF16),
        scratch_shapes=[pltpu.VMEM((n_x + c, w), BF16), pltpu.VMEM((n_x + c, w), BF16)],
        compiler_params=_cparams("parallel", "parallel", "arbitrary"),
        name="diff_attention" if n_x else "diff_ctx_attention",
    )(*args)


def _rg_kernel(*refs, n_x, n_c, ctx_out):
    if ctx_out:
        (ux_ref, uc_ref, gx_ref, gc_ref, cw_ref, cb_ref, wa_ref, ba_ref, wx_ref, bx_ref, lam_ref,
         ox_ref, oc_ref, a_s, b_s, p_s) = refs
    else:
        (ux_ref, uc_ref, gx_ref, cw_ref, cb_ref, wa_ref, ba_ref, wx_ref, bx_ref, lam_ref,
         ox_ref, a_s, b_s, p_s) = refs
    n_t = n_x + n_c
    clen = n_t // SCAN_CHUNKS
    cstride = clen + SCAN_PAD_ROWS

    def conv(z):
        n = z.shape[0]
        t = lax.broadcasted_iota(jnp.int32, (n, 1), 0)
        zm2 = jnp.where(t >= 2, pltpu.roll(z, 2, 0), 0.0)
        zm1 = jnp.where(t >= 1, pltpu.roll(z, 1, 0), 0.0)
        zp1 = jnp.where(t < n - 1, pltpu.roll(z, n - 1, 0), 0.0)
        return (zm2 * cw_ref[0:1, :] + zm1 * cw_ref[1:2, :] + z * cw_ref[2:3, :] + zp1 * cw_ref[3:4, :]
                + cb_ref[...])

    def pieces(t0, n):
        out, t = [], t0
        while t < t0 + n:
            ch = t // clen
            stop = min((ch + 1) * clen, t0 + n)
            out.append((t - t0, ch * cstride + (t - ch * clen), stop - t))
            t = stop
        return out

    def put(ref, d, t0, val):
        for off, row, ln in pieces(t0, val.shape[0]):
            ref[d, row:row + ln, :] = val[off:off + ln]

    def get(ref, d, t0, n):
        return jnp.concatenate([ref[d, row:row + ln, :] for _, row, ln in pieces(t0, n)], axis=0)

    def coeffs(u, d, t0):
        ub = u.astype(BF16)
        r = jax.nn.sigmoid(jnp.dot(ub, wa_ref[d, 0], preferred_element_type=F32) + ba_ref[d:d + 1, :])
        gi = jax.nn.sigmoid(jnp.dot(ub, wx_ref[d, 0], preferred_element_type=F32) + bx_ref[d:d + 1, :])
        log_a = (-RG_C) * r * jax.nn.softplus(-lam_ref[d:d + 1, :])
        a = jnp.exp(log_a)
        put(a_s, d, t0, a)
        put(b_s, d, t0, jnp.sqrt(-jnp.tanh(log_a) * (a * a + 1.0)) * (gi * u))

    u_c = conv(uc_ref[0])
    coeffs(u_c, 0, 0)
    coeffs(u_c, 1, n_x)
    u_x = conv(ux_ref[0])
    coeffs(u_x, 0, n_c)
    coeffs(u_x, 1, 0)

    def step(tau, carry):
        h_f, p_f, h_r, p_r = carry
        i_f = pl.ds(tau, SCAN_CHUNKS, stride=cstride)
        i_r = pl.ds(clen - 1 - tau, SCAN_CHUNKS, stride=cstride)
        a_f = a_s[0, i_f, :]
        a_r = a_s[1, i_r, :]
        h_f = a_f * h_f + b_s[0, i_f, :]
        h_r = a_r * h_r + b_s[1, i_r, :]
        p_f = a_f * p_f
        p_r = a_r * p_r
        b_s[0, i_f, :] = h_f
        b_s[1, i_r, :] = h_r
        p_s[0, i_f, :] = p_f
        p_s[1, i_r, :] = p_r
        return h_f, p_f, h_r, p_r

    zeros = jnp.zeros((SCAN_CHUNKS, RG_BW), F32)
    ones = jnp.ones((SCAN_CHUNKS, RG_BW), F32)
    h_f, p_f, h_r, p_r = lax.fori_loop(0, clen, step, (zeros, ones, zeros, ones), unroll=2)

    carry = jnp.zeros((1, RG_BW), F32)
    for ch in range(1, SCAN_CHUNKS):
        carry = p_f[ch - 1:ch, :] * carry + h_f[ch - 1:ch, :]
        rows = slice(ch * cstride, ch * cstride + clen)
        b_s[0, rows, :] = b_s[0, rows, :] + p_s[0, rows, :] * carry
    carry = jnp.zeros((1, RG_BW), F32)
    for ch in range(SCAN_CHUNKS - 2, -1, -1):
        carry = p_r[ch + 1:ch + 2, :] * carry + h_r[ch + 1:ch + 2, :]
        rows = slice(ch * cstride, ch * cstride + clen)
        b_s[1, rows, :] = b_s[1, rows, :] + p_s[1, rows, :] * carry

    gx = jax.nn.gelu(gx_ref[0], approximate=True)
    ox_ref[0] = ((get(b_s, 0, n_c, n_x) + get(b_s, 1, 0, n_x)) * gx).astype(BF16)
    if ctx_out:
        gc = jax.nn.gelu(gc_ref[0], approximate=True)
        oc_ref[0] = ((get(b_s, 0, 0, n_c) + get(b_s, 1, n_x, n_c)) * gc).astype(BF16)


def _rg_lru(zx, zc, conv_w, conv_b, w_a, b_a, w_x, b_x, lam, ctx_out):
    b, n_x, _ = zx.shape
    n_c = zc.shape[1]
    bw = RG_BW
    col = lambda base: (lambda bi, n: (bi, 0, base // bw + n))
    vec2 = pl.BlockSpec((2, bw), lambda bi, n: (0, n))
    wspec = pl.BlockSpec((2, 1, bw, bw), lambda bi, n: (0, n, 0, 0))
    in_specs = [pl.BlockSpec((1, n_x, bw), col(COL_RG_X)), pl.BlockSpec((1, n_c, bw), col(COL_RG_X)),
                pl.BlockSpec((1, n_x, bw), col(COL_RG_G))]
    args = [zx, zc, zx]
    if ctx_out:
        in_specs.append(pl.BlockSpec((1, n_c, bw), col(COL_RG_G)))
        args.append(zc)
    in_specs += [pl.BlockSpec((4, bw), lambda bi, n: (0, n)), pl.BlockSpec((1, bw), lambda bi, n: (0, n)),
                 wspec, vec2, wspec, vec2, vec2]
    args += [conv_w, conv_b.reshape(1, RG_WIDTH), w_a, b_a, w_x, b_x, lam]
    out_specs = [pl.BlockSpec((1, n_x, bw), lambda bi, n: (bi, 0, n))]
    out_shape = [jax.ShapeDtypeStruct((b, n_x, RG_WIDTH), BF16)]
    if ctx_out:
        out_specs.append(pl.BlockSpec((1, n_c, bw), lambda bi, n: (bi, 0, n)))
        out_shape.append(jax.ShapeDtypeStruct((b, n_c, RG_WIDTH), BF16))
    n_s = SCAN_CHUNKS * ((n_x + n_c) // SCAN_CHUNKS + SCAN_PAD_ROWS)
    outs = pl.pallas_call(
        functools.partial(_rg_kernel, n_x=n_x, n_c=n_c, ctx_out=ctx_out),
        grid=(b, RG_BLOCKS),
        in_specs=in_specs,
        out_specs=out_specs,
        out_shape=out_shape,
        scratch_shapes=[pltpu.VMEM((2, n_s, bw), F32), pltpu.VMEM((2, n_s, bw), F32), pltpu.VMEM((2, n_s, bw), F32)],
        compiler_params=_cparams("parallel", "parallel"),
        name="rg_lru",
    )(*args)
    return outs if ctx_out else (outs[0], None)


def _merge_kernel(ya_ref, yb_ref, yc_ref, ga_ref, gb_ref, gc_ref, wa_ref, wb_ref, wc_ref, o_ref):
    m = jax.nn.sigmoid(ga_ref[0]) * jnp.dot(ya_ref[0], wa_ref[...], preferred_element_type=F32)
    m = m + jax.nn.sigmoid(gb_ref[0]) * jnp.dot(yb_ref[0], wb_ref[...], preferred_element_type=F32)
    m = m + jax.nn.sigmoid(gc_ref[0]) * jnp.dot(yc_ref[0], wc_ref[...], preferred_element_type=F32)
    o_ref[0] = m.astype(BF16)


def _branch_merge(ya, yb, yc, z, w_branch, tm, tn):
    b, t, _ = ya.shape
    d = D_MODEL
    wa, wb, wc = w_branch[:512], w_branch[512:1024], w_branch[1024:]
    gate = lambda k: (lambda bi, i, j: (bi, i, (COL_GATE + k * d) // tn + j))
    return pl.pallas_call(
        _merge_kernel,
        grid=(b, t // tm, d // tn),
        in_specs=[
            pl.BlockSpec((1, tm, 512), lambda bi, i, j: (bi, i, 0)),
            pl.BlockSpec((1, tm, 512), lambda bi, i, j: (bi, i, 0)),
            pl.BlockSpec((1, tm, 1024), lambda bi, i, j: (bi, i, 0)),
            pl.BlockSpec((1, tm, tn), gate(0)),
            pl.BlockSpec((1, tm, tn), gate(1)),
            pl.BlockSpec((1, tm, tn), gate(2)),
            pl.BlockSpec((512, tn), lambda bi, i, j: (0, j)),
            pl.BlockSpec((512, tn), lambda bi, i, j: (0, j)),
            pl.BlockSpec((1024, tn), lambda bi, i, j: (0, j)),
        ],
        out_specs=pl.BlockSpec((1, tm, tn), lambda bi, i, j: (bi, i, j)),
        out_shape=jax.ShapeDtypeStruct((b, t, d), BF16),
        compiler_params=_cparams("parallel", "parallel", "arbitrary"),
        name="branch_merge",
    )(ya, yb, yc, z, z, z, wa, wb, wc)


def _resid_kernel(m_ref, w_ref, x_ref, gt_ref, o_ref):
    o_ref[0] = x_ref[0] + gt_ref[0] * jnp.dot(m_ref[0], w_ref[...], preferred_element_type=F32)


def _out_proj_residual(m, w_out, x, gt, tm, tn):
    b, t, d = x.shape
    return pl.pallas_call(
        _resid_kernel,
        grid=(b, t // tm, d // tn),
        in_specs=[
            pl.BlockSpec((1, tm, d), lambda bi, i, j: (bi, i, 0)),
            pl.BlockSpec((d, tn), lambda bi, i, j: (0, j)),
            pl.BlockSpec((1, tm, tn), lambda bi, i, j: (bi, i, j)),
            pl.BlockSpec((1, 1, tn), lambda bi, i, j: (bi, 0, j)),
        ],
        out_specs=pl.BlockSpec((1, tm, tn), lambda bi, i, j: (bi, i, j)),
        out_shape=jax.ShapeDtypeStruct((b, t, d), F32),
        compiler_params=_cparams("parallel", "parallel", "arbitrary"),
        name="out_proj_residual",
    )(m, w_out, x, gt)


def _router_kernel(x_ref, g_ref, sc_ref, sh_ref, wh_ref, wl_ref, br_ref, h_ref, eid_ref, wt_ref):
    h = _rms(x_ref[0], g_ref[...]) * (1.0 + sc_ref[0]) + sh_ref[0]
    h_ref[0] = h
    hh = h.astype(BF16)
    hl = (h - hh.astype(F32)).astype(BF16)
    logits = (jnp.dot(hh, wh_ref[...], preferred_element_type=F32)
              + jnp.dot(hl, wh_ref[...], preferred_element_type=F32)
              + jnp.dot(hh, wl_ref[...], preferred_element_type=F32)) + br_ref[...]
    lane = lax.broadcasted_iota(jnp.int32, logits.shape, 1)
    lane_f = lane.astype(F32)

    def first_argmax(v, valid):
        vm = jnp.where(valid, v, -jnp.inf)
        mx = jnp.max(vm, axis=-1, keepdims=True)
        idx = jnp.min(jnp.where(valid & (vm == mx), lane_f, 1e9), axis=-1, keepdims=True)
        return mx, idx.astype(jnp.int32)

    is_g = lane < N_GROUPS
    gmax, grp = first_argmax(logits, is_g)
    p_grp = 1.0 / jnp.sum(jnp.where(is_g, jnp.exp(logits - gmax), 0.0), axis=-1, keepdims=True)
    e_lo = N_GROUPS + grp * EXPERTS_PER_GROUP
    in_grp = (lane >= e_lo) & (lane < e_lo + EXPERTS_PER_GROUP)
    v0, i0 = first_argmax(logits, in_grp)
    v1, i1 = first_argmax(logits, in_grp & (lane != i0))
    e1 = jnp.exp(v1 - v0)
    w0 = p_grp / (1.0 + e1)
    w1 = p_grp * e1 / (1.0 + e1)
    eid_ref[0] = jnp.where(lane == 0, i0 - N_GROUPS, jnp.where(lane == 1, i1 - N_GROUPS, 0))
    wt_ref[0] = jnp.where(lane == 0, w0, jnp.where(lane == 1, w1, 0.0))


def _router(x, g, sc, sh, wr_hi, wr_lo, br, tm):
    b, t, d = x.shape
    row = pl.BlockSpec((1, tm, d), lambda bi, i: (bi, i, 0))
    mod = pl.BlockSpec((1, 1, d), lambda bi, i: (bi, 0, 0))
    small = pl.BlockSpec((1, tm, 128), lambda bi, i: (bi, i, 0))
    return pl.pallas_call(
        _router_kernel,
        grid=(b, t // tm),
        in_specs=[row, pl.BlockSpec((1, d), lambda bi, i: (0, 0)), mod, mod,
                  pl.BlockSpec((d, 128), lambda bi, i: (0, 0)), pl.BlockSpec((d, 128), lambda bi, i: (0, 0)),
                  pl.BlockSpec((1, 128), lambda bi, i: (0, 0))],
        out_specs=[row, small, small],
        out_shape=[jax.ShapeDtypeStruct((b, t, d), F32), jax.ShapeDtypeStruct((b, t, 128), jnp.int32),
                   jax.ShapeDtypeStruct((b, t, 128), F32)],
        compiler_params=_cparams("parallel", "parallel"),
        name="moe_router",
    )(x, g.reshape(1, d), sc, sh, wr_hi, wr_lo, br)


def _row_copy(src_hbm, dst_vmem, sem, src_row, dst_row):
    return pltpu.make_async_copy(src_hbm.at[pl.ds(src_row, 1)], dst_vmem.at[pl.ds(dst_row, 1)], sem)


def _expert_kernel(blk_e_ref, tok_ref, tok_next_ref, x_hbm, w1_ref, w3_ref, w2_ref, o_ref, xbuf, sem):
    del blk_e_ref
    tm = xbuf.shape[1]
    i = pl.program_id(0)
    cur = i % 2

    def gather(idx_ref, buf):
        def issue(r, c):
            _row_copy(x_hbm, xbuf.at[buf], sem.at[buf], idx_ref[0, 0, r], r).start()
            return c
        lax.fori_loop(0, tm, issue, 0, unroll=8)

    @pl.when(i == 0)
    def _():
        gather(tok_ref, 0)

    @pl.when(i + 1 < pl.num_programs(0))
    def _():
        gather(tok_next_ref, 1 - cur)

    def wait(r, c):
        _row_copy(x_hbm, xbuf.at[cur], sem.at[cur], 0, r).wait()
        return c

    lax.fori_loop(0, tm, wait, 0, unroll=8)
    xb = xbuf[cur].astype(BF16)
    h1 = jnp.dot(xb, w1_ref[0], preferred_element_type=F32)
    h3 = jnp.dot(xb, w3_ref[0], preferred_element_type=F32)
    hid = (h1 * jax.nn.sigmoid(h1) * h3).astype(BF16)
    o_ref[...] = jnp.dot(hid, w2_ref[0], preferred_element_type=F32)


def _expert_blocks(blk_e, buf_tok, x_rows, w1, w3, w2):
    n_blk = blk_e.shape[0]
    tm = MOE_TM
    d = x_rows.shape[1]
    grid_spec = pltpu.PrefetchScalarGridSpec(
        num_scalar_prefetch=1,
        grid=(n_blk,),
        in_specs=[
            pl.BlockSpec((1, 1, tm), lambda i, e: (i, 0, 0), memory_space=pltpu.SMEM),
            pl.BlockSpec((1, 1, tm), lambda i, e: (jnp.minimum(i + 1, n_blk - 1), 0, 0), memory_space=pltpu.SMEM),
            pl.BlockSpec(memory_space=pl.ANY),
            pl.BlockSpec((1, d, D_EXPERT), lambda i, e: (e[i], 0, 0)),
            pl.BlockSpec((1, d, D_EXPERT), lambda i, e: (e[i], 0, 0)),
            pl.BlockSpec((1, D_EXPERT, d), lambda i, e: (e[i], 0, 0)),
        ],
        out_specs=pl.BlockSpec((tm, d), lambda i, e: (i, 0)),
        scratch_shapes=[pltpu.VMEM((2, tm, d), F32), pltpu.SemaphoreType.DMA((2,))],
    )
    tok = buf_tok.reshape(n_blk, 1, tm)
    return pl.pallas_call(
        _expert_kernel,
        grid_spec=grid_spec,
        out_shape=jax.ShapeDtypeStruct((n_blk * tm, d), F32),
        compiler_params=_cparams("arbitrary"),
        name="moe_experts",
    )(blk_e, tok, tok, x_rows, w1, w3, w2)


def _combine_kernel(slot_ref, y_hbm, x_ref, gt_ref, wt_ref, o_ref, buf0, buf1, sem):
    tm = buf0.shape[0]

    def issue(r, c):
        _row_copy(y_hbm, buf0, sem, slot_ref[0, 0, 2 * r], r).start()
        _row_copy(y_hbm, buf1, sem, slot_ref[0, 0, 2 * r + 1], r).start()
        return c

    def wait(r, c):
        _row_copy(y_hbm, buf0, sem, 0, r).wait()
        _row_copy(y_hbm, buf1, sem, 0, r).wait()
        return c

    lax.fori_loop(0, tm, issue, 0)
    lax.fori_loop(0, tm, wait, 0)
    wt = wt_ref[...]
    f = buf0[...] * wt[:, 0:1] + buf1[...] * wt[:, 1:2]
    o_ref[...] = x_ref[...] + gt_ref[0] * f


def _combine(slots, y, x, gt, wt, tok0, tm):
    b, t, d = x.shape
    n_i = t // tm
    x2 = x.reshape(b * t, d)
    blk0 = tok0 // tm
    out = pl.pallas_call(
        _combine_kernel,
        grid=(b, n_i),
        in_specs=[
            pl.BlockSpec((1, 1, 2 * tm), lambda bi, i: (blk0 + bi * n_i + i, 0, 0), memory_space=pltpu.SMEM),
            pl.BlockSpec(memory_space=pl.ANY),
            pl.BlockSpec((tm, d), lambda bi, i: (bi * n_i + i, 0)),
            pl.BlockSpec((1, 1, d), lambda bi, i: (bi, 0, 0)),
            pl.BlockSpec((tm, 128), lambda bi, i: (blk0 + bi * n_i + i, 0)),
        ],
        out_specs=pl.BlockSpec((tm, d), lambda bi, i: (bi * n_i + i, 0)),
        out_shape=jax.ShapeDtypeStruct((b * t, d), F32),
        scratch_shapes=[pltpu.VMEM((tm, d), F32), pltpu.VMEM((tm, d), F32), pltpu.SemaphoreType.DMA(())],
        compiler_params=_cparams("arbitrary", "arbitrary"),
        name="moe_combine",
    )(slots.reshape(-1, 1, 2 * tm), y, x2, gt, wt)
    return out.reshape(b, t, d)


def _routing_tables(eid):
    m = eid.shape[0]
    tm = MOE_TM
    i32 = jnp.int32
    iota = jnp.arange(m, dtype=i32)
    se, order = lax.sort_key_val(eid, iota)
    onehot = (se[:, None] == jnp.arange(N_EXPERTS, dtype=i32)[None, :]).astype(i32)
    counts = jnp.sum(onehot, axis=0)
    starts = jnp.cumsum(counts) - counts
    pcounts = (counts + tm - 1) // tm * tm
    pends = jnp.cumsum(pcounts)
    pstarts = pends - pcounts
    dest = iota + jnp.sum(onehot * (pstarts - starts)[None, :], axis=1)
    _, slot = lax.sort_key_val(order, dest)
    n_blk = (m + N_EXPERTS * (tm - 1) + tm - 1) // tm
    blk_start = jnp.arange(n_blk, dtype=i32) * tm
    blk_e = jnp.minimum(jnp.sum((pends[None, :] <= blk_start[:, None]).astype(i32), axis=1), N_EXPERTS - 1)
    off = (blk_start - pstarts[blk_e])[:, None] + jnp.arange(tm, dtype=i32)[None, :]
    valid = off < counts[blk_e][:, None]
    src = jnp.clip(starts[blk_e][:, None] + off, 0, m - 1)
    buf_tok = jnp.where(valid, order[src] // TOP_K, 0).reshape(-1)
    return blk_e.astype(i32), buf_tok, slot


def _moe(xs, scs, shs, gts, norm_g, wr_hi, wr_lo, br, w1, w3, w2):
    hs, eids, wts = [], [], []
    for x, sc, sh in zip(xs, scs, shs):
        h, eid, wt = _router(x, norm_g, sc, sh, wr_hi, wr_lo, br, tm=512)
        hs.append(h.reshape(-1, D_MODEL))
        eids.append(eid.reshape(-1, 128)[:, :TOP_K])
        wts.append(wt.reshape(-1, 128))
    h_all = hs[0] if len(hs) == 1 else jnp.concatenate(hs, axis=0)
    eid_all = (eids[0] if len(eids) == 1 else jnp.concatenate(eids, axis=0)).reshape(-1)
    wt_all = wts[0] if len(wts) == 1 else jnp.concatenate(wts, axis=0)
    blk_e, buf_tok, slot = _routing_tables(eid_all)
    y = _expert_blocks(blk_e, buf_tok, h_all, w1, w3, w2)
    outs = []
    tok0 = 0
    for x, gt in zip(xs, gts):
        outs.append(_combine(slot, y, x, gt, wt_all, tok0, tm=256))
        tok0 += x.shape[0] * x.shape[1]
    return outs


def kernel(x, c, ctx, c_ctx, w_mod, b_mod, norm1_g, norm2_g, w_in, na_q_g, na_k_g, na_rpb, df_q_g, df_k_g, df_lam, df_sub_g, rg_conv_w, rg_conv_b, rg_w_a, rg_b_a, rg_w_x, rg_b_x, rg_lam, w_branch, w_out, w_router_g, b_router_g, w_router_e, b_router_e, w1, w3, w2):
    B, S, D = x.shape
    C = ctx.shape[1]
    n_rows = S // GRID_W
    rope_cos, rope_sin = _rope_tables(S)
    na_bias = _na_bias(na_rpb, n_rows)

    pad = (-(B + 1)) % 8
    rows = jnp.concatenate([c, c_ctx[None, :], jnp.zeros((pad, D), F32)], axis=0)
    mod_all = _modulation(rows, w_mod, b_mod)

    xc = ctx.reshape(1, B * C, D)
    for l in range(DEPTH):
        need_ctx = l < DEPTH - 1
        lam_init = 0.8 - 0.6 * float(np.exp(-0.3 * l))
        mod = mod_all[l]
        sh1, sc1, gt1, sh2, sc2, gt2 = [mod[:B, k * D:(k + 1) * D][:, None, :] for k in range(6)]
        csh1, csc1, cgt1, csh2, csc2, cgt2 = [mod[B:B + 1, k * D:(k + 1) * D][:, None, :] for k in range(6)]

        w_in_l = w_in[l].astype(BF16)
        zx = _norm_mod_matmul(x, norm1_g[l], sc1, sh1, w_in_l, tm=1024, tn=1024)
        n_cc = IN_COLS if need_ctx else KV_COLS
        zc = _norm_mod_matmul(xc, norm1_g[l], csc1, csh1, w_in_l[:, :n_cc], tm=1024, tn=1024).reshape(B, C, n_cc)

        y_a = _na_attention(zx, zc, na_bias[l], na_q_g[l], na_k_g[l])

        lp = df_lam[l]
        lam = jnp.exp(jnp.sum(lp[0] * lp[1])) - jnp.exp(jnp.sum(lp[2] * lp[3])) + lam_init
        y_b = _df_attention(lam, zx, zx, zc, rope_cos, rope_sin, df_q_g[l], df_k_g[l], df_sub_g[l], 1.0 - lam_init)

        y_c, y_cc = _rg_lru(zx, zc, rg_conv_w[l], rg_conv_b[l], rg_w_a[l].astype(BF16), rg_b_a[l],
                            rg_w_x[l].astype(BF16), rg_b_x[l], rg_lam[l], need_ctx)

        wb_l = w_branch[l].astype(BF16)
        wo_l = w_out[l].astype(BF16)
        m_x = _branch_merge(y_a, y_b, y_c, zx, wb_l, tm=1024, tn=1024)
        x = _out_proj_residual(m_x, wo_l, x, gt1, tm=1024, tn=1024)
        if need_ctx:
            y_ac = _na_ctx_attention(zc, na_q_g[l], na_k_g[l])
            y_bc = _df_attention(lam, zc, None, zc, None, None, df_q_g[l], df_k_g[l], df_sub_g[l], 1.0 - lam_init)
            m_c = _branch_merge(y_ac, y_bc, y_cc, zc, wb_l, tm=C, tn=1024)
            xc = _out_proj_residual(m_c.reshape(1, B * C, D), wo_l, xc, cgt1, tm=1024, tn=1024)

        wr = jnp.concatenate([w_router_g[l], w_router_e[l],
                              jnp.zeros((D, 128 - N_GROUPS - N_EXPERTS), F32)], axis=1)
        wr_hi = wr.astype(BF16)
        wr_lo = (wr - wr_hi.astype(F32)).astype(BF16)
        br = jnp.concatenate([b_router_g[l], b_router_e[l],
                              jnp.zeros((128 - N_GROUPS - N_EXPERTS,), F32)]).reshape(1, 128)
        w1_l, w3_l, w2_l = w1[l].astype(BF16), w3[l].astype(BF16), w2[l].astype(BF16)
        if need_ctx:
            xc, x = _moe([xc, x], [csc2, sc2], [csh2, sh2], [cgt2, gt2], norm2_g[l],
                         wr_hi, wr_lo, br, w1_l, w3_l, w2_l)
        else:
            (x,) = _moe([x], [sc2], [sh2], [gt2], norm2_g[l], wr_hi, wr_lo, br, w1_l, w3_l, w2_l)
    return x
```

```python
import functools

import jax
import jax.numpy as jnp
import numpy as np
from jax import lax
from jax.experimental import pallas as pl
from jax.experimental.pallas import tpu as pltpu

F32 = jnp.float32
BF16 = jnp.bfloat16

D_MODEL = 2048
DEPTH = 4
GRID_W = 64
HEAD_DIM = 128
N_HEADS = 4
NA_WIN_R = 8
NA_WIN_C = 16
NA_QCB = 16
NA_KCB = 32
DF_DIM = 64
RG_WIDTH = 1024
RG_BLOCKS = 8
RG_BW = 128
RG_C = 8.0
N_GROUPS = 4
EXPERTS_PER_GROUP = 8
N_EXPERTS = 32
TOP_K = 2
D_EXPERT = 512
ROPE_BASE = 10000.0
EPS = 1e-6
NEG = -1e30
LOG2_E = 1.4426950408889634

COL_NA_K, COL_NA_V, COL_DF_K, COL_DF_V, COL_RG_X = 0, 512, 1024, 1536, 2048
COL_NA_Q, COL_DF_Q, COL_RG_G, COL_GATE = 3072, 3584, 4096, 5120
KV_COLS = 3072
MIX_COLS = 5120
IN_COLS = MIX_COLS + 3 * D_MODEL

VMEM_LIMIT_BYTES = 56 * 1024 * 1024

NA_ROWS_PER_STEP = 4
NA_KEY_ROWS = NA_ROWS_PER_STEP + NA_WIN_R - 1
DF_QBLK = 1024
DF_KCHUNK = 768
MOE_TM = 256
SCAN_CHUNKS = 8
SCAN_PAD_ROWS = 8


def _cparams(*sem):
    return pltpu.CompilerParams(dimension_semantics=sem, vmem_limit_bytes=VMEM_LIMIT_BYTES)


def _rms(x, g):
    return x * lax.rsqrt(jnp.mean(x * x, axis=-1, keepdims=True) + EPS) * g


def _mod_kernel(s_ref, w_ref, b_ref, o_ref):
    s = s_ref[...]
    a = (s * jax.nn.sigmoid(s)).astype(BF16)
    o_ref[0] = jnp.dot(a, w_ref[0].astype(BF16), preferred_element_type=F32) + b_ref[0]


def _modulation(rows, w_mod, b_mod, tn=1024):
    n_l, d, n = w_mod.shape
    r = rows.shape[0]
    return pl.pallas_call(
        _mod_kernel,
        grid=(n_l, n // tn),
        in_specs=[
            pl.BlockSpec((r, d), lambda l, j: (0, 0)),
            pl.BlockSpec((1, d, tn), lambda l, j: (l, 0, j)),
            pl.BlockSpec((1, 1, tn), lambda l, j: (l, 0, j)),
        ],
        out_specs=pl.BlockSpec((1, r, tn), lambda l, j: (l, 0, j)),
        out_shape=jax.ShapeDtypeStruct((n_l, r, n), F32),
        compiler_params=_cparams("parallel", "parallel"),
        name="modulation",
    )(rows, w_mod, b_mod.reshape(n_l, 1, n))


def _nmm_kernel(x_ref, g_ref, sc_ref, sh_ref, w_ref, o_ref, h_ref):
    @pl.when(pl.program_id(2) == 0)
    def _():
        h = _rms(x_ref[0], g_ref[...]) * (1.0 + sc_ref[0]) + sh_ref[0]
        h_ref[...] = h.astype(BF16)

    o_ref[0] = jnp.dot(h_ref[...], w_ref[...], preferred_element_type=F32)


def _norm_mod_matmul(x, g, sc, sh, w, tm, tn):
    b, t, d = x.shape
    n = w.shape[1]
    return pl.pallas_call(
        _nmm_kernel,
        grid=(b, t // tm, n // tn),
        in_specs=[
            pl.BlockSpec((1, tm, d), lambda bi, i, j: (bi, i, 0)),
            pl.BlockSpec((1, d), lambda bi, i, j: (0, 0)),
            pl.BlockSpec((1, 1, d), lambda bi, i, j: (bi, 0, 0)),
            pl.BlockSpec((1, 1, d), lambda bi, i, j: (bi, 0, 0)),
            pl.BlockSpec((d, tn), lambda bi, i, j: (0, j)),
        ],
        out_specs=pl.BlockSpec((1, tm, tn), lambda bi, i, j: (bi, i, j)),
        out_shape=jax.ShapeDtypeStruct((b, t, n), F32),
        scratch_shapes=[pltpu.VMEM((tm, d), BF16)],
        compiler_params=_cparams("parallel", "parallel", "arbitrary"),
        name="norm_mod_proj",
    )(x, g.reshape(1, d), sc, sh, w)


def _na_window_start(r0):
    lower = jnp.clip(r0 - NA_WIN_R // 2, 0, (2048 // GRID_W) - NA_WIN_R)
    return jnp.minimum(lower, (2048 // GRID_W) - NA_KEY_ROWS)


def _na_bias(rpb, n_rows):
    rb, kw, w = NA_ROWS_PER_STEP, NA_KEY_ROWS, GRID_W
    n_dr = 2 * NA_WIN_R - 1
    qc = np.arange(w)
    kc = np.arange(w)
    qwin = np.clip(qc - NA_WIN_C // 2, 0, w - NA_WIN_C)
    kcol_start = np.clip((qc // NA_QCB) * NA_QCB - (NA_KCB - NA_QCB) // 2, 0, w - NA_KCB)
    col_ok = (kc[None, :] >= qwin[:, None]) & (kc[None, :] < qwin[:, None] + NA_WIN_C)
    col_ok &= (kc[None, :] >= kcol_start[:, None]) & (kc[None, :] < kcol_start[:, None] + NA_KCB)
    edge = w - NA_WIN_C
    ext = jnp.concatenate([jnp.repeat(rpb[..., :1], edge, axis=-1), rpb, jnp.repeat(rpb[..., -1:], edge, axis=-1)],
                          axis=-1)
    toep = jnp.stack([ext[..., w - 1 - q:2 * w - 1 - q] for q in range(w)], axis=-2)
    toep = jnp.where(col_ok, toep, NEG)
    toep = jnp.concatenate([toep, jnp.full(toep.shape[:-3] + (1, w, w), NEG, F32)], axis=-3)
    rows = np.arange(n_rows)
    row_start = np.clip(rows - NA_WIN_R // 2, 0, n_rows - NA_WIN_R)
    n_steps = n_rows // rb
    blk = np.full((n_steps, rb, kw), n_dr, np.int32)
    for s in range(n_steps):
        ws = min(int(np.clip(s * rb - NA_WIN_R // 2, 0, n_rows - NA_WIN_R)), n_rows - kw)
        for j in range(rb):
            r = s * rb + j
            for i in range(kw):
                if row_start[r] <= ws + i < row_start[r] + NA_WIN_R:
                    blk[s, j, i] = ws + i - r + (NA_WIN_R - 1)
    lead = toep.shape[:-4]
    nl = len(lead)
    out = jnp.take(toep, jnp.asarray(blk.reshape(-1)), axis=-3)
    out = out.reshape(lead + (N_HEADS, n_steps, rb, kw, w, w))
    perm = tuple(range(nl)) + (nl + 1, nl, nl + 2, nl + 4, nl + 3, nl + 5)
    return out.transpose(perm).reshape(lead + (n_steps, N_HEADS, rb * w, kw * w))


def _na_kernel(q_ref, k_ref, v_ref, kc_ref, vc_ref, bias_ref, gq_ref, gk_ref, o_ref, kn, vn, knc, vnc):
    step = pl.program_id(1)

    @pl.when(step == 0)
    def _():
        for h in range(N_HEADS):
            sl = slice(h * HEAD_DIM, (h + 1) * HEAD_DIM)
            kn[:, sl] = _rms(k_ref[0, :, sl], gk_ref[...]).astype(BF16)
            knc[:, sl] = _rms(kc_ref[0, :, sl], gk_ref[...]).astype(BF16)
        vn[...] = v_ref[0].astype(BF16)
        vnc[...] = vc_ref[0].astype(BF16)

    ws = _na_window_start(step * NA_ROWS_PER_STEP)
    start = pl.multiple_of(ws * GRID_W, GRID_W)
    nk = NA_KEY_ROWS * GRID_W
    scale = HEAD_DIM ** -0.5
    nt = (((1,), (1,)), ((), ()))
    for h in range(N_HEADS):
        sl = slice(h * HEAD_DIM, (h + 1) * HEAD_DIM)
        qn = (_rms(q_ref[0, :, sl], gq_ref[...]) * scale).astype(BF16)
        s_loc = lax.dot_general(qn, kn[pl.ds(start, nk), sl], nt, preferred_element_type=F32) + bias_ref[0, h]
        s_ctx = lax.dot_general(qn, knc[:, sl], nt, preferred_element_type=F32)
        m = jnp.maximum(jnp.max(s_loc, axis=-1, keepdims=True), jnp.max(s_ctx, axis=-1, keepdims=True))
        p_loc = jnp.exp(s_loc - m)
        p_ctx = jnp.exp(s_ctx - m)
        l = jnp.sum(p_loc, axis=-1, keepdims=True) + jnp.sum(p_ctx, axis=-1, keepdims=True)
        o = jnp.dot(p_loc.astype(BF16), vn[pl.ds(start, nk), sl], preferred_element_type=F32)
        o = o + jnp.dot(p_ctx.astype(BF16), vnc[:, sl], preferred_element_type=F32)
        o_ref[0, :, sl] = (o * (1.0 / l)).astype(BF16)


def _na_attention(zx, zc, bias, gq, gk):
    b, s, _ = zx.shape
    c = zc.shape[1]
    w = N_HEADS * HEAD_DIM
    tq = NA_ROWS_PER_STEP * GRID_W
    nk = NA_KEY_ROWS * GRID_W
    return pl.pallas_call(
        _na_kernel,
        grid=(b, s // tq),
        in_specs=[
            pl.BlockSpec((1, tq, w), lambda bi, i: (bi, i, COL_NA_Q // w)),
            pl.BlockSpec((1, s, w), lambda bi, i: (bi, 0, COL_NA_K // w)),
            pl.BlockSpec((1, s, w), lambda bi, i: (bi, 0, COL_NA_V // w)),
            pl.BlockSpec((1, c, w), lambda bi, i: (bi, 0, COL_NA_K // w)),
            pl.BlockSpec((1, c, w), lambda bi, i: (bi, 0, COL_NA_V // w)),
            pl.BlockSpec((1, N_HEADS, tq, nk), lambda bi, i: (i, 0, 0, 0)),
            pl.BlockSpec((1, HEAD_DIM), lambda bi, i: (0, 0)),
            pl.BlockSpec((1, HEAD_DIM), lambda bi, i: (0, 0)),
        ],
        out_specs=pl.BlockSpec((1, tq, w), lambda bi, i: (bi, i, 0)),
        out_shape=jax.ShapeDtypeStruct((b, s, w), BF16),
        scratch_shapes=[pltpu.VMEM((s, w), BF16), pltpu.VMEM((s, w), BF16),
                        pltpu.VMEM((c, w), BF16), pltpu.VMEM((c, w), BF16)],
        compiler_params=_cparams("parallel", "arbitrary"),
        name="na_attention",
    )(zx, zx, zx, zc, zc, bias, gq.reshape(1, HEAD_DIM), gk.reshape(1, HEAD_DIM))


def _na_ctx_kernel(q_ref, k_ref, v_ref, gq_ref, gk_ref, o_ref):
    scale = HEAD_DIM ** -0.5
    nt = (((1,), (1,)), ((), ()))
    for h in range(N_HEADS):
        sl = slice(h * HEAD_DIM, (h + 1) * HEAD_DIM)
        qn = (_rms(q_ref[0, :, sl], gq_ref[...]) * scale).astype(BF16)
        kn = _rms(k_ref[0, :, sl], gk_ref[...]).astype(BF16)
        s = lax.dot_general(qn, kn, nt, preferred_element_type=F32)
        p = jnp.exp(s - jnp.max(s, axis=-1, keepdims=True))
        l = jnp.sum(p, axis=-1, keepdims=True)
        o = jnp.dot(p.astype(BF16), v_ref[0, :, sl].astype(BF16), preferred_element_type=F32)
        o_ref[0, :, sl] = (o * (1.0 / l)).astype(BF16)


def _na_ctx_attention(zc, gq, gk):
    b, c, _ = zc.shape
    w = N_HEADS * HEAD_DIM
    return pl.pallas_call(
        _na_ctx_kernel,
        grid=(b,),
        in_specs=[
            pl.BlockSpec((1, c, w), lambda bi: (bi, 0, COL_NA_Q // w)),
            pl.BlockSpec((1, c, w), lambda bi: (bi, 0, COL_NA_K // w)),
            pl.BlockSpec((1, c, w), lambda bi: (bi, 0, COL_NA_V // w)),
            pl.BlockSpec((1, HEAD_DIM), lambda bi: (0, 0)),
            pl.BlockSpec((1, HEAD_DIM), lambda bi: (0, 0)),
        ],
        out_specs=pl.BlockSpec((1, c, w), lambda bi: (bi, 0, 0)),
        out_shape=jax.ShapeDtypeStruct((b, c, w), BF16),
        compiler_params=_cparams("parallel"),
        name="na_ctx_attention",
    )(zc, zc, zc, gq.reshape(1, HEAD_DIM), gk.reshape(1, HEAD_DIM))


def _rope_tables(n_tok):
    t = np.arange(n_tok)
    pos = np.stack([t // GRID_W, t % GRID_W], axis=0).astype(np.float32)
    n_freq = DF_DIM // 4
    inv = (np.float32(ROPE_BASE) ** (-np.arange(n_freq, dtype=np.float32) / n_freq)).astype(np.float32)
    ang = jnp.asarray(pos[:, :, None] * inv)
    cos, sin = jnp.cos(ang), jnp.sin(ang)
    cos64 = jnp.concatenate([cos[0], cos[0], cos[1], cos[1]], axis=-1)
    sin64 = jnp.concatenate([-sin[0], sin[0], -sin[1], sin[1]], axis=-1)
    return jnp.concatenate([cos64, cos64], axis=-1), jnp.concatenate([sin64, sin64], axis=-1)


def _df_kernel(lam_ref, q_ref, *refs, n_x, n_c, rope, out_scale):
    if n_x:
        kx_ref, vx_ref, kc_ref, vc_ref, cq_ref, sq_ref, ck_ref, sk_ref, gq_ref, gk_ref, gs_ref, o_ref, kn, vn = refs
    else:
        kc_ref, vc_ref, gq_ref, gk_ref, gs_ref, o_ref, kn, vn = refs
    lane = lax.broadcasted_iota(jnp.int32, (1, 2 * DF_DIM), 1)
    lo = lane < DF_DIM
    first = (lane % (DF_DIM // 2)) < (DF_DIM // 4)

    def norm64(x, g):
        x2 = x * x
        s0 = jnp.sum(jnp.where(lo, x2, 0.0), axis=-1, keepdims=True)
        s1 = jnp.sum(jnp.where(lo, 0.0, x2), axis=-1, keepdims=True)
        ms = jnp.where(lo, s0, s1) * (1.0 / DF_DIM)
        return x * lax.rsqrt(ms + EPS) * g

    def rot(x, cos, sin):
        partner = jnp.where(first, pltpu.roll(x, 2 * DF_DIM - DF_DIM // 4, 1), pltpu.roll(x, DF_DIM // 4, 1))
        return x * cos + partner * sin

    @pl.when(pl.program_id(2) == 0)
    def _():
        if n_x:
            kx = rot(norm64(kx_ref[0], gk_ref[...]), ck_ref[...], sk_ref[...])
            kn[0:n_x, :] = kx.astype(BF16)
            vn[0:n_x, :] = vx_ref[0].astype(BF16)
        kn[n_x:n_x + n_c, :] = norm64(kc_ref[0], gk_ref[...]).astype(BF16)
        vn[n_x:n_x + n_c, :] = vc_ref[0].astype(BF16)

    q = norm64(q_ref[0], gq_ref[...])
    if rope:
        q = rot(q, cq_ref[...], sq_ref[...])
    q = q * (DF_DIM ** -0.5 * LOG2_E)
    nt = (((1,), (1,)), ((), ()))

    n_k = n_x + n_c
    qs = (jnp.where(lo, q, 0.0).astype(BF16), jnp.where(lo, 0.0, q).astype(BF16))
    tq = q.shape[0]
    ms = [jnp.full((tq, 1), -jnp.inf, F32)] * 2
    ls = [jnp.zeros((tq, 1), F32)] * 2
    accs = [jnp.zeros((tq, 2 * DF_DIM), F32)] * 2
    for c0 in range(0, n_k, DF_KCHUNK):
        c1 = min(c0 + DF_KCHUNK, n_k)
        for mi in range(2):
            s = lax.dot_general(qs[mi], kn[c0:c1, :], nt, preferred_element_type=F32)
            m_new = jnp.maximum(ms[mi], jnp.max(s, axis=-1, keepdims=True))
            alpha = jnp.exp2(ms[mi] - m_new)
            p = jnp.exp2(s - m_new)
            ls[mi] = alpha * ls[mi] + jnp.sum(p, axis=-1, keepdims=True)
            accs[mi] = alpha * accs[mi] + jnp.dot(p.astype(BF16), vn[c0:c1, :], preferred_element_type=F32)
            ms[mi] = m_new
    o = accs[0] * (1.0 / ls[0]) - accs[1] * (lam_ref[0] / ls[1])
    o_ref[0] = (_rms(o, gs_ref[...]) * out_scale).astype(BF16)


def _df_attention(lam, zq, zx, zc, cos, sin, gq, gk, gs, out_scale):
    b, t, _ = zq.shape
    c = zc.shape[1]
    n_x = 0 if zx is None else zx.shape[1]
    w = 2 * DF_DIM
    tq = min(DF_QBLK, t)
    gq2 = jnp.concatenate([gq, gq]).reshape(1, w)
    gk2 = jnp.concatenate([gk, gk]).reshape(1, w)
    col = lambda base: (lambda bi, h, i: (bi, 0, base // w + h))
    vec = pl.BlockSpec((1, w), lambda bi, h, i: (0, 0))
    in_specs = [pl.BlockSpec(memory_space=pltpu.SMEM),
                pl.BlockSpec((1, tq, w), lambda bi, h, i: (bi, i, COL_DF_Q // w + h))]
    args = [lam.reshape(1), zq]
    if n_x:
        in_specs += [pl.BlockSpec((1, n_x, w), col(COL_DF_K)), pl.BlockSpec((1, n_x, w), col(COL_DF_V))]
        args += [zx, zx]
    in_specs += [pl.BlockSpec((1, c, w), col(COL_DF_K)), pl.BlockSpec((1, c, w), col(COL_DF_V))]
    args += [zc, zc]
    if n_x:
        in_specs += [pl.BlockSpec((tq, w), lambda bi, h, i: (i, 0)), pl.BlockSpec((tq, w), lambda bi, h, i: (i, 0)),
                     pl.BlockSpec((n_x, w), lambda bi, h, i: (0, 0)), pl.BlockSpec((n_x, w), lambda bi, h, i: (0, 0))]
        args += [cos, sin, cos, sin]
    in_specs += [vec, vec, vec]
    args += [gq2, gk2, gs.reshape(1, w)]
    return pl.pallas_call(
        functools.partial(_df_kernel, n_x=n_x, n_c=c, rope=bool(n_x), out_scale=out_scale),
        grid=(b, N_HEADS, t // tq),
        in_specs=in_specs,
        out_specs=pl.BlockSpec((1, tq, w), lambda bi, h, i: (bi, i, h)),
        out_shape=jax.ShapeDtypeStruct((b, t, N_HEADS * w), BF16),
        scratch_shapes=[pltpu.VMEM((n_x + c, w), BF16), pltpu.VMEM((n_x + c, w), BF16)],
        compiler_params=_cparams("parallel", "parallel", "arbitrary"),
        name="diff_attention" if n_x else "diff_ctx_attention",
    )(*args)


def _rg_kernel(*refs, n_x, n_c, ctx_out):
    if ctx_out:
        (ux_ref, uc_ref, gx_ref, gc_ref, cw_ref, cb_ref, wa_ref, ba_ref, wx_ref, bx_ref, lam_ref,
         ox_ref, oc_ref, a_s, b_s, p_s) = refs
    else:
        (ux_ref, uc_ref, gx_ref, cw_ref, cb_ref, wa_ref, ba_ref, wx_ref, bx_ref, lam_ref,
         ox_ref, a_s, b_s, p_s) = refs
    n_t = n_x + n_c
    clen = n_t // SCAN_CHUNKS
    cstride = clen + SCAN_PAD_ROWS

    def conv(z):
        n = z.shape[0]
        t = lax.broadcasted_iota(jnp.int32, (n, 1), 0)
        zm2 = jnp.where(t >= 2, pltpu.roll(z, 2, 0), 0.0)
        zm1 = jnp.where(t >= 1, pltpu.roll(z, 1, 0), 0.0)
        zp1 = jnp.where(t < n - 1, pltpu.roll(z, n - 1, 0), 0.0)
        return (zm2 * cw_ref[0:1, :] + zm1 * cw_ref[1:2, :] + z * cw_ref[2:3, :] + zp1 * cw_ref[3:4, :]
                + cb_ref[...])

    def pieces(t0, n):
        out, t = [], t0
        while t < t0 + n:
            ch = t // clen
            stop = min((ch + 1) * clen, t0 + n)
            out.append((t - t0, ch * cstride + (t - ch * clen), stop - t))
            t = stop
        return out

    def put(ref, d, t0, val):
        for off, row, ln in pieces(t0, val.shape[0]):
            ref[d, row:row + ln, :] = val[off:off + ln]

    def get(ref, d, t0, n):
        return jnp.concatenate([ref[d, row:row + ln, :] for _, row, ln in pieces(t0, n)], axis=0)

    def coeffs(u, d, t0):
        ub = u.astype(BF16)
        r = jax.nn.sigmoid(jnp.dot(ub, wa_ref[d, 0], preferred_element_type=F32) + ba_ref[d:d + 1, :])
        gi = jax.nn.sigmoid(jnp.dot(ub, wx_ref[d, 0], preferred_element_type=F32) + bx_ref[d:d + 1, :])
        log_a = (-RG_C) * r * jax.nn.softplus(-lam_ref[d:d + 1, :])
        a = jnp.exp(log_a)
        put(a_s, d, t0, a)
        put(b_s, d, t0, jnp.sqrt(-jnp.tanh(log_a) * (a * a + 1.0)) * (gi * u))

    u_c = conv(uc_ref[0])
    coeffs(u_c, 0, 0)
    coeffs(u_c, 1, n_x)
    u_x = conv(ux_ref[0])
    coeffs(u_x, 0, n_c)
    coeffs(u_x, 1, 0)

    def step(tau, carry):
        h_f, p_f, h_r, p_r = carry
        i_f = pl.ds(tau, SCAN_CHUNKS, stride=cstride)
        i_r = pl.ds(clen - 1 - tau, SCAN_CHUNKS, stride=cstride)
        a_f = a_s[0, i_f, :]
        a_r = a_s[1, i_r, :]
        h_f = a_f * h_f + b_s[0, i_f, :]
        h_r = a_r * h_r + b_s[1, i_r, :]
        p_f = a_f * p_f
        p_r = a_r * p_r
        b_s[0, i_f, :] = h_f
        b_s[1, i_r, :] = h_r
        p_s[0, i_f, :] = p_f
        p_s[1, i_r, :] = p_r
        return h_f, p_f, h_r, p_r

    zeros = jnp.zeros((SCAN_CHUNKS, RG_BW), F32)
    ones = jnp.ones((SCAN_CHUNKS, RG_BW), F32)
    h_f, p_f, h_r, p_r = lax.fori_loop(0, clen, step, (zeros, ones, zeros, ones), unroll=2)

    carry = jnp.zeros((1, RG_BW), F32)
    for ch in range(1, SCAN_CHUNKS):
        carry = p_f[ch - 1:ch, :] * carry + h_f[ch - 1:ch, :]
        rows = slice(ch * cstride, ch * cstride + clen)
        b_s[0, rows, :] = b_s[0, rows, :] + p_s[0, rows, :] * carry
    carry = jnp.zeros((1, RG_BW), F32)
    for ch in range(SCAN_CHUNKS - 2, -1, -1):
        carry = p_r[ch + 1:ch + 2, :] * carry + h_r[ch + 1:ch + 2, :]
        rows = slice(ch * cstride, ch * cstride + clen)
        b_s[1, rows, :] = b_s[1, rows, :] + p_s[1, rows, :] * carry

    gx = jax.nn.gelu(gx_ref[0], approximate=True)
    ox_ref[0] = ((get(b_s, 0, n_c, n_x) + get(b_s, 1, 0, n_x)) * gx).astype(BF16)
    if ctx_out:
        gc = jax.nn.gelu(gc_ref[0], approximate=True)
        oc_ref[0] = ((get(b_s, 0, 0, n_c) + get(b_s, 1, n_x, n_c)) * gc).astype(BF16)


def _rg_lru(zx, zc, conv_w, conv_b, w_a, b_a, w_x, b_x, lam, ctx_out):
    b, n_x, _ = zx.shape
    n_c = zc.shape[1]
    bw = RG_BW
    col = lambda base: (lambda bi, n: (bi, 0, base // bw + n))
    vec2 = pl.BlockSpec((2, bw), lambda bi, n: (0, n))
    wspec = pl.BlockSpec((2, 1, bw, bw), lambda bi, n: (0, n, 0, 0))
    in_specs = [pl.BlockSpec((1, n_x, bw), col(COL_RG_X)), pl.BlockSpec((1, n_c, bw), col(COL_RG_X)),
                pl.BlockSpec((1, n_x, bw), col(COL_RG_G))]
    args = [zx, zc, zx]
    if ctx_out:
        in_specs.append(pl.BlockSpec((1, n_c, bw), col(COL_RG_G)))
        args.append(zc)
    in_specs += [pl.BlockSpec((4, bw), lambda bi, n: (0, n)), pl.BlockSpec((1, bw), lambda bi, n: (0, n)),
                 wspec, vec2, wspec, vec2, vec2]
    args += [conv_w, conv_b.reshape(1, RG_WIDTH), w_a, b_a, w_x, b_x, lam]
    out_specs = [pl.BlockSpec((1, n_x, bw), lambda bi, n: (bi, 0, n))]
    out_shape = [jax.ShapeDtypeStruct((b, n_x, RG_WIDTH), BF16)]
    if ctx_out:
        out_specs.append(pl.BlockSpec((1, n_c, bw), lambda bi, n: (bi, 0, n)))
        out_shape.append(jax.ShapeDtypeStruct((b, n_c, RG_WIDTH), BF16))
    n_s = SCAN_CHUNKS * ((n_x + n_c) // SCAN_CHUNKS + SCAN_PAD_ROWS)
    outs = pl.pallas_call(
        functools.partial(_rg_kernel, n_x=n_x, n_c=n_c, ctx_out=ctx_out),
        grid=(b, RG_BLOCKS),
        in_specs=in_specs,
        out_specs=out_specs,
        out_shape=out_shape,
        scratch_shapes=[pltpu.VMEM((2, n_s, bw), F32), pltpu.VMEM((2, n_s, bw), F32), pltpu.VMEM((2, n_s, bw), F32)],
        compiler_params=_cparams("parallel", "parallel"),
        name="rg_lru",
    )(*args)
    return outs if ctx_out else (outs[0], None)


def _merge_kernel(ya_ref, yb_ref, yc_ref, ga_ref, gb_ref, gc_ref, wa_ref, wb_ref, wc_ref, o_ref):
    m = jax.nn.sigmoid(ga_ref[0]) * jnp.dot(ya_ref[0], wa_ref[...], preferred_element_type=F32)
    m = m + jax.nn.sigmoid(gb_ref[0]) * jnp.dot(yb_ref[0], wb_ref[...], preferred_element_type=F32)
    m = m + jax.nn.sigmoid(gc_ref[0]) * jnp.dot(yc_ref[0], wc_ref[...], preferred_element_type=F32)
    o_ref[0] = m.astype(BF16)


def _branch_merge(ya, yb, yc, z, w_branch, tm, tn):
    b, t, _ = ya.shape
    d = D_MODEL
    wa, wb, wc = w_branch[:512], w_branch[512:1024], w_branch[1024:]
    gate = lambda k: (lambda bi, i, j: (bi, i, (COL_GATE + k * d) // tn + j))
    return pl.pallas_call(
        _merge_kernel,
        grid=(b, t // tm, d // tn),
        in_specs=[
            pl.BlockSpec((1, tm, 512), lambda bi, i, j: (bi, i, 0)),
            pl.BlockSpec((1, tm, 512), lambda bi, i, j: (bi, i, 0)),
            pl.BlockSpec((1, tm, 1024), lambda bi, i, j: (bi, i, 0)),
            pl.BlockSpec((1, tm, tn), gate(0)),
            pl.BlockSpec((1, tm, tn), gate(1)),
            pl.BlockSpec((1, tm, tn), gate(2)),
            pl.BlockSpec((512, tn), lambda bi, i, j: (0, j)),
            pl.BlockSpec((512, tn), lambda bi, i, j: (0, j)),
            pl.BlockSpec((1024, tn), lambda bi, i, j: (0, j)),
        ],
        out_specs=pl.BlockSpec((1, tm, tn), lambda bi, i, j: (bi, i, j)),
        out_shape=jax.ShapeDtypeStruct((b, t, d), BF16),
        compiler_params=_cparams("parallel", "parallel", "arbitrary"),
        name="branch_merge",
    )(ya, yb, yc, z, z, z, wa, wb, wc)


def _resid_kernel(m_ref, w_ref, x_ref, gt_ref, o_ref):
    o_ref[0] = x_ref[0] + gt_ref[0] * jnp.dot(m_ref[0], w_ref[...], preferred_element_type=F32)


def _out_proj_residual(m, w_out, x, gt, tm, tn):
    b, t, d = x.shape
    return pl.pallas_call(
        _resid_kernel,
        grid=(b, t // tm, d // tn),
        in_specs=[
            pl.BlockSpec((1, tm, d), lambda bi, i, j: (bi, i, 0)),
            pl.BlockSpec((d, tn), lambda bi, i, j: (0, j)),
            pl.BlockSpec((1, tm, tn), lambda bi, i, j: (bi, i, j)),
            pl.BlockSpec((1, 1, tn), lambda bi, i, j: (bi, 0, j)),
        ],
        out_specs=pl.BlockSpec((1, tm, tn), lambda bi, i, j: (bi, i, j)),
        out_shape=jax.ShapeDtypeStruct((b, t, d), F32),
        compiler_params=_cparams("parallel", "parallel", "arbitrary"),
        name="out_proj_residual",
    )(m, w_out, x, gt)


def _router_kernel(*refs, starts):
    n_s = len(starts) - 1
    g_ref, wh_ref, wl_ref, br_ref, h_ref, eid_ref, wt_ref = refs[3 * n_s:]
    i = pl.program_id(0)
    for k in range(n_s):
        x_ref, sc_ref, sh_ref = refs[3 * k:3 * k + 3]

        @pl.when((i >= starts[k]) & (i < starts[k + 1]))
        def _(x_ref=x_ref, sc_ref=sc_ref, sh_ref=sh_ref):
            _route_rows(x_ref, sc_ref, sh_ref, g_ref, wh_ref, wl_ref, br_ref, h_ref, eid_ref, wt_ref)


def _route_rows(x_ref, sc_ref, sh_ref, g_ref, wh_ref, wl_ref, br_ref, h_ref, eid_ref, wt_ref):
    h = _rms(x_ref[0], g_ref[...]) * (1.0 + sc_ref[0]) + sh_ref[0]
    h_ref[...] = h
    hh = h.astype(BF16)
    hl = (h - hh.astype(F32)).astype(BF16)
    logits = (jnp.dot(hh, wh_ref[...], preferred_element_type=F32)
              + jnp.dot(hl, wh_ref[...], preferred_element_type=F32)
              + jnp.dot(hh, wl_ref[...], preferred_element_type=F32)) + br_ref[...]
    lane = lax.broadcasted_iota(jnp.int32, logits.shape, 1)
    lane_f = lane.astype(F32)

    def first_argmax(v, valid):
        vm = jnp.where(valid, v, -jnp.inf)
        mx = jnp.max(vm, axis=-1, keepdims=True)
        idx = jnp.min(jnp.where(valid & (vm == mx), lane_f, 1e9), axis=-1, keepdims=True)
        return mx, idx.astype(jnp.int32)

    is_g = lane < N_GROUPS
    gmax, grp = first_argmax(logits, is_g)
    p_grp = 1.0 / jnp.sum(jnp.where(is_g, jnp.exp(logits - gmax), 0.0), axis=-1, keepdims=True)
    e_lo = N_GROUPS + grp * EXPERTS_PER_GROUP
    in_grp = (lane >= e_lo) & (lane < e_lo + EXPERTS_PER_GROUP)
    v0, i0 = first_argmax(logits, in_grp)
    v1, i1 = first_argmax(logits, in_grp & (lane != i0))
    e1 = jnp.exp(v1 - v0)
    w0 = p_grp / (1.0 + e1)
    w1 = p_grp * e1 / (1.0 + e1)
    eid_ref[...] = jnp.where(lane == 0, i0 - N_GROUPS, jnp.where(lane == 1, i1 - N_GROUPS, 0))
    wt_ref[...] = jnp.where(lane == 0, w0, jnp.where(lane == 1, w1, 0.0))


def _router(xs, scs, shs, g, wr_hi, wr_lo, br, tm):
    d = xs[0].shape[-1]
    starts = [0]
    in_specs, args = [], []
    for x, sc, sh in zip(xs, scs, shs):
        b, t, _ = x.shape
        n_i = t // tm
        s0 = starts[-1]
        starts.append(s0 + b * n_i)

        def local(i, s0=s0, n=b * n_i):
            return jnp.clip(i - s0, 0, n - 1)

        in_specs += [pl.BlockSpec((1, tm, d), lambda i, f=local, n_i=n_i: (f(i) // n_i, f(i) % n_i, 0)),
                     pl.BlockSpec((1, 1, d), lambda i, f=local, n_i=n_i: (f(i) // n_i, 0, 0)),
                     pl.BlockSpec((1, 1, d), lambda i, f=local, n_i=n_i: (f(i) // n_i, 0, 0))]
        args += [x, sc, sh]
    n_all = starts[-1] * tm
    in_specs += [pl.BlockSpec((1, d), lambda i: (0, 0)), pl.BlockSpec((d, 128), lambda i: (0, 0)),
                 pl.BlockSpec((d, 128), lambda i: (0, 0)), pl.BlockSpec((1, 128), lambda i: (0, 0))]
    args += [g.reshape(1, d), wr_hi, wr_lo, br]
    flat = lambda width: pl.BlockSpec((tm, width), lambda i: (i, 0))
    return pl.pallas_call(
        functools.partial(_router_kernel, starts=tuple(starts)),
        grid=(starts[-1],),
        in_specs=in_specs,
        out_specs=[flat(d), flat(128), flat(128)],
        out_shape=[jax.ShapeDtypeStruct((n_all, d), F32), jax.ShapeDtypeStruct((n_all, 128), jnp.int32),
                   jax.ShapeDtypeStruct((n_all, 128), F32)],
        compiler_params=_cparams("parallel"),
        name="moe_router",
    )(*args)


def _row_copy(src_hbm, dst_vmem, sem, src_row, dst_row):
    return pltpu.make_async_copy(src_hbm.at[pl.ds(src_row, 1)], dst_vmem.at[pl.ds(dst_row, 1)], sem)


def _expert_kernel(blk_e_ref, tok_ref, tok_next_ref, x_hbm, w1_ref, w3_ref, w2_ref, o_ref, xbuf0, xbuf1, sem):
    del blk_e_ref
    tm = xbuf0.shape[0]
    i = pl.program_id(0)
    n = pl.num_programs(0)
    bufs = (xbuf0, xbuf1)

    def wait_rows(b):
        def wait(r, c):
            _row_copy(x_hbm, bufs[b], sem.at[b], 0, r).wait()
            return c
        lax.fori_loop(0, tm, wait, 0, unroll=8)

    @pl.when(i == 0)
    def _():
        def issue(r, c):
            _row_copy(x_hbm, xbuf0, sem.at[0], tok_ref[0, 0, r], r).start()
            return c
        lax.fori_loop(0, tm, issue, 0, unroll=8)

    def block(cur):
        wait_rows(cur)
        for r in range(tm):
            _row_copy(x_hbm, bufs[1 - cur], sem.at[1 - cur], tok_next_ref[0, 0, r], r).start()
        xb = bufs[cur][...].astype(BF16)
        h1 = jnp.dot(xb, w1_ref[0], preferred_element_type=F32)
        h3 = jnp.dot(xb, w3_ref[0], preferred_element_type=F32)
        hid = (h1 * jax.nn.sigmoid(h1) * h3).astype(BF16)
        o_ref[...] = jnp.dot(hid, w2_ref[0], preferred_element_type=F32)

    for parity in range(2):
        @pl.when(i % 2 == parity)
        def _(parity=parity):
            block(parity)

        @pl.when((i == n - 1) & (i % 2 == parity))
        def _(parity=parity):
            wait_rows(1 - parity)


def _expert_blocks(blk_e, buf_tok, x_rows, w1, w3, w2):
    n_blk = blk_e.shape[0]
    tm = MOE_TM
    d = x_rows.shape[1]
    grid_spec = pltpu.PrefetchScalarGridSpec(
        num_scalar_prefetch=1,
        grid=(n_blk,),
        in_specs=[
            pl.BlockSpec((1, 1, tm), lambda i, e: (i, 0, 0), memory_space=pltpu.SMEM),
            pl.BlockSpec((1, 1, tm), lambda i, e: (jnp.minimum(i + 1, n_blk - 1), 0, 0), memory_space=pltpu.SMEM),
            pl.BlockSpec(memory_space=pl.ANY),
            pl.BlockSpec((1, d, D_EXPERT), lambda i, e: (e[i], 0, 0)),
            pl.BlockSpec((1, d, D_EXPERT), lambda i, e: (e[i], 0, 0)),
            pl.BlockSpec((1, D_EXPERT, d), lambda i, e: (e[i], 0, 0)),
        ],
        out_specs=pl.BlockSpec((tm, d), lambda i, e: (i, 0)),
        scratch_shapes=[pltpu.VMEM((tm, d), F32), pltpu.VMEM((tm, d), F32), pltpu.SemaphoreType.DMA((2,))],
    )
    tok = buf_tok.reshape(n_blk, 1, tm)
    return pl.pallas_call(
        _expert_kernel,
        grid_spec=grid_spec,
        out_shape=jax.ShapeDtypeStruct((n_blk * tm, d), F32),
        compiler_params=_cparams("arbitrary"),
        name="moe_experts",
    )(blk_e, tok, tok, x_rows, w1, w3, w2)


def _combine_kernel(slot_ref, slot_next_ref, y_hbm, x_ref, gt_ref, wt_ref, o_ref, a0, b0, a1, b1, sem):
    tm = a0.shape[0]
    i = pl.program_id(0)
    n = pl.num_programs(0)
    bufs = ((a0, b0), (a1, b1))

    def wait_rows(p):
        def wait(r, c):
            _row_copy(y_hbm, bufs[p][0], sem.at[p], 0, r).wait()
            _row_copy(y_hbm, bufs[p][1], sem.at[p], 0, r).wait()
            return c
        lax.fori_loop(0, tm, wait, 0, unroll=8)

    @pl.when(i == 0)
    def _():
        def issue(r, c):
            _row_copy(y_hbm, a0, sem.at[0], slot_ref[0, 0, 2 * r], r).start()
            _row_copy(y_hbm, b0, sem.at[0], slot_ref[0, 0, 2 * r + 1], r).start()
            return c
        lax.fori_loop(0, tm, issue, 0, unroll=8)

    def block(cur):
        wait_rows(cur)
        nxt_a, nxt_b = bufs[1 - cur]
        for r in range(tm):
            _row_copy(y_hbm, nxt_a, sem.at[1 - cur], slot_next_ref[0, 0, 2 * r], r).start()
            _row_copy(y_hbm, nxt_b, sem.at[1 - cur], slot_next_ref[0, 0, 2 * r + 1], r).start()
        wt = wt_ref[...]
        f = bufs[cur][0][...] * wt[:, 0:1] + bufs[cur][1][...] * wt[:, 1:2]
        o_ref[...] = x_ref[...] + gt_ref[0] * f

    for parity in range(2):
        @pl.when(i % 2 == parity)
        def _(parity=parity):
            block(parity)

        @pl.when((i == n - 1) & (i % 2 == parity))
        def _(parity=parity):
            wait_rows(1 - parity)


def _combine(slots, y, x, gt, wt, tok0, tm):
    b, t, d = x.shape
    n_i = t // tm
    n = b * n_i
    x2 = x.reshape(b * t, d)
    blk0 = tok0 // tm
    slots3 = slots.reshape(-1, 1, 2 * tm)
    out = pl.pallas_call(
        _combine_kernel,
        grid=(n,),
        in_specs=[
            pl.BlockSpec((1, 1, 2 * tm), lambda i: (blk0 + i, 0, 0), memory_space=pltpu.SMEM),
            pl.BlockSpec((1, 1, 2 * tm), lambda i: (blk0 + jnp.minimum(i + 1, n - 1), 0, 0),
                         memory_space=pltpu.SMEM),
            pl.BlockSpec(memory_space=pl.ANY),
            pl.BlockSpec((tm, d), lambda i: (i, 0)),
            pl.BlockSpec((1, 1, d), lambda i: (i // n_i, 0, 0)),
            pl.BlockSpec((tm, 128), lambda i: (blk0 + i, 0)),
        ],
        out_specs=pl.BlockSpec((tm, d), lambda i: (i, 0)),
        out_shape=jax.ShapeDtypeStruct((b * t, d), F32),
        scratch_shapes=[pltpu.VMEM((tm, d), F32)] * 4 + [pltpu.SemaphoreType.DMA((2,))],
        compiler_params=_cparams("arbitrary"),
        name="moe_combine",
    )(slots3, slots3, y, x2, gt, wt)
    return out.reshape(b, t, d)


def _routing_tables(eid):
    m = eid.shape[0]
    tm = MOE_TM
    i32 = jnp.int32
    iota = jnp.arange(m, dtype=i32)
    se, order = lax.sort_key_val(eid, iota)
    onehot = (se[:, None] == jnp.arange(N_EXPERTS, dtype=i32)[None, :]).astype(i32)
    counts = jnp.sum(onehot, axis=0)
    starts = jnp.cumsum(counts) - counts
    pcounts = (counts + tm - 1) // tm * tm
    pends = jnp.cumsum(pcounts)
    pstarts = pends - pcounts
    dest = iota + jnp.sum(onehot * (pstarts - starts)[None, :], axis=1)
    _, slot = lax.sort_key_val(order, dest)
    n_blk = (m + N_EXPERTS * (tm - 1) + tm - 1) // tm
    blk_start = jnp.arange(n_blk, dtype=i32) * tm
    blk_e = jnp.minimum(jnp.sum((pends[None, :] <= blk_start[:, None]).astype(i32), axis=1), N_EXPERTS - 1)
    off = (blk_start - pstarts[blk_e])[:, None] + jnp.arange(tm, dtype=i32)[None, :]
    valid = off < counts[blk_e][:, None]
    src = jnp.clip(starts[blk_e][:, None] + off, 0, m - 1)
    buf_tok = jnp.where(valid, order[src] // TOP_K, 0).reshape(-1)
    return blk_e.astype(i32), buf_tok, slot


def _moe(xs, scs, shs, gts, norm_g, wr_hi, wr_lo, br, w1, w3, w2):
    h_all, eid_all, wt_all = _router(xs, scs, shs, norm_g, wr_hi, wr_lo, br, 512)
    blk_e, buf_tok, slot = _routing_tables(eid_all[:, :TOP_K].reshape(-1))
    y = _expert_blocks(blk_e, buf_tok, h_all, w1, w3, w2)
    outs = []
    tok0 = 0
    for x, gt in zip(xs, gts):
        outs.append(_combine(slot, y, x, gt, wt_all, tok0, tm=256))
        tok0 += x.shape[0] * x.shape[1]
    return outs


def kernel(x, c, ctx, c_ctx, w_mod, b_mod, norm1_g, norm2_g, w_in, na_q_g, na_k_g, na_rpb, df_q_g, df_k_g, df_lam, df_sub_g, rg_conv_w, rg_conv_b, rg_w_a, rg_b_a, rg_w_x, rg_b_x, rg_lam, w_branch, w_out, w_router_g, b_router_g, w_router_e, b_router_e, w1, w3, w2):
    B, S, D = x.shape
    C = ctx.shape[1]
    n_rows = S // GRID_W
    rope_cos, rope_sin = _rope_tables(S)
    na_bias = _na_bias(na_rpb, n_rows)

    pad = (-(B + 1)) % 8
    rows = jnp.concatenate([c, c_ctx[None, :], jnp.zeros((pad, D), F32)], axis=0)
    mod_all = _modulation(rows, w_mod, b_mod)

    xc = ctx.reshape(1, B * C, D)
    for l in range(DEPTH):
        need_ctx = l < DEPTH - 1
        lam_init = 0.8 - 0.6 * float(np.exp(-0.3 * l))
        mod = mod_all[l]
        sh1, sc1, gt1, sh2, sc2, gt2 = [mod[:B, k * D:(k + 1) * D][:, None, :] for k in range(6)]
        csh1, csc1, cgt1, csh2, csc2, cgt2 = [mod[B:B + 1, k * D:(k + 1) * D][:, None, :] for k in range(6)]

        w_in_l = w_in[l].astype(BF16)
        zx = _norm_mod_matmul(x, norm1_g[l], sc1, sh1, w_in_l, tm=1024, tn=1024)
        n_cc = IN_COLS if need_ctx else KV_COLS
        zc = _norm_mod_matmul(xc, norm1_g[l], csc1, csh1, w_in_l[:, :n_cc], tm=1024, tn=1024).reshape(B, C, n_cc)

        y_a = _na_attention(zx, zc, na_bias[l], na_q_g[l], na_k_g[l])

        lp = df_lam[l]
        lam = jnp.exp(jnp.sum(lp[0] * lp[1])) - jnp.exp(jnp.sum(lp[2] * lp[3])) + lam_init
        y_b = _df_attention(lam, zx, zx, zc, rope_cos, rope_sin, df_q_g[l], df_k_g[l], df_sub_g[l], 1.0 - lam_init)

        y_c, y_cc = _rg_lru(zx, zc, rg_conv_w[l], rg_conv_b[l], rg_w_a[l].astype(BF16), rg_b_a[l],
                            rg_w_x[l].astype(BF16), rg_b_x[l], rg_lam[l], need_ctx)

        wb_l = w_branch[l].astype(BF16)
        wo_l = w_out[l].astype(BF16)
        m_x = _branch_merge(y_a, y_b, y_c, zx, wb_l, tm=1024, tn=1024)
        x = _out_proj_residual(m_x, wo_l, x, gt1, tm=1024, tn=1024)
        if need_ctx:
            y_ac = _na_ctx_attention(zc, na_q_g[l], na_k_g[l])
            y_bc = _df_attention(lam, zc, None, zc, None, None, df_q_g[l], df_k_g[l], df_sub_g[l], 1.0 - lam_init)
            m_c = _branch_merge(y_ac, y_bc, y_cc, zc, wb_l, tm=C, tn=1024)
            xc = _out_proj_residual(m_c.reshape(1, B * C, D), wo_l, xc, cgt1, tm=1024, tn=1024)

        wr = jnp.concatenate([w_router_g[l], w_router_e[l],
                              jnp.zeros((D, 128 - N_GROUPS - N_EXPERTS), F32)], axis=1)
        wr_hi = wr.astype(BF16)
        wr_lo = (wr - wr_hi.astype(F32)).astype(BF16)
        br = jnp.concatenate([b_router_g[l], b_router_e[l],
                              jnp.zeros((128 - N_GROUPS - N_EXPERTS,), F32)]).reshape(1, 128)
        w1_l, w3_l, w2_l = w1[l].astype(BF16), w3[l].astype(BF16), w2[l].astype(BF16)
        if need_ctx:
            xc, x = _moe([xc, x], [csc2, sc2], [csh2, sh2], [cgt2, gt2], norm2_g[l],
                         wr_hi, wr_lo, br, w1_l, w3_l, w2_l)
        else:
            (x,) = _moe([x], [sc2], [sh2], [gt2], norm2_g[l], wr_hi, wr_lo, br, w1_l, w3_l, w2_l)
    return x
```

```python
import functools

import jax
import jax.numpy as jnp
import numpy as np
from jax import lax
from jax.experimental import pallas as pl
from jax.experimental.pallas import tpu as pltpu

F32 = jnp.float32
BF16 = jnp.bfloat16

D_MODEL = 2048
DEPTH = 4
GRID_W = 64
HEAD_DIM = 128
N_HEADS = 4
NA_WIN_R = 8
NA_WIN_C = 16
NA_QCB = 16
NA_KCB = 32
DF_DIM = 64
RG_WIDTH = 1024
RG_BLOCKS = 8
RG_BW = 128
RG_C = 8.0
N_GROUPS = 4
EXPERTS_PER_GROUP = 8
N_EXPERTS = 32
TOP_K = 2
D_EXPERT = 512
ROPE_BASE = 10000.0
EPS = 1e-6
NEG = -1e30
LOG2_E = 1.4426950408889634

COL_NA_K, COL_NA_V, COL_DF_K, COL_DF_V, COL_RG_X = 0, 512, 1024, 1536, 2048
COL_NA_Q, COL_DF_Q, COL_RG_G, COL_GATE = 3072, 3584, 4096, 5120
KV_COLS = 3072
MIX_COLS = 5120
IN_COLS = MIX_COLS + 3 * D_MODEL

VMEM_LIMIT_BYTES = 56 * 1024 * 1024

NA_ROWS_PER_STEP = 4
NA_KEY_ROWS = NA_ROWS_PER_STEP + NA_WIN_R - 1
DF_QBLK = 1024
DF_KCHUNK = 768
MOE_TM = 256
SCAN_CHUNKS = 8
SCAN_PAD_ROWS = 8


def _cparams(*sem):
    return pltpu.CompilerParams(dimension_semantics=sem, vmem_limit_bytes=VMEM_LIMIT_BYTES)


def _rms(x, g):
    x = x.astype(F32)
    return x * lax.rsqrt(jnp.mean(x * x, axis=-1, keepdims=True) + EPS) * g


def _mod_kernel(s_ref, w_ref, b_ref, o_ref):
    s = s_ref[...]
    a = (s * jax.nn.sigmoid(s)).astype(BF16)
    o_ref[0] = jnp.dot(a, w_ref[0].astype(BF16), preferred_element_type=F32) + b_ref[0]


def _modulation(rows, w_mod, b_mod, tn=1024):
    n_l, d, n = w_mod.shape
    r = rows.shape[0]
    return pl.pallas_call(
        _mod_kernel,
        grid=(n_l, n // tn),
        in_specs=[
            pl.BlockSpec((r, d), lambda l, j: (0, 0)),
            pl.BlockSpec((1, d, tn), lambda l, j: (l, 0, j)),
            pl.BlockSpec((1, 1, tn), lambda l, j: (l, 0, j)),
        ],
        out_specs=pl.BlockSpec((1, r, tn), lambda l, j: (l, 0, j)),
        out_shape=jax.ShapeDtypeStruct((n_l, r, n), F32),
        compiler_params=_cparams("parallel", "parallel"),
        name="modulation",
    )(rows, w_mod, b_mod.reshape(n_l, 1, n))


def _nmm_kernel(x_ref, g_ref, sc_ref, sh_ref, w_ref, o_ref, h_ref):
    @pl.when(pl.program_id(2) == 0)
    def _():
        h = _rms(x_ref[0], g_ref[...]) * (1.0 + sc_ref[0]) + sh_ref[0]
        h_ref[...] = h.astype(BF16)

    o_ref[0] = jnp.dot(h_ref[...], w_ref[...], preferred_element_type=F32).astype(o_ref.dtype)


def _norm_mod_matmul(x, g, sc, sh, w, tm, tn):
    b, t, d = x.shape
    n = w.shape[1]
    return pl.pallas_call(
        _nmm_kernel,
        grid=(b, t // tm, n // tn),
        in_specs=[
            pl.BlockSpec((1, tm, d), lambda bi, i, j: (bi, i, 0)),
            pl.BlockSpec((1, d), lambda bi, i, j: (0, 0)),
            pl.BlockSpec((1, 1, d), lambda bi, i, j: (bi, 0, 0)),
            pl.BlockSpec((1, 1, d), lambda bi, i, j: (bi, 0, 0)),
            pl.BlockSpec((d, tn), lambda bi, i, j: (0, j)),
        ],
        out_specs=pl.BlockSpec((1, tm, tn), lambda bi, i, j: (bi, i, j)),
        out_shape=jax.ShapeDtypeStruct((b, t, n), BF16),
        scratch_shapes=[pltpu.VMEM((tm, d), BF16)],
        compiler_params=_cparams("parallel", "parallel", "arbitrary"),
        name="norm_mod_proj",
    )(x, g.reshape(1, d), sc, sh, w)


def _na_window_start(r0):
    lower = jnp.clip(r0 - NA_WIN_R // 2, 0, (2048 // GRID_W) - NA_WIN_R)
    return jnp.minimum(lower, (2048 // GRID_W) - NA_KEY_ROWS)


def _na_bias(rpb, n_rows):
    rb, kw, w = NA_ROWS_PER_STEP, NA_KEY_ROWS, GRID_W
    n_dr = 2 * NA_WIN_R - 1
    qc = np.arange(w)
    kc = np.arange(w)
    qwin = np.clip(qc - NA_WIN_C // 2, 0, w - NA_WIN_C)
    kcol_start = np.clip((qc // NA_QCB) * NA_QCB - (NA_KCB - NA_QCB) // 2, 0, w - NA_KCB)
    col_ok = (kc[None, :] >= qwin[:, None]) & (kc[None, :] < qwin[:, None] + NA_WIN_C)
    col_ok &= (kc[None, :] >= kcol_start[:, None]) & (kc[None, :] < kcol_start[:, None] + NA_KCB)
    edge = w - NA_WIN_C
    ext = jnp.concatenate([jnp.repeat(rpb[..., :1], edge, axis=-1), rpb, jnp.repeat(rpb[..., -1:], edge, axis=-1)],
                          axis=-1)
    toep = jnp.stack([ext[..., w - 1 - q:2 * w - 1 - q] for q in range(w)], axis=-2)
    toep = jnp.where(col_ok, toep, NEG)
    toep = jnp.concatenate([toep, jnp.full(toep.shape[:-3] + (1, w, w), NEG, F32)], axis=-3)
    rows = np.arange(n_rows)
    row_start = np.clip(rows - NA_WIN_R // 2, 0, n_rows - NA_WIN_R)
    n_steps = n_rows // rb
    blk = np.full((n_steps, rb, kw), n_dr, np.int32)
    for s in range(n_steps):
        ws = min(int(np.clip(s * rb - NA_WIN_R // 2, 0, n_rows - NA_WIN_R)), n_rows - kw)
        for j in range(rb):
            r = s * rb + j
            for i in range(kw):
                if row_start[r] <= ws + i < row_start[r] + NA_WIN_R:
                    blk[s, j, i] = ws + i - r + (NA_WIN_R - 1)
    lead = toep.shape[:-4]
    nl = len(lead)
    out = jnp.take(toep, jnp.asarray(blk.reshape(-1)), axis=-3)
    out = out.reshape(lead + (N_HEADS, n_steps, rb, kw, w, w))
    perm = tuple(range(nl)) + (nl + 1, nl, nl + 2, nl + 4, nl + 3, nl + 5)
    return out.transpose(perm).reshape(lead + (n_steps, N_HEADS, rb * w, kw * w))


def _na_kernel(q_ref, k_ref, v_ref, kc_ref, vc_ref, bias_ref, gq_ref, gk_ref, o_ref, kn, vn, knc, vnc):
    step = pl.program_id(1)

    @pl.when(step == 0)
    def _():
        for h in range(N_HEADS):
            sl = slice(h * HEAD_DIM, (h + 1) * HEAD_DIM)
            kn[:, sl] = _rms(k_ref[0, :, sl], gk_ref[...]).astype(BF16)
            knc[:, sl] = _rms(kc_ref[0, :, sl], gk_ref[...]).astype(BF16)
        vn[...] = v_ref[0].astype(BF16)
        vnc[...] = vc_ref[0].astype(BF16)

    ws = _na_window_start(step * NA_ROWS_PER_STEP)
    start = pl.multiple_of(ws * GRID_W, GRID_W)
    nk = NA_KEY_ROWS * GRID_W
    scale = HEAD_DIM ** -0.5
    nt = (((1,), (1,)), ((), ()))
    for h in range(N_HEADS):
        sl = slice(h * HEAD_DIM, (h + 1) * HEAD_DIM)
        qn = (_rms(q_ref[0, :, sl], gq_ref[...]) * scale).astype(BF16)
        s_loc = lax.dot_general(qn, kn[pl.ds(start, nk), sl], nt, preferred_element_type=F32) + bias_ref[0, h]
        s_ctx = lax.dot_general(qn, knc[:, sl], nt, preferred_element_type=F32)
        m = jnp.maximum(jnp.max(s_loc, axis=-1, keepdims=True), jnp.max(s_ctx, axis=-1, keepdims=True))
        p_loc = jnp.exp(s_loc - m)
        p_ctx = jnp.exp(s_ctx - m)
        l = jnp.sum(p_loc, axis=-1, keepdims=True) + jnp.sum(p_ctx, axis=-1, keepdims=True)
        o = jnp.dot(p_loc.astype(BF16), vn[pl.ds(start, nk), sl], preferred_element_type=F32)
        o = o + jnp.dot(p_ctx.astype(BF16), vnc[:, sl], preferred_element_type=F32)
        o_ref[0, :, sl] = (o * (1.0 / l)).astype(BF16)


def _na_attention(zx, zc, bias, gq, gk):
    b, s, _ = zx.shape
    c = zc.shape[1]
    w = N_HEADS * HEAD_DIM
    tq = NA_ROWS_PER_STEP * GRID_W
    nk = NA_KEY_ROWS * GRID_W
    return pl.pallas_call(
        _na_kernel,
        grid=(b, s // tq),
        in_specs=[
            pl.BlockSpec((1, tq, w), lambda bi, i: (bi, i, COL_NA_Q // w)),
            pl.BlockSpec((1, s, w), lambda bi, i: (bi, 0, COL_NA_K // w)),
            pl.BlockSpec((1, s, w), lambda bi, i: (bi, 0, COL_NA_V // w)),
            pl.BlockSpec((1, c, w), lambda bi, i: (bi, 0, COL_NA_K // w)),
            pl.BlockSpec((1, c, w), lambda bi, i: (bi, 0, COL_NA_V // w)),
            pl.BlockSpec((1, N_HEADS, tq, nk), lambda bi, i: (i, 0, 0, 0)),
            pl.BlockSpec((1, HEAD_DIM), lambda bi, i: (0, 0)),
            pl.BlockSpec((1, HEAD_DIM), lambda bi, i: (0, 0)),
        ],
        out_specs=pl.BlockSpec((1, tq, w), lambda bi, i: (bi, i, 0)),
        out_shape=jax.ShapeDtypeStruct((b, s, w), BF16),
        scratch_shapes=[pltpu.VMEM((s, w), BF16), pltpu.VMEM((s, w), BF16),
                        pltpu.VMEM((c, w), BF16), pltpu.VMEM((c, w), BF16)],
        compiler_params=_cparams("parallel", "arbitrary"),
        name="na_attention",
    )(zx, zx, zx, zc, zc, bias, gq.reshape(1, HEAD_DIM), gk.reshape(1, HEAD_DIM))


def _na_ctx_kernel(q_ref, k_ref, v_ref, gq_ref, gk_ref, o_ref):
    scale = HEAD_DIM ** -0.5
    nt = (((1,), (1,)), ((), ()))
    for h in range(N_HEADS):
        sl = slice(h * HEAD_DIM, (h + 1) * HEAD_DIM)
        qn = (_rms(q_ref[0, :, sl], gq_ref[...]) * scale).astype(BF16)
        kn = _rms(k_ref[0, :, sl], gk_ref[...]).astype(BF16)
        s = lax.dot_general(qn, kn, nt, preferred_element_type=F32)
        p = jnp.exp(s - jnp.max(s, axis=-1, keepdims=True))
        l = jnp.sum(p, axis=-1, keepdims=True)
        o = jnp.dot(p.astype(BF16), v_ref[0, :, sl].astype(BF16), preferred_element_type=F32)
        o_ref[0, :, sl] = (o * (1.0 / l)).astype(BF16)


def _na_ctx_attention(zc, gq, gk):
    b, c, _ = zc.shape
    w = N_HEADS * HEAD_DIM
    return pl.pallas_call(
        _na_ctx_kernel,
        grid=(b,),
        in_specs=[
            pl.BlockSpec((1, c, w), lambda bi: (bi, 0, COL_NA_Q // w)),
            pl.BlockSpec((1, c, w), lambda bi: (bi, 0, COL_NA_K // w)),
            pl.BlockSpec((1, c, w), lambda bi: (bi, 0, COL_NA_V // w)),
            pl.BlockSpec((1, HEAD_DIM), lambda bi: (0, 0)),
            pl.BlockSpec((1, HEAD_DIM), lambda bi: (0, 0)),
        ],
        out_specs=pl.BlockSpec((1, c, w), lambda bi: (bi, 0, 0)),
        out_shape=jax.ShapeDtypeStruct((b, c, w), BF16),
        compiler_params=_cparams("parallel"),
        name="na_ctx_attention",
    )(zc, zc, zc, gq.reshape(1, HEAD_DIM), gk.reshape(1, HEAD_DIM))


def _rope_tables(n_tok):
    t = np.arange(n_tok)
    pos = np.stack([t // GRID_W, t % GRID_W], axis=0).astype(np.float32)
    n_freq = DF_DIM // 4
    inv = (np.float32(ROPE_BASE) ** (-np.arange(n_freq, dtype=np.float32) / n_freq)).astype(np.float32)
    ang = jnp.asarray(pos[:, :, None] * inv)
    cos, sin = jnp.cos(ang), jnp.sin(ang)
    cos64 = jnp.concatenate([cos[0], cos[0], cos[1], cos[1]], axis=-1)
    sin64 = jnp.concatenate([-sin[0], sin[0], -sin[1], sin[1]], axis=-1)
    return jnp.concatenate([cos64, cos64], axis=-1), jnp.concatenate([sin64, sin64], axis=-1)


def _df_kernel(lam_ref, q_ref, *refs, n_x, n_c, rope, out_scale):
    if n_x:
        kx_ref, vx_ref, kc_ref, vc_ref, cq_ref, sq_ref, ck_ref, sk_ref, gq_ref, gk_ref, gs_ref, o_ref, kn, vn = refs
    else:
        kc_ref, vc_ref, gq_ref, gk_ref, gs_ref, o_ref, kn, vn = refs
    lane = lax.broadcasted_iota(jnp.int32, (1, 2 * DF_DIM), 1)
    lo = lane < DF_DIM
    first = (lane % (DF_DIM // 2)) < (DF_DIM // 4)

    def norm64(x, g):
        x = x.astype(F32)
        x2 = x * x
        s0 = jnp.sum(jnp.where(lo, x2, 0.0), axis=-1, keepdims=True)
        s1 = jnp.sum(jnp.where(lo, 0.0, x2), axis=-1, keepdims=True)
        ms = jnp.where(lo, s0, s1) * (1.0 / DF_DIM)
        return x * lax.rsqrt(ms + EPS) * g

    def rot(x, cos, sin):
        partner = jnp.where(first, pltpu.roll(x, 2 * DF_DIM - DF_DIM // 4, 1), pltpu.roll(x, DF_DIM // 4, 1))
        return x * cos + partner * sin

    @pl.when(pl.program_id(2) == 0)
    def _():
        if n_x:
            kx = rot(norm64(kx_ref[0], gk_ref[...]), ck_ref[...], sk_ref[...])
            kn[0:n_x, :] = kx.astype(BF16)
            vn[0:n_x, :] = vx_ref[0].astype(BF16)
        kn[n_x:n_x + n_c, :] = norm64(kc_ref[0], gk_ref[...]).astype(BF16)
        vn[n_x:n_x + n_c, :] = vc_ref[0].astype(BF16)

    q = norm64(q_ref[0], gq_ref[...])
    if rope:
        q = rot(q, cq_ref[...], sq_ref[...])
    q = q * (DF_DIM ** -0.5 * LOG2_E)
    nt = (((1,), (1,)), ((), ()))

    n_k = n_x + n_c
    qs = (jnp.where(lo, q, 0.0).astype(BF16), jnp.where(lo, 0.0, q).astype(BF16))
    tq = q.shape[0]
    ms = [jnp.full((tq, 1), -jnp.inf, F32)] * 2
    ls = [jnp.zeros((tq, 1), F32)] * 2
    accs = [jnp.zeros((tq, 2 * DF_DIM), F32)] * 2
    for c0 in range(0, n_k, DF_KCHUNK):
        c1 = min(c0 + DF_KCHUNK, n_k)
        for mi in range(2):
            s = lax.dot_general(qs[mi], kn[c0:c1, :], nt, preferred_element_type=F32)
            m_new = jnp.maximum(ms[mi], jnp.max(s, axis=-1, keepdims=True))
            alpha = jnp.exp2(ms[mi] - m_new)
            p = jnp.exp2(s - m_new)
            ls[mi] = alpha * ls[mi] + jnp.sum(p, axis=-1, keepdims=True)
            accs[mi] = alpha * accs[mi] + jnp.dot(p.astype(BF16), vn[c0:c1, :], preferred_element_type=F32)
            ms[mi] = m_new
    o = accs[0] * (1.0 / ls[0]) - accs[1] * (lam_ref[0] / ls[1])
    o_ref[0] = (_rms(o, gs_ref[...]) * out_scale).astype(BF16)


def _df_attention(lam, zq, zx, zc, cos, sin, gq, gk, gs, out_scale):
    b, t, _ = zq.shape
    c = zc.shape[1]
    n_x = 0 if zx is None else zx.shape[1]
    w = 2 * DF_DIM
    tq = min(DF_QBLK, t)
    gq2 = jnp.concatenate([gq, gq]).reshape(1, w)
    gk2 = jnp.concatenate([gk, gk]).reshape(1, w)
    col = lambda base: (lambda bi, h, i: (bi, 0, base // w + h))
    vec = pl.BlockSpec((1, w), lambda bi, h, i: (0, 0))
    in_specs = [pl.BlockSpec(memory_space=pltpu.SMEM),
                pl.BlockSpec((1, tq, w), lambda bi, h, i: (bi, i, COL_DF_Q // w + h))]
    args = [lam.reshape(1), zq]
    if n_x:
        in_specs += [pl.BlockSpec((1, n_x, w), col(COL_DF_K)), pl.BlockSpec((1, n_x, w), col(COL_DF_V))]
        args += [zx, zx]
    in_specs += [pl.BlockSpec((1, c, w), col(COL_DF_K)), pl.BlockSpec((1, c, w), col(COL_DF_V))]
    args += [zc, zc]
    if n_x:
        in_specs += [pl.BlockSpec((tq, w), lambda bi, h, i: (i, 0)), pl.BlockSpec((tq, w), lambda bi, h, i: (i, 0)),
                     pl.BlockSpec((n_x, w), lambda bi, h, i: (0, 0)), pl.BlockSpec((n_x, w), lambda bi, h, i: (0, 0))]
        args += [cos, sin, cos, sin]
    in_specs += [vec, vec, vec]
    args += [gq2, gk2, gs.reshape(1, w)]
    return pl.pallas_call(
        functools.partial(_df_kernel, n_x=n_x, n_c=c, rope=bool(n_x), out_scale=out_scale),
        grid=(b, N_HEADS, t // tq),
        in_specs=in_specs,
        out_specs=pl.BlockSpec((1, tq, w), lambda bi, h, i: (bi, i, h)),
        out_shape=jax.ShapeDtypeStruct((b, t, N_HEADS * w), BF16),
        scratch_shapes=[pltpu.VMEM((n_x + c, w), BF16), pltpu.VMEM((n_x + c, w), BF16)],
        compiler_params=_cparams("parallel", "parallel", "arbitrary"),
        name="diff_attention" if n_x else "diff_ctx_attention",
    )(*args)


def _rg_kernel(*refs, n_x, n_c, ctx_out):
    if ctx_out:
        (ux_ref, uc_ref, gx_ref, gc_ref, cw_ref, cb_ref, wa_ref, ba_ref, wx_ref, bx_ref, lam_ref,
         ox_ref, oc_ref, a_s, b_s, p_s) = refs
    else:
        (ux_ref, uc_ref, gx_ref, cw_ref, cb_ref, wa_ref, ba_ref, wx_ref, bx_ref, lam_ref,
         ox_ref, a_s, b_s, p_s) = refs
    n_t = n_x + n_c
    clen = n_t // SCAN_CHUNKS
    cstride = clen + SCAN_PAD_ROWS

    def conv(z):
        z = z.astype(F32)
        n = z.shape[0]
        t = lax.broadcasted_iota(jnp.int32, (n, 1), 0)
        zm2 = jnp.where(t >= 2, pltpu.roll(z, 2, 0), 0.0)
        zm1 = jnp.where(t >= 1, pltpu.roll(z, 1, 0), 0.0)
        zp1 = jnp.where(t < n - 1, pltpu.roll(z, n - 1, 0), 0.0)
        return (zm2 * cw_ref[0:1, :] + zm1 * cw_ref[1:2, :] + z * cw_ref[2:3, :] + zp1 * cw_ref[3:4, :]
                + cb_ref[...])

    def pieces(t0, n):
        out, t = [], t0
        while t < t0 + n:
            ch = t // clen
            stop = min((ch + 1) * clen, t0 + n)
            out.append((t - t0, ch * cstride + (t - ch * clen), stop - t))
            t = stop
        return out

    def put(ref, d, t0, val):
        for off, row, ln in pieces(t0, val.shape[0]):
            ref[d, row:row + ln, :] = val[off:off + ln]

    def get(ref, d, t0, n):
        return jnp.concatenate([ref[d, row:row + ln, :] for _, row, ln in pieces(t0, n)], axis=0)

    def coeffs(u, d, t0):
        ub = u.astype(BF16)
        r = jax.nn.sigmoid(jnp.dot(ub, wa_ref[d, 0], preferred_element_type=F32) + ba_ref[d:d + 1, :])
        gi = jax.nn.sigmoid(jnp.dot(ub, wx_ref[d, 0], preferred_element_type=F32) + bx_ref[d:d + 1, :])
        log_a = (-RG_C) * r * jax.nn.softplus(-lam_ref[d:d + 1, :])
        a = jnp.exp(log_a)
        put(a_s, d, t0, a)
        put(b_s, d, t0, jnp.sqrt(-jnp.tanh(log_a) * (a * a + 1.0)) * (gi * u))

    u_c = conv(uc_ref[0])
    coeffs(u_c, 0, 0)
    coeffs(u_c, 1, n_x)
    u_x = conv(ux_ref[0])
    coeffs(u_x, 0, n_c)
    coeffs(u_x, 1, 0)

    def step(tau, carry):
        h_f, p_f, h_r, p_r = carry
        i_f = pl.ds(tau, SCAN_CHUNKS, stride=cstride)
        i_r = pl.ds(clen - 1 - tau, SCAN_CHUNKS, stride=cstride)
        a_f = a_s[0, i_f, :]
        a_r = a_s[1, i_r, :]
        h_f = a_f * h_f + b_s[0, i_f, :]
        h_r = a_r * h_r + b_s[1, i_r, :]
        p_f = a_f * p_f
        p_r = a_r * p_r
        b_s[0, i_f, :] = h_f
        b_s[1, i_r, :] = h_r
        p_s[0, i_f, :] = p_f
        p_s[1, i_r, :] = p_r
        return h_f, p_f, h_r, p_r

    zeros = jnp.zeros((SCAN_CHUNKS, RG_BW), F32)
    ones = jnp.ones((SCAN_CHUNKS, RG_BW), F32)
    h_f, p_f, h_r, p_r = lax.fori_loop(0, clen, step, (zeros, ones, zeros, ones), unroll=2)

    carry = jnp.zeros((1, RG_BW), F32)
    for ch in range(1, SCAN_CHUNKS):
        carry = p_f[ch - 1:ch, :] * carry + h_f[ch - 1:ch, :]
        rows = slice(ch * cstride, ch * cstride + clen)
        b_s[0, rows, :] = b_s[0, rows, :] + p_s[0, rows, :] * carry
    carry = jnp.zeros((1, RG_BW), F32)
    for ch in range(SCAN_CHUNKS - 2, -1, -1):
        carry = p_r[ch + 1:ch + 2, :] * carry + h_r[ch + 1:ch + 2, :]
        rows = slice(ch * cstride, ch * cstride + clen)
        b_s[1, rows, :] = b_s[1, rows, :] + p_s[1, rows, :] * carry

    gx = jax.nn.gelu(gx_ref[0].astype(F32), approximate=True)
    ox_ref[0] = ((get(b_s, 0, n_c, n_x) + get(b_s, 1, 0, n_x)) * gx).astype(BF16)
    if ctx_out:
        gc = jax.nn.gelu(gc_ref[0].astype(F32), approximate=True)
        oc_ref[0] = ((get(b_s, 0, 0, n_c) + get(b_s, 1, n_x, n_c)) * gc).astype(BF16)


def _rg_lru(zx, zc, conv_w, conv_b, w_a, b_a, w_x, b_x, lam, ctx_out):
    b, n_x, _ = zx.shape
    n_c = zc.shape[1]
    bw = RG_BW
    col = lambda base: (lambda bi, n: (bi, 0, base // bw + n))
    vec2 = pl.BlockSpec((2, bw), lambda bi, n: (0, n))
    wspec = pl.BlockSpec((2, 1, bw, bw), lambda bi, n: (0, n, 0, 0))
    in_specs = [pl.BlockSpec((1, n_x, bw), col(COL_RG_X)), pl.BlockSpec((1, n_c, bw), col(COL_RG_X)),
                pl.BlockSpec((1, n_x, bw), col(COL_RG_G))]
    args = [zx, zc, zx]
    if ctx_out:
        in_specs.append(pl.BlockSpec((1, n_c, bw), col(COL_RG_G)))
        args.append(zc)
    in_specs += [pl.BlockSpec((4, bw), lambda bi, n: (0, n)), pl.BlockSpec((1, bw), lambda bi, n: (0, n)),
                 wspec, vec2, wspec, vec2, vec2]
    args += [conv_w, conv_b.reshape(1, RG_WIDTH), w_a, b_a, w_x, b_x, lam]
    out_specs = [pl.BlockSpec((1, n_x, bw), lambda bi, n: (bi, 0, n))]
    out_shape = [jax.ShapeDtypeStruct((b, n_x, RG_WIDTH), BF16)]
    if ctx_out:
        out_specs.append(pl.BlockSpec((1, n_c, bw), lambda bi, n: (bi, 0, n)))
        out_shape.append(jax.ShapeDtypeStruct((b, n_c, RG_WIDTH), BF16))
    n_s = SCAN_CHUNKS * ((n_x + n_c) // SCAN_CHUNKS + SCAN_PAD_ROWS)
    outs = pl.pallas_call(
        functools.partial(_rg_kernel, n_x=n_x, n_c=n_c, ctx_out=ctx_out),
        grid=(b, RG_BLOCKS),
        in_specs=in_specs,
        out_specs=out_specs,
        out_shape=out_shape,
        scratch_shapes=[pltpu.VMEM((2, n_s, bw), F32), pltpu.VMEM((2, n_s, bw), F32), pltpu.VMEM((2, n_s, bw), F32)],
        compiler_params=_cparams("parallel", "parallel"),
        name="rg_lru",
    )(*args)
    return outs if ctx_out else (outs[0], None)


def _merge_kernel(ya_ref, yb_ref, yc_ref, ga_ref, gb_ref, gc_ref, wa_ref, wb_ref, wc_ref, o_ref):
    gate = lambda ref: jax.nn.sigmoid(ref[0].astype(F32))
    m = gate(ga_ref) * jnp.dot(ya_ref[0], wa_ref[...], preferred_element_type=F32)
    m = m + gate(gb_ref) * jnp.dot(yb_ref[0], wb_ref[...], preferred_element_type=F32)
    m = m + gate(gc_ref) * jnp.dot(yc_ref[0], wc_ref[...], preferred_element_type=F32)
    o_ref[0] = m.astype(BF16)


def _branch_merge(ya, yb, yc, z, w_branch, tm, tn):
    b, t, _ = ya.shape
    d = D_MODEL
    wa, wb, wc = w_branch[:512], w_branch[512:1024], w_branch[1024:]
    gate = lambda k: (lambda bi, i, j: (bi, i, (COL_GATE + k * d) // tn + j))
    return pl.pallas_call(
        _merge_kernel,
        grid=(b, t // tm, d // tn),
        in_specs=[
            pl.BlockSpec((1, tm, 512), lambda bi, i, j: (bi, i, 0)),
            pl.BlockSpec((1, tm, 512), lambda bi, i, j: (bi, i, 0)),
            pl.BlockSpec((1, tm, 1024), lambda bi, i, j: (bi, i, 0)),
            pl.BlockSpec((1, tm, tn), gate(0)),
            pl.BlockSpec((1, tm, tn), gate(1)),
            pl.BlockSpec((1, tm, tn), gate(2)),
            pl.BlockSpec((512, tn), lambda bi, i, j: (0, j)),
            pl.BlockSpec((512, tn), lambda bi, i, j: (0, j)),
            pl.BlockSpec((1024, tn), lambda bi, i, j: (0, j)),
        ],
        out_specs=pl.BlockSpec((1, tm, tn), lambda bi, i, j: (bi, i, j)),
        out_shape=jax.ShapeDtypeStruct((b, t, d), BF16),
        compiler_params=_cparams("parallel", "parallel", "arbitrary"),
        name="branch_merge",
    )(ya, yb, yc, z, z, z, wa, wb, wc)


def _resid_kernel(m_ref, w_ref, x_ref, gt_ref, o_ref):
    o_ref[0] = x_ref[0] + gt_ref[0] * jnp.dot(m_ref[0], w_ref[...], preferred_element_type=F32)


def _out_proj_residual(m, w_out, x, gt, tm, tn):
    b, t, d = x.shape
    return pl.pallas_call(
        _resid_kernel,
        grid=(b, t // tm, d // tn),
        in_specs=[
            pl.BlockSpec((1, tm, d), lambda bi, i, j: (bi, i, 0)),
            pl.BlockSpec((d, tn), lambda bi, i, j: (0, j)),
            pl.BlockSpec((1, tm, tn), lambda bi, i, j: (bi, i, j)),
            pl.BlockSpec((1, 1, tn), lambda bi, i, j: (bi, 0, j)),
        ],
        out_specs=pl.BlockSpec((1, tm, tn), lambda bi, i, j: (bi, i, j)),
        out_shape=jax.ShapeDtypeStruct((b, t, d), F32),
        compiler_params=_cparams("parallel", "parallel", "arbitrary"),
        name="out_proj_residual",
    )(m, w_out, x, gt)


def _router_kernel(*refs, starts):
    n_s = len(starts) - 1
    g_ref, wh_ref, wl_ref, br_ref, h_ref, eid_ref, wt_ref = refs[3 * n_s:]
    i = pl.program_id(0)
    for k in range(n_s):
        x_ref, sc_ref, sh_ref = refs[3 * k:3 * k + 3]

        @pl.when((i >= starts[k]) & (i < starts[k + 1]))
        def _(x_ref=x_ref, sc_ref=sc_ref, sh_ref=sh_ref):
            _route_rows(x_ref, sc_ref, sh_ref, g_ref, wh_ref, wl_ref, br_ref, h_ref, eid_ref, wt_ref)


def _route_rows(x_ref, sc_ref, sh_ref, g_ref, wh_ref, wl_ref, br_ref, h_ref, eid_ref, wt_ref):
    h = _rms(x_ref[0], g_ref[...]) * (1.0 + sc_ref[0]) + sh_ref[0]
    h_ref[...] = h
    hh = h.astype(BF16)
    hl = (h - hh.astype(F32)).astype(BF16)
    logits = (jnp.dot(hh, wh_ref[...], preferred_element_type=F32)
              + jnp.dot(hl, wh_ref[...], preferred_element_type=F32)
              + jnp.dot(hh, wl_ref[...], preferred_element_type=F32)) + br_ref[...]
    lane = lax.broadcasted_iota(jnp.int32, logits.shape, 1)
    lane_f = lane.astype(F32)

    def first_argmax(v, valid):
        vm = jnp.where(valid, v, -jnp.inf)
        mx = jnp.max(vm, axis=-1, keepdims=True)
        idx = jnp.min(jnp.where(valid & (vm == mx), lane_f, 1e9), axis=-1, keepdims=True)
        return mx, idx.astype(jnp.int32)

    is_g = lane < N_GROUPS
    gmax, grp = first_argmax(logits, is_g)
    p_grp = 1.0 / jnp.sum(jnp.where(is_g, jnp.exp(logits - gmax), 0.0), axis=-1, keepdims=True)
    e_lo = N_GROUPS + grp * EXPERTS_PER_GROUP
    in_grp = (lane >= e_lo) & (lane < e_lo + EXPERTS_PER_GROUP)
    v0, i0 = first_argmax(logits, in_grp)
    v1, i1 = first_argmax(logits, in_grp & (lane != i0))
    e1 = jnp.exp(v1 - v0)
    w0 = p_grp / (1.0 + e1)
    w1 = p_grp * e1 / (1.0 + e1)
    eid_ref[...] = jnp.where(lane == 0, i0 - N_GROUPS, jnp.where(lane == 1, i1 - N_GROUPS, 0))
    wt_ref[...] = jnp.where(lane == 0, w0, jnp.where(lane == 1, w1, 0.0))


def _router(xs, scs, shs, g, wr_hi, wr_lo, br, tm):
    d = xs[0].shape[-1]
    starts = [0]
    in_specs, args = [], []
    for x, sc, sh in zip(xs, scs, shs):
        b, t, _ = x.shape
        n_i = t // tm
        s0 = starts[-1]
        starts.append(s0 + b * n_i)

        def local(i, s0=s0, n=b * n_i):
            return jnp.clip(i - s0, 0, n - 1)

        in_specs += [pl.BlockSpec((1, tm, d), lambda i, f=local, n_i=n_i: (f(i) // n_i, f(i) % n_i, 0)),
                     pl.BlockSpec((1, 1, d), lambda i, f=local, n_i=n_i: (f(i) // n_i, 0, 0)),
                     pl.BlockSpec((1, 1, d), lambda i, f=local, n_i=n_i: (f(i) // n_i, 0, 0))]
        args += [x, sc, sh]
    n_all = starts[-1] * tm
    in_specs += [pl.BlockSpec((1, d), lambda i: (0, 0)), pl.BlockSpec((d, 128), lambda i: (0, 0)),
                 pl.BlockSpec((d, 128), lambda i: (0, 0)), pl.BlockSpec((1, 128), lambda i: (0, 0))]
    args += [g.reshape(1, d), wr_hi, wr_lo, br]
    flat = lambda width: pl.BlockSpec((tm, width), lambda i: (i, 0))
    return pl.pallas_call(
        functools.partial(_router_kernel, starts=tuple(starts)),
        grid=(starts[-1],),
        in_specs=in_specs,
        out_specs=[flat(d), flat(128), flat(128)],
        out_shape=[jax.ShapeDtypeStruct((n_all, d), F32), jax.ShapeDtypeStruct((n_all, 128), jnp.int32),
                   jax.ShapeDtypeStruct((n_all, 128), F32)],
        compiler_params=_cparams("parallel"),
        name="moe_router",
    )(*args)


def _row_copy(src_hbm, dst_vmem, sem, src_row, dst_row):
    return pltpu.make_async_copy(src_hbm.at[pl.ds(src_row, 1)], dst_vmem.at[pl.ds(dst_row, 1)], sem)


def _expert_kernel(blk_e_ref, tok_ref, tok_next_ref, x_hbm, w1_ref, w3_ref, w2_ref, o_ref, xbuf0, xbuf1, sem):
    del blk_e_ref
    tm = xbuf0.shape[0]
    i = pl.program_id(0)
    n = pl.num_programs(0)
    bufs = (xbuf0, xbuf1)

    def wait_rows(b):
        def wait(r, c):
            _row_copy(x_hbm, bufs[b], sem.at[b], 0, r).wait()
            return c
        lax.fori_loop(0, tm, wait, 0, unroll=8)

    @pl.when(i == 0)
    def _():
        def issue(r, c):
            _row_copy(x_hbm, xbuf0, sem.at[0], tok_ref[0, 0, r], r).start()
            return c
        lax.fori_loop(0, tm, issue, 0, unroll=8)

    def block(cur):
        wait_rows(cur)

        @pl.when(i >= 0)
        def _():
            for r in range(tm):
                _row_copy(x_hbm, bufs[1 - cur], sem.at[1 - cur], tok_next_ref[0, 0, r], r).start(priority=r % 2)

        xb = bufs[cur][...].astype(BF16)
        h1 = jnp.dot(xb, w1_ref[0], preferred_element_type=F32)
        h3 = jnp.dot(xb, w3_ref[0], preferred_element_type=F32)
        hid = (h1 * jax.nn.sigmoid(h1) * h3).astype(BF16)
        o_ref[...] = jnp.dot(hid, w2_ref[0], preferred_element_type=F32)

    for parity in range(2):
        @pl.when(i % 2 == parity)
        def _(parity=parity):
            block(parity)

        @pl.when((i == n - 1) & (i % 2 == parity))
        def _(parity=parity):
            wait_rows(1 - parity)


def _expert_blocks(blk_e, buf_tok, x_rows, w1, w3, w2):
    n_blk = blk_e.shape[0]
    tm = MOE_TM
    d = x_rows.shape[1]
    grid_spec = pltpu.PrefetchScalarGridSpec(
        num_scalar_prefetch=1,
        grid=(n_blk,),
        in_specs=[
            pl.BlockSpec((1, 1, tm), lambda i, e: (i, 0, 0), memory_space=pltpu.SMEM),
            pl.BlockSpec((1, 1, tm), lambda i, e: (jnp.minimum(i + 1, n_blk - 1), 0, 0), memory_space=pltpu.SMEM),
            pl.BlockSpec(memory_space=pl.ANY),
            pl.BlockSpec((1, d, D_EXPERT), lambda i, e: (e[i], 0, 0)),
            pl.BlockSpec((1, d, D_EXPERT), lambda i, e: (e[i], 0, 0)),
            pl.BlockSpec((1, D_EXPERT, d), lambda i, e: (e[i], 0, 0)),
        ],
        out_specs=pl.BlockSpec((tm, d), lambda i, e: (i, 0)),
        scratch_shapes=[pltpu.VMEM((tm, d), F32), pltpu.VMEM((tm, d), F32), pltpu.SemaphoreType.DMA((2,))],
    )
    tok = buf_tok.reshape(n_blk, 1, tm)
    return pl.pallas_call(
        _expert_kernel,
        grid_spec=grid_spec,
        out_shape=jax.ShapeDtypeStruct((n_blk * tm, d), F32),
        compiler_params=_cparams("arbitrary"),
        name="moe_experts",
    )(blk_e, tok, tok, x_rows, w1, w3, w2)


def _combine_kernel(slot_ref, slot_next_ref, y_hbm, x_ref, gt_ref, wt_ref, o_ref, a0, b0, a1, b1, sem):
    tm = a0.shape[0]
    i = pl.program_id(0)
    n = pl.num_programs(0)
    bufs = ((a0, b0), (a1, b1))

    def wait_rows(p):
        def wait(r, c):
            _row_copy(y_hbm, bufs[p][0], sem.at[p], 0, r).wait()
            _row_copy(y_hbm, bufs[p][1], sem.at[p], 0, r).wait()
            return c
        lax.fori_loop(0, tm, wait, 0, unroll=8)

    @pl.when(i == 0)
    def _():
        def issue(r, c):
            _row_copy(y_hbm, a0, sem.at[0], slot_ref[0, 0, 2 * r], r).start()
            _row_copy(y_hbm, b0, sem.at[0], slot_ref[0, 0, 2 * r + 1], r).start()
            return c
        lax.fori_loop(0, tm, issue, 0, unroll=8)

    def block(cur):
        wait_rows(cur)
        nxt_a, nxt_b = bufs[1 - cur]
        for r in range(tm):
            _row_copy(y_hbm, nxt_a, sem.at[1 - cur], slot_next_ref[0, 0, 2 * r], r).start(priority=0)
            _row_copy(y_hbm, nxt_b, sem.at[1 - cur], slot_next_ref[0, 0, 2 * r + 1], r).start(priority=1)
        wt = wt_ref[...]
        f = bufs[cur][0][...] * wt[:, 0:1] + bufs[cur][1][...] * wt[:, 1:2]
        o_ref[...] = x_ref[...] + gt_ref[0] * f

    for parity in range(2):
        @pl.when(i % 2 == parity)
        def _(parity=parity):
            block(parity)

        @pl.when((i == n - 1) & (i % 2 == parity))
        def _(parity=parity):
            wait_rows(1 - parity)


def _combine(slots, y, x, gt, wt, tok0, tm):
    b, t, d = x.shape
    n_i = t // tm
    n = b * n_i
    x2 = x.reshape(b * t, d)
    blk0 = tok0 // tm
    slots3 = slots.reshape(-1, 1, 2 * tm)
    out = pl.pallas_call(
        _combine_kernel,
        grid=(n,),
        in_specs=[
            pl.BlockSpec((1, 1, 2 * tm), lambda i: (blk0 + i, 0, 0), memory_space=pltpu.SMEM),
            pl.BlockSpec((1, 1, 2 * tm), lambda i: (blk0 + jnp.minimum(i + 1, n - 1), 0, 0),
                         memory_space=pltpu.SMEM),
            pl.BlockSpec(memory_space=pl.ANY),
            pl.BlockSpec((tm, d), lambda i: (i, 0)),
            pl.BlockSpec((1, 1, d), lambda i: (i // n_i, 0, 0)),
            pl.BlockSpec((tm, 128), lambda i: (blk0 + i, 0)),
        ],
        out_specs=pl.BlockSpec((tm, d), lambda i: (i, 0)),
        out_shape=jax.ShapeDtypeStruct((b * t, d), F32),
        scratch_shapes=[pltpu.VMEM((tm, d), F32)] * 4 + [pltpu.SemaphoreType.DMA((2,))],
        compiler_params=_cparams("arbitrary"),
        name="moe_combine",
    )(slots3, slots3, y, x2, gt, wt)
    return out.reshape(b, t, d)


def _routing_tables(eid):
    m = eid.shape[0]
    tm = MOE_TM
    i32 = jnp.int32
    iota = jnp.arange(m, dtype=i32)
    se, order = lax.sort_key_val(eid, iota)
    onehot = (se[:, None] == jnp.arange(N_EXPERTS, dtype=i32)[None, :]).astype(i32)
    counts = jnp.sum(onehot, axis=0)
    starts = jnp.cumsum(counts) - counts
    pcounts = (counts + tm - 1) // tm * tm
    pends = jnp.cumsum(pcounts)
    pstarts = pends - pcounts
    dest = iota + jnp.sum(onehot * (pstarts - starts)[None, :], axis=1)
    _, slot = lax.sort_key_val(order, dest)
    n_blk = (m + N_EXPERTS * (tm - 1) + tm - 1) // tm
    blk_start = jnp.arange(n_blk, dtype=i32) * tm
    blk_e = jnp.minimum(jnp.sum((pends[None, :] <= blk_start[:, None]).astype(i32), axis=1), N_EXPERTS - 1)
    off = (blk_start - pstarts[blk_e])[:, None] + jnp.arange(tm, dtype=i32)[None, :]
    valid = off < counts[blk_e][:, None]
    src = jnp.clip(starts[blk_e][:, None] + off, 0, m - 1)
    buf_tok = jnp.where(valid, order[src] // TOP_K, 0).reshape(-1)
    return blk_e.astype(i32), buf_tok, slot


def _moe(xs, scs, shs, gts, norm_g, wr_hi, wr_lo, br, w1, w3, w2):
    h_all, eid_all, wt_all = _router(xs, scs, shs, norm_g, wr_hi, wr_lo, br, 512)
    blk_e, buf_tok, slot = _routing_tables(eid_all[:, :TOP_K].reshape(-1))
    y = _expert_blocks(blk_e, buf_tok, h_all, w1, w3, w2)
    outs = []
    tok0 = 0
    for x, gt in zip(xs, gts):
        outs.append(_combine(slot, y, x, gt, wt_all, tok0, tm=256))
        tok0 += x.shape[0] * x.shape[1]
    return outs


def kernel(x, c, ctx, c_ctx, w_mod, b_mod, norm1_g, norm2_g, w_in, na_q_g, na_k_g, na_rpb, df_q_g, df_k_g, df_lam, df_sub_g, rg_conv_w, rg_conv_b, rg_w_a, rg_b_a, rg_w_x, rg_b_x, rg_lam, w_branch, w_out, w_router_g, b_router_g, w_router_e, b_router_e, w1, w3, w2):
    B, S, D = x.shape
    C = ctx.shape[1]
    n_rows = S // GRID_W
    rope_cos, rope_sin = _rope_tables(S)
    na_bias = _na_bias(na_rpb, n_rows)

    pad = (-(B + 1)) % 8
    rows = jnp.concatenate([c, c_ctx[None, :], jnp.zeros((pad, D), F32)], axis=0)
    mod_all = _modulation(rows, w_mod, b_mod)

    xc = ctx.reshape(1, B * C, D)
    for l in range(DEPTH):
        need_ctx = l < DEPTH - 1
        lam_init = 0.8 - 0.6 * float(np.exp(-0.3 * l))
        mod = mod_all[l]
        sh1, sc1, gt1, sh2, sc2, gt2 = [mod[:B, k * D:(k + 1) * D][:, None, :] for k in range(6)]
        csh1, csc1, cgt1, csh2, csc2, cgt2 = [mod[B:B + 1, k * D:(k + 1) * D][:, None, :] for k in range(6)]

        w_in_l = w_in[l].astype(BF16)
        zx = _norm_mod_matmul(x, norm1_g[l], sc1, sh1, w_in_l, tm=1024, tn=1024)
        n_cc = IN_COLS if need_ctx else KV_COLS
        zc = _norm_mod_matmul(xc, norm1_g[l], csc1, csh1, w_in_l[:, :n_cc], tm=1024, tn=1024).reshape(B, C, n_cc)

        y_a = _na_attention(zx, zc, na_bias[l], na_q_g[l], na_k_g[l])

        lp = df_lam[l]
        lam = jnp.exp(jnp.sum(lp[0] * lp[1])) - jnp.exp(jnp.sum(lp[2] * lp[3])) + lam_init
        y_b = _df_attention(lam, zx, zx, zc, rope_cos, rope_sin, df_q_g[l], df_k_g[l], df_sub_g[l], 1.0 - lam_init)

        y_c, y_cc = _rg_lru(zx, zc, rg_conv_w[l], rg_conv_b[l], rg_w_a[l].astype(BF16), rg_b_a[l],
                            rg_w_x[l].astype(BF16), rg_b_x[l], rg_lam[l], need_ctx)

        wb_l = w_branch[l].astype(BF16)
        wo_l = w_out[l].astype(BF16)
        m_x = _branch_merge(y_a, y_b, y_c, zx, wb_l, tm=1024, tn=1024)
        x = _out_proj_residual(m_x, wo_l, x, gt1, tm=1024, tn=1024)
        if need_ctx:
            y_ac = _na_ctx_attention(zc, na_q_g[l], na_k_g[l])
            y_bc = _df_attention(lam, zc, None, zc, None, None, df_q_g[l], df_k_g[l], df_sub_g[l], 1.0 - lam_init)
            m_c = _branch_merge(y_ac, y_bc, y_cc, zc, wb_l, tm=C, tn=1024)
            xc = _out_proj_residual(m_c.reshape(1, B * C, D), wo_l, xc, cgt1, tm=1024, tn=1024)

        wr = jnp.concatenate([w_router_g[l], w_router_e[l],
                              jnp.zeros((D, 128 - N_GROUPS - N_EXPERTS), F32)], axis=1)
        wr_hi = wr.astype(BF16)
        wr_lo = (wr - wr_hi.astype(F32)).astype(BF16)
        br = jnp.concatenate([b_router_g[l], b_router_e[l],
                              jnp.zeros((128 - N_GROUPS - N_EXPERTS,), F32)]).reshape(1, 128)
        w1_l, w3_l, w2_l = w1[l].astype(BF16), w3[l].astype(BF16), w2[l].astype(BF16)
        if need_ctx:
            xc, x = _moe([xc, x], [csc2, sc2], [csh2, sh2], [cgt2, gt2], norm2_g[l],
                         wr_hi, wr_lo, br, w1_l, w3_l, w2_l)
        else:
            (x,) = _moe([x], [sc2], [sh2], [gt2], norm2_g[l], wr_hi, wr_lo, br, w1_l, w3_l, w2_l)
    return x
```

```python
import functools

import jax
import jax.numpy as jnp
import numpy as np
from jax import lax
from jax.experimental import pallas as pl
from jax.experimental.pallas import tpu as pltpu

F32 = jnp.float32
BF16 = jnp.bfloat16

D_MODEL = 2048
DEPTH = 4
GRID_W = 64
HEAD_DIM = 128
N_HEADS = 4
NA_WIN_R = 8
NA_WIN_C = 16
NA_QCB = 16
NA_KCB = 32
DF_DIM = 64
RG_WIDTH = 1024
RG_BLOCKS = 8
RG_BW = 128
RG_C = 8.0
N_GROUPS = 4
EXPERTS_PER_GROUP = 8
N_EXPERTS = 32
TOP_K = 2
D_EXPERT = 512
ROPE_BASE = 10000.0
EPS = 1e-6
NEG = -1e30
LOG2_E = 1.4426950408889634

COL_NA_K, COL_NA_V, COL_DF_K, COL_DF_V, COL_RG_X = 0, 512, 1024, 1536, 2048
COL_NA_Q, COL_DF_Q, COL_RG_G, COL_GATE = 3072, 3584, 4096, 5120
KV_COLS = 3072
MIX_COLS = 5120
IN_COLS = MIX_COLS + 3 * D_MODEL

VMEM_LIMIT_BYTES = 56 * 1024 * 1024

NA_ROWS_PER_STEP = 4
NA_KEY_ROWS = NA_ROWS_PER_STEP + NA_WIN_R
DF_QBLK = 1024
DF_KCHUNK = 768
MOE_TM = 256
MOE_DISPATCH_ROWS = 1024
SCAN_CHUNKS = 8
SCAN_PAD_ROWS = 8


def _cparams(*sem):
    return pltpu.CompilerParams(dimension_semantics=sem, vmem_limit_bytes=VMEM_LIMIT_BYTES)


def _rms(x, g):
    x = x.astype(F32)
    return x * lax.rsqrt(jnp.mean(x * x, axis=-1, keepdims=True) + EPS) * g


def _mod_kernel(s_ref, w_ref, b_ref, o_ref):
    s = s_ref[...]
    a = (s * jax.nn.sigmoid(s)).astype(BF16)
    o_ref[0] = jnp.dot(a, w_ref[0].astype(BF16), preferred_element_type=F32) + b_ref[0]


def _modulation(rows, w_mod, b_mod, tn=1024):
    n_l, d, n = w_mod.shape
    r = rows.shape[0]
    return pl.pallas_call(
        _mod_kernel,
        grid=(n_l, n // tn),
        in_specs=[
            pl.BlockSpec((r, d), lambda l, j: (0, 0)),
            pl.BlockSpec((1, d, tn), lambda l, j: (l, 0, j)),
            pl.BlockSpec((1, 1, tn), lambda l, j: (l, 0, j)),
        ],
        out_specs=pl.BlockSpec((1, r, tn), lambda l, j: (l, 0, j)),
        out_shape=jax.ShapeDtypeStruct((n_l, r, n), F32),
        compiler_params=_cparams("parallel", "parallel"),
        name="modulation",
    )(rows, w_mod, b_mod.reshape(n_l, 1, n))


def _nmm_kernel(x_ref, g_ref, sc_ref, sh_ref, w_ref, o_ref, h_ref):
    @pl.when(pl.program_id(2) == 0)
    def _():
        h = _rms(x_ref[0], g_ref[...]) * (1.0 + sc_ref[0]) + sh_ref[0]
        h_ref[...] = h.astype(BF16)

    o_ref[0] = jnp.dot(h_ref[...], w_ref[...], preferred_element_type=F32).astype(o_ref.dtype)


def _norm_mod_matmul(x, g, sc, sh, w, tm, tn):
    b, t, d = x.shape
    n = w.shape[1]
    return pl.pallas_call(
        _nmm_kernel,
        grid=(b, t // tm, n // tn),
        in_specs=[
            pl.BlockSpec((1, tm, d), lambda bi, i, j: (bi, i, 0)),
            pl.BlockSpec((1, d), lambda bi, i, j: (0, 0)),
            pl.BlockSpec((1, 1, d), lambda bi, i, j: (bi, 0, 0)),
            pl.BlockSpec((1, 1, d), lambda bi, i, j: (bi, 0, 0)),
            pl.BlockSpec((d, tn), lambda bi, i, j: (0, j)),
        ],
        out_specs=pl.BlockSpec((1, tm, tn), lambda bi, i, j: (bi, i, j)),
        out_shape=jax.ShapeDtypeStruct((b, t, n), BF16),
        scratch_shapes=[pltpu.VMEM((tm, d), BF16)],
        compiler_params=_cparams("parallel", "parallel", "arbitrary"),
        name="norm_mod_proj",
    )(x, g.reshape(1, d), sc, sh, w)


def _na_window_start(r0):
    lower = jnp.clip(r0 - NA_WIN_R // 2, 0, (2048 // GRID_W) - NA_WIN_R)
    return jnp.minimum(lower, (2048 // GRID_W) - NA_KEY_ROWS)


def _na_bias_tables(rpb):
    w = GRID_W
    qc = np.arange(w)
    kc = np.arange(w)
    qwin = np.clip(qc - NA_WIN_C // 2, 0, w - NA_WIN_C)
    kcol_start = np.clip((qc // NA_QCB) * NA_QCB - (NA_KCB - NA_QCB) // 2, 0, w - NA_KCB)
    col_ok = (kc[None, :] >= qwin[:, None]) & (kc[None, :] < qwin[:, None] + NA_WIN_C)
    col_ok &= (kc[None, :] >= kcol_start[:, None]) & (kc[None, :] < kcol_start[:, None] + NA_KCB)
    edge = w - NA_WIN_C
    lead = rpb.shape[:-1]
    ext = jnp.concatenate([jnp.repeat(rpb[..., :1], edge, axis=-1), rpb, jnp.repeat(rpb[..., -1:], edge, axis=-1),
                           jnp.zeros(lead + (1,), F32)], axis=-1)
    skew = jnp.tile(ext, (1,) * len(lead) + (w,))[..., :w * (2 * w - 1)].reshape(lead + (w, 2 * w - 1))
    toep = jnp.where(col_ok, skew[..., w - 1:], NEG)
    toep = jnp.concatenate([toep, jnp.full(toep.shape[:-3] + (1, w, w), NEG, F32)], axis=-3)
    zeros = jnp.zeros_like(toep)
    return jnp.concatenate([toep, zeros], axis=-1), jnp.concatenate([zeros, toep], axis=-1)


def _na_bias_blocks(n_rows):
    rb, kw = NA_ROWS_PER_STEP, NA_KEY_ROWS
    rows = np.arange(n_rows)
    row_start = np.clip(rows - NA_WIN_R // 2, 0, n_rows - NA_WIN_R)
    n_steps = n_rows // rb
    blk = np.full((n_steps, rb, kw), 2 * NA_WIN_R - 1, np.int32)
    for s in range(n_steps):
        ws = min(int(np.clip(s * rb - NA_WIN_R // 2, 0, n_rows - NA_WIN_R)), n_rows - kw)
        for j in range(rb):
            r = s * rb + j
            for i in range(kw):
                if row_start[r] <= ws + i < row_start[r] + NA_WIN_R:
                    blk[s, j, i] = ws + i - r + (NA_WIN_R - 1)
    return blk.reshape(-1)


def _na_kernel(blk_ref, q_ref, k_ref, v_ref, kc_ref, vc_ref, lo_ref, hi_ref, gq_ref, gk_ref, o_ref,
               kn, vn, knc, vnc):
    step = pl.program_id(1)

    def bias(h):
        rows = []
        for j in range(NA_ROWS_PER_STEP):
            base = (step * NA_ROWS_PER_STEP + j) * NA_KEY_ROWS
            rows.append(jnp.concatenate(
                [lo_ref[h, blk_ref[base + i]] + hi_ref[h, blk_ref[base + i + 1]] for i in range(0, NA_KEY_ROWS, 2)],
                axis=-1))
        return jnp.concatenate(rows, axis=0)

    @pl.when(step == 0)
    def _():
        for h in range(N_HEADS):
            sl = slice(h * HEAD_DIM, (h + 1) * HEAD_DIM)
            kn[:, sl] = _rms(k_ref[0, :, sl], gk_ref[...]).astype(BF16)
            knc[:, sl] = _rms(kc_ref[0, :, sl], gk_ref[...]).astype(BF16)
        vn[...] = v_ref[0].astype(BF16)
        vnc[...] = vc_ref[0].astype(BF16)

    ws = _na_window_start(step * NA_ROWS_PER_STEP)
    start = pl.multiple_of(ws * GRID_W, GRID_W)
    nk = NA_KEY_ROWS * GRID_W
    scale = HEAD_DIM ** -0.5
    nt = (((1,), (1,)), ((), ()))
    for h in range(N_HEADS):
        sl = slice(h * HEAD_DIM, (h + 1) * HEAD_DIM)
        qn = (_rms(q_ref[0, :, sl], gq_ref[...]) * scale).astype(BF16)
        s_loc = lax.dot_general(qn, kn[pl.ds(start, nk), sl], nt, preferred_element_type=F32) + bias(h)
        s_ctx = lax.dot_general(qn, knc[:, sl], nt, preferred_element_type=F32)
        m = jnp.maximum(jnp.max(s_loc, axis=-1, keepdims=True), jnp.max(s_ctx, axis=-1, keepdims=True))
        p_loc = jnp.exp(s_loc - m)
        p_ctx = jnp.exp(s_ctx - m)
        l = jnp.sum(p_loc, axis=-1, keepdims=True) + jnp.sum(p_ctx, axis=-1, keepdims=True)
        o = jnp.dot(p_loc.astype(BF16), vn[pl.ds(start, nk), sl], preferred_element_type=F32)
        o = o + jnp.dot(p_ctx.astype(BF16), vnc[:, sl], preferred_element_type=F32)
        o_ref[0, :, sl] = (o * (1.0 / l)).astype(BF16)


def _na_attention(zx, zc, bias_lo, bias_hi, gq, gk):
    b, s, _ = zx.shape
    c = zc.shape[1]
    w = N_HEADS * HEAD_DIM
    tq = NA_ROWS_PER_STEP * GRID_W
    blk = jnp.asarray(_na_bias_blocks(s // GRID_W))
    table = pl.BlockSpec(bias_lo.shape, lambda bi, i, blk: (0, 0, 0, 0))
    grid_spec = pltpu.PrefetchScalarGridSpec(
        num_scalar_prefetch=1,
        grid=(b, s // tq),
        in_specs=[
            pl.BlockSpec((1, tq, w), lambda bi, i, blk: (bi, i, COL_NA_Q // w)),
            pl.BlockSpec((1, s, w), lambda bi, i, blk: (bi, 0, COL_NA_K // w)),
            pl.BlockSpec((1, s, w), lambda bi, i, blk: (bi, 0, COL_NA_V // w)),
            pl.BlockSpec((1, c, w), lambda bi, i, blk: (bi, 0, COL_NA_K // w)),
            pl.BlockSpec((1, c, w), lambda bi, i, blk: (bi, 0, COL_NA_V // w)),
            table, table,
            pl.BlockSpec((1, HEAD_DIM), lambda bi, i, blk: (0, 0)),
            pl.BlockSpec((1, HEAD_DIM), lambda bi, i, blk: (0, 0)),
        ],
        out_specs=pl.BlockSpec((1, tq, w), lambda bi, i, blk: (bi, i, 0)),
        scratch_shapes=[pltpu.VMEM((s, w), BF16), pltpu.VMEM((s, w), BF16),
                        pltpu.VMEM((c, w), BF16), pltpu.VMEM((c, w), BF16)],
    )
    return pl.pallas_call(
        _na_kernel,
        grid_spec=grid_spec,
        out_shape=jax.ShapeDtypeStruct((b, s, w), BF16),
        compiler_params=_cparams("parallel", "arbitrary"),
        name="na_attention",
    )(blk, zx, zx, zx, zc, zc, bias_lo, bias_hi, gq.reshape(1, HEAD_DIM), gk.reshape(1, HEAD_DIM))


def _na_ctx_kernel(q_ref, k_ref, v_ref, gq_ref, gk_ref, o_ref):
    scale = HEAD_DIM ** -0.5
    nt = (((1,), (1,)), ((), ()))
    for h in range(N_HEADS):
        sl = slice(h * HEAD_DIM, (h + 1) * HEAD_DIM)
        qn = (_rms(q_ref[0, :, sl], gq_ref[...]) * scale).astype(BF16)
        kn = _rms(k_ref[0, :, sl], gk_ref[...]).astype(BF16)
        s = lax.dot_general(qn, kn, nt, preferred_element_type=F32)
        p = jnp.exp(s - jnp.max(s, axis=-1, keepdims=True))
        l = jnp.sum(p, axis=-1, keepdims=True)
        o = jnp.dot(p.astype(BF16), v_ref[0, :, sl].astype(BF16), preferred_element_type=F32)
        o_ref[0, :, sl] = (o * (1.0 / l)).astype(BF16)


def _na_ctx_attention(zc, gq, gk):
    b, c, _ = zc.shape
    w = N_HEADS * HEAD_DIM
    return pl.pallas_call(
        _na_ctx_kernel,
        grid=(b,),
        in_specs=[
            pl.BlockSpec((1, c, w), lambda bi: (bi, 0, COL_NA_Q // w)),
            pl.BlockSpec((1, c, w), lambda bi: (bi, 0, COL_NA_K // w)),
            pl.BlockSpec((1, c, w), lambda bi: (bi, 0, COL_NA_V // w)),
            pl.BlockSpec((1, HEAD_DIM), lambda bi: (0, 0)),
            pl.BlockSpec((1, HEAD_DIM), lambda bi: (0, 0)),
        ],
        out_specs=pl.BlockSpec((1, c, w), lambda bi: (bi, 0, 0)),
        out_shape=jax.ShapeDtypeStruct((b, c, w), BF16),
        compiler_params=_cparams("parallel"),
        name="na_ctx_attention",
    )(zc, zc, zc, gq.reshape(1, HEAD_DIM), gk.reshape(1, HEAD_DIM))


def _rope_tables(n_tok):
    t = np.arange(n_tok)
    pos = np.stack([t // GRID_W, t % GRID_W], axis=0).astype(np.float32)
    n_freq = DF_DIM // 4
    inv = (np.float32(ROPE_BASE) ** (-np.arange(n_freq, dtype=np.float32) / n_freq)).astype(np.float32)
    ang = jnp.asarray(pos[:, :, None] * inv)
    cos, sin = jnp.cos(ang), jnp.sin(ang)
    cos64 = jnp.concatenate([cos[0], cos[0], cos[1], cos[1]], axis=-1)
    sin64 = jnp.concatenate([-sin[0], sin[0], -sin[1], sin[1]], axis=-1)
    return jnp.concatenate([cos64, cos64], axis=-1), jnp.concatenate([sin64, sin64], axis=-1)


def _df_kernel(lam_ref, q_ref, *refs, n_x, n_c, rope, out_scale):
    if n_x:
        kx_ref, vx_ref, kc_ref, vc_ref, cq_ref, sq_ref, ck_ref, sk_ref, gq_ref, gk_ref, gs_ref, o_ref, kn, vn = refs
    else:
        kc_ref, vc_ref, gq_ref, gk_ref, gs_ref, o_ref, kn, vn = refs
    lane = lax.broadcasted_iota(jnp.int32, (1, 2 * DF_DIM), 1)
    lo = lane < DF_DIM
    first = (lane % (DF_DIM // 2)) < (DF_DIM // 4)

    def norm64(x, g):
        x = x.astype(F32)
        x2 = x * x
        s0 = jnp.sum(jnp.where(lo, x2, 0.0), axis=-1, keepdims=True)
        s1 = jnp.sum(jnp.where(lo, 0.0, x2), axis=-1, keepdims=True)
        ms = jnp.where(lo, s0, s1) * (1.0 / DF_DIM)
        return x * lax.rsqrt(ms + EPS) * g

    def rot(x, cos, sin):
        partner = jnp.where(first, pltpu.roll(x, 2 * DF_DIM - DF_DIM // 4, 1), pltpu.roll(x, DF_DIM // 4, 1))
        return x * cos + partner * sin

    @pl.when(pl.program_id(2) == 0)
    def _():
        if n_x:
            kx = rot(norm64(kx_ref[0], gk_ref[...]), ck_ref[...], sk_ref[...])
            kn[0:n_x, :] = kx.astype(BF16)
            vn[0:n_x, :] = vx_ref[0].astype(BF16)
        kn[n_x:n_x + n_c, :] = norm64(kc_ref[0], gk_ref[...]).astype(BF16)
        vn[n_x:n_x + n_c, :] = vc_ref[0].astype(BF16)

    q = norm64(q_ref[0], gq_ref[...])
    if rope:
        q = rot(q, cq_ref[...], sq_ref[...])
    q = q * (DF_DIM ** -0.5 * LOG2_E)
    nt = (((1,), (1,)), ((), ()))

    n_k = n_x + n_c
    qs = (jnp.where(lo, q, 0.0).astype(BF16), jnp.where(lo, 0.0, q).astype(BF16))
    tq = q.shape[0]
    ms = [jnp.full((tq, 1), -jnp.inf, F32)] * 2
    ls = [jnp.zeros((tq, 1), F32)] * 2
    accs = [jnp.zeros((tq, 2 * DF_DIM), F32)] * 2
    for c0 in range(0, n_k, DF_KCHUNK):
        c1 = min(c0 + DF_KCHUNK, n_k)
        for mi in range(2):
            s = lax.dot_general(qs[mi], kn[c0:c1, :], nt, preferred_element_type=F32)
            m_new = jnp.maximum(ms[mi], jnp.max(s, axis=-1, keepdims=True))
            alpha = jnp.exp2(ms[mi] - m_new)
            p = jnp.exp2(s - m_new)
            ls[mi] = alpha * ls[mi] + jnp.sum(p, axis=-1, keepdims=True)
            accs[mi] = alpha * accs[mi] + jnp.dot(p.astype(BF16), vn[c0:c1, :], preferred_element_type=F32)
            ms[mi] = m_new
    o = accs[0] * (1.0 / ls[0]) - accs[1] * (lam_ref[0] / ls[1])
    o_ref[0] = (_rms(o, gs_ref[...]) * out_scale).astype(BF16)


def _df_attention(lam, zq, zx, zc, cos, sin, gq, gk, gs, out_scale):
    b, t, _ = zq.shape
    c = zc.shape[1]
    n_x = 0 if zx is None else zx.shape[1]
    w = 2 * DF_DIM
    tq = min(DF_QBLK, t)
    gq2 = jnp.concatenate([gq, gq]).reshape(1, w)
    gk2 = jnp.concatenate([gk, gk]).reshape(1, w)
    col = lambda base: (lambda bi, h, i: (bi, 0, base // w + h))
    vec = pl.BlockSpec((1, w), lambda bi, h, i: (0, 0))
    in_specs = [pl.BlockSpec(memory_space=pltpu.SMEM),
                pl.BlockSpec((1, tq, w), lambda bi, h, i: (bi, i, COL_DF_Q // w + h))]
    args = [lam.reshape(1), zq]
    if n_x:
        in_specs += [pl.BlockSpec((1, n_x, w), col(COL_DF_K)), pl.BlockSpec((1, n_x, w), col(COL_DF_V))]
        args += [zx, zx]
    in_specs += [pl.BlockSpec((1, c, w), col(COL_DF_K)), pl.BlockSpec((1, c, w), col(COL_DF_V))]
    args += [zc, zc]
    if n_x:
        in_specs += [pl.BlockSpec((tq, w), lambda bi, h, i: (i, 0)), pl.BlockSpec((tq, w), lambda bi, h, i: (i, 0)),
                     pl.BlockSpec((n_x, w), lambda bi, h, i: (0, 0)), pl.BlockSpec((n_x, w), lambda bi, h, i: (0, 0))]
        args += [cos, sin, cos, sin]
    in_specs += [vec, vec, vec]
    args += [gq2, gk2, gs.reshape(1, w)]
    return pl.pallas_call(
        functools.partial(_df_kernel, n_x=n_x, n_c=c, rope=bool(n_x), out_scale=out_scale),
        grid=(b, N_HEADS, t // tq),
        in_specs=in_specs,
        out_specs=pl.BlockSpec((1, tq, w), lambda bi, h, i: (bi, i, h)),
        out_shape=jax.ShapeDtypeStruct((b, t, N_HEADS * w), BF16),
        scratch_shapes=[pltpu.VMEM((n_x + c, w), BF16), pltpu.VMEM((n_x + c, w), BF16)],
        compiler_params=_cparams("parallel", "parallel", "arbitrary"),
        name="diff_attention" if n_x else "diff_ctx_attention",
    )(*args)


def _rg_kernel(*refs, n_x, n_c, ctx_out):
    if ctx_out:
        (ux_ref, uc_ref, gx_ref, gc_ref, cw_ref, cb_ref, wa_ref, ba_ref, wx_ref, bx_ref, lam_ref,
         ox_ref, oc_ref, a_s, b_s, p_s) = refs
    else:
        (ux_ref, uc_ref, gx_ref, cw_ref, cb_ref, wa_ref, ba_ref, wx_ref, bx_ref, lam_ref,
         ox_ref, a_s, b_s, p_s) = refs
    n_t = n_x + n_c
    clen = n_t // SCAN_CHUNKS
    cstride = clen + SCAN_PAD_ROWS

    def conv(z):
        z = z.astype(F32)
        n = z.shape[0]
        t = lax.broadcasted_iota(jnp.int32, (n, 1), 0)
        zm2 = jnp.where(t >= 2, pltpu.roll(z, 2, 0), 0.0)
        zm1 = jnp.where(t >= 1, pltpu.roll(z, 1, 0), 0.0)
        zp1 = jnp.where(t < n - 1, pltpu.roll(z, n - 1, 0), 0.0)
        return (zm2 * cw_ref[0:1, :] + zm1 * cw_ref[1:2, :] + z * cw_ref[2:3, :] + zp1 * cw_ref[3:4, :]
                + cb_ref[...])

    def pieces(t0, n):
        out, t = [], t0
        while t < t0 + n:
            ch = t // clen
            stop = min((ch + 1) * clen, t0 + n)
            out.append((t - t0, ch * cstride + (t - ch * clen), stop - t))
            t = stop
        return out

    def put(ref, d, t0, val):
        for off, row, ln in pieces(t0, val.shape[0]):
            ref[d, row:row + ln, :] = val[off:off + ln]

    def get(ref, d, t0, n):
        return jnp.concatenate([ref[d, row:row + ln, :] for _, row, ln in pieces(t0, n)], axis=0)

    def coeffs(u, d, t0):
        ub = u.astype(BF16)
        r = jax.nn.sigmoid(jnp.dot(ub, wa_ref[d, 0], preferred_element_type=F32) + ba_ref[d:d + 1, :])
        gi = jax.nn.sigmoid(jnp.dot(ub, wx_ref[d, 0], preferred_element_type=F32) + bx_ref[d:d + 1, :])
        log_a = (-RG_C) * r * jax.nn.softplus(-lam_ref[d:d + 1, :])
        a = jnp.exp(log_a)
        put(a_s, d, t0, a)
        put(b_s, d, t0, jnp.sqrt(-jnp.tanh(log_a) * (a * a + 1.0)) * (gi * u))

    u_c = conv(uc_ref[0])
    coeffs(u_c, 0, 0)
    coeffs(u_c, 1, n_x)
    u_x = conv(ux_ref[0])
    coeffs(u_x, 0, n_c)
    coeffs(u_x, 1, 0)

    def step(tau, carry):
        h_f, p_f, h_r, p_r = carry
        i_f = pl.ds(tau, SCAN_CHUNKS, stride=cstride)
        i_r = pl.ds(clen - 1 - tau, SCAN_CHUNKS, stride=cstride)
        a_f = a_s[0, i_f, :]
        a_r = a_s[1, i_r, :]
        h_f = a_f * h_f + b_s[0, i_f, :]
        h_r = a_r * h_r + b_s[1, i_r, :]
        p_f = a_f * p_f
        p_r = a_r * p_r
        b_s[0, i_f, :] = h_f
        b_s[1, i_r, :] = h_r
        p_s[0, i_f, :] = p_f
        p_s[1, i_r, :] = p_r
        return h_f, p_f, h_r, p_r

    zeros = jnp.zeros((SCAN_CHUNKS, RG_BW), F32)
    ones = jnp.ones((SCAN_CHUNKS, RG_BW), F32)
    h_f, p_f, h_r, p_r = lax.fori_loop(0, clen, step, (zeros, ones, zeros, ones), unroll=2)

    carry = jnp.zeros((1, RG_BW), F32)
    for ch in range(1, SCAN_CHUNKS):
        carry = p_f[ch - 1:ch, :] * carry + h_f[ch - 1:ch, :]
        rows = slice(ch * cstride, ch * cstride + clen)
        b_s[0, rows, :] = b_s[0, rows, :] + p_s[0, rows, :] * carry
    carry = jnp.zeros((1, RG_BW), F32)
    for ch in range(SCAN_CHUNKS - 2, -1, -1):
        carry = p_r[ch + 1:ch + 2, :] * carry + h_r[ch + 1:ch + 2, :]
        rows = slice(ch * cstride, ch * cstride + clen)
        b_s[1, rows, :] = b_s[1, rows, :] + p_s[1, rows, :] * carry

    gx = jax.nn.gelu(gx_ref[0].astype(F32), approximate=True)
    ox_ref[0] = ((get(b_s, 0, n_c, n_x) + get(b_s, 1, 0, n_x)) * gx).astype(BF16)
    if ctx_out:
        gc = jax.nn.gelu(gc_ref[0].astype(F32), approximate=True)
        oc_ref[0] = ((get(b_s, 0, 0, n_c) + get(b_s, 1, n_x, n_c)) * gc).astype(BF16)


def _rg_lru(zx, zc, conv_w, conv_b, w_a, b_a, w_x, b_x, lam, ctx_out):
    b, n_x, _ = zx.shape
    n_c = zc.shape[1]
    bw = RG_BW
    col = lambda base: (lambda bi, n: (bi, 0, base // bw + n))
    vec2 = pl.BlockSpec((2, bw), lambda bi, n: (0, n))
    wspec = pl.BlockSpec((2, 1, bw, bw), lambda bi, n: (0, n, 0, 0))
    in_specs = [pl.BlockSpec((1, n_x, bw), col(COL_RG_X)), pl.BlockSpec((1, n_c, bw), col(COL_RG_X)),
                pl.BlockSpec((1, n_x, bw), col(COL_RG_G))]
    args = [zx, zc, zx]
    if ctx_out:
        in_specs.append(pl.BlockSpec((1, n_c, bw), col(COL_RG_G)))
        args.append(zc)
    in_specs += [pl.BlockSpec((4, bw), lambda bi, n: (0, n)), pl.BlockSpec((1, bw), lambda bi, n: (0, n)),
                 wspec, vec2, wspec, vec2, vec2]
    args += [conv_w, conv_b.reshape(1, RG_WIDTH), w_a, b_a, w_x, b_x, lam]
    out_specs = [pl.BlockSpec((1, n_x, bw), lambda bi, n: (bi, 0, n))]
    out_shape = [jax.ShapeDtypeStruct((b, n_x, RG_WIDTH), BF16)]
    if ctx_out:
        out_specs.append(pl.BlockSpec((1, n_c, bw), lambda bi, n: (bi, 0, n)))
        out_shape.append(jax.ShapeDtypeStruct((b, n_c, RG_WIDTH), BF16))
    n_s = SCAN_CHUNKS * ((n_x + n_c) // SCAN_CHUNKS + SCAN_PAD_ROWS)
    outs = pl.pallas_call(
        functools.partial(_rg_kernel, n_x=n_x, n_c=n_c, ctx_out=ctx_out),
        grid=(b, RG_BLOCKS),
        in_specs=in_specs,
        out_specs=out_specs,
        out_shape=out_shape,
        scratch_shapes=[pltpu.VMEM((2, n_s, bw), F32), pltpu.VMEM((2, n_s, bw), F32), pltpu.VMEM((2, n_s, bw), F32)],
        compiler_params=_cparams("parallel", "parallel"),
        name="rg_lru",
    )(*args)
    return outs if ctx_out else (outs[0], None)


def _merge_kernel(ya_ref, yb_ref, yc_ref, ga_ref, gb_ref, gc_ref, wa_ref, wb_ref, wc_ref, o_ref):
    gate = lambda ref: jax.nn.sigmoid(ref[0].astype(F32))
    m = gate(ga_ref) * jnp.dot(ya_ref[0], wa_ref[...], preferred_element_type=F32)
    m = m + gate(gb_ref) * jnp.dot(yb_ref[0], wb_ref[...], preferred_element_type=F32)
    m = m + gate(gc_ref) * jnp.dot(yc_ref[0], wc_ref[...], preferred_element_type=F32)
    o_ref[0] = m.astype(BF16)


def _branch_merge(ya, yb, yc, z, w_branch, tm, tn):
    b, t, _ = ya.shape
    d = D_MODEL
    wa, wb, wc = w_branch[:512], w_branch[512:1024], w_branch[1024:]
    gate = lambda k: (lambda bi, i, j: (bi, i, (COL_GATE + k * d) // tn + j))
    return pl.pallas_call(
        _merge_kernel,
        grid=(b, t // tm, d // tn),
        in_specs=[
            pl.BlockSpec((1, tm, 512), lambda bi, i, j: (bi, i, 0)),
            pl.BlockSpec((1, tm, 512), lambda bi, i, j: (bi, i, 0)),
            pl.BlockSpec((1, tm, 1024), lambda bi, i, j: (bi, i, 0)),
            pl.BlockSpec((1, tm, tn), gate(0)),
            pl.BlockSpec((1, tm, tn), gate(1)),
            pl.BlockSpec((1, tm, tn), gate(2)),
            pl.BlockSpec((512, tn), lambda bi, i, j: (0, j)),
            pl.BlockSpec((512, tn), lambda bi, i, j: (0, j)),
            pl.BlockSpec((1024, tn), lambda bi, i, j: (0, j)),
        ],
        out_specs=pl.BlockSpec((1, tm, tn), lambda bi, i, j: (bi, i, j)),
        out_shape=jax.ShapeDtypeStruct((b, t, d), BF16),
        compiler_params=_cparams("parallel", "parallel", "arbitrary"),
        name="branch_merge",
    )(ya, yb, yc, z, z, z, wa, wb, wc)


def _resid_kernel(m_ref, w_ref, x_ref, gt_ref, o_ref):
    o_ref[0] = x_ref[0] + gt_ref[0] * jnp.dot(m_ref[0], w_ref[...], preferred_element_type=F32)


def _out_proj_residual(m, w_out, x, gt, tm, tn):
    b, t, d = x.shape
    return pl.pallas_call(
        _resid_kernel,
        grid=(b, t // tm, d // tn),
        in_specs=[
            pl.BlockSpec((1, tm, d), lambda bi, i, j: (bi, i, 0)),
            pl.BlockSpec((d, tn), lambda bi, i, j: (0, j)),
            pl.BlockSpec((1, tm, tn), lambda bi, i, j: (bi, i, j)),
            pl.BlockSpec((1, 1, tn), lambda bi, i, j: (bi, 0, j)),
        ],
        out_specs=pl.BlockSpec((1, tm, tn), lambda bi, i, j: (bi, i, j)),
        out_shape=jax.ShapeDtypeStruct((b, t, d), F32),
        compiler_params=_cparams("parallel", "parallel", "arbitrary"),
        name="out_proj_residual",
    )(m, w_out, x, gt)


def _router_kernel(*refs, starts):
    n_s = len(starts) - 1
    g_ref, wh_ref, wl_ref, br_ref, h_ref, eid_ref, wt_ref = refs[3 * n_s:]
    i = pl.program_id(0)
    for k in range(n_s):
        x_ref, sc_ref, sh_ref = refs[3 * k:3 * k + 3]

        @pl.when((i >= starts[k]) & (i < starts[k + 1]))
        def _(x_ref=x_ref, sc_ref=sc_ref, sh_ref=sh_ref):
            _route_rows(x_ref, sc_ref, sh_ref, g_ref, wh_ref, wl_ref, br_ref, h_ref, eid_ref, wt_ref)


def _route_rows(x_ref, sc_ref, sh_ref, g_ref, wh_ref, wl_ref, br_ref, h_ref, eid_ref, wt_ref):
    h = _rms(x_ref[0], g_ref[...]) * (1.0 + sc_ref[0]) + sh_ref[0]
    h_ref[...] = h
    hh = h.astype(BF16)
    hl = (h - hh.astype(F32)).astype(BF16)
    logits = (jnp.dot(hh, wh_ref[...], preferred_element_type=F32)
              + jnp.dot(hl, wh_ref[...], preferred_element_type=F32)
              + jnp.dot(hh, wl_ref[...], preferred_element_type=F32)) + br_ref[...]
    lane = lax.broadcasted_iota(jnp.int32, logits.shape, 1)
    lane_f = lane.astype(F32)

    def first_argmax(v, valid):
        vm = jnp.where(valid, v, -jnp.inf)
        mx = jnp.max(vm, axis=-1, keepdims=True)
        idx = jnp.min(jnp.where(valid & (vm == mx), lane_f, 1e9), axis=-1, keepdims=True)
        return mx, idx.astype(jnp.int32)

    is_g = lane < N_GROUPS
    gmax, grp = first_argmax(logits, is_g)
    p_grp = 1.0 / jnp.sum(jnp.where(is_g, jnp.exp(logits - gmax), 0.0), axis=-1, keepdims=True)
    e_lo = N_GROUPS + grp * EXPERTS_PER_GROUP
    in_grp = (lane >= e_lo) & (lane < e_lo + EXPERTS_PER_GROUP)
    v0, i0 = first_argmax(logits, in_grp)
    v1, i1 = first_argmax(logits, in_grp & (lane != i0))
    e1 = jnp.exp(v1 - v0)
    w0 = p_grp / (1.0 + e1)
    w1 = p_grp * e1 / (1.0 + e1)
    eid_ref[...] = jnp.where(lane == 0, i0 - N_GROUPS, jnp.where(lane == 1, i1 - N_GROUPS, 0))
    wt_ref[...] = jnp.where(lane == 0, w0, jnp.where(lane == 1, w1, 0.0))


def _router(xs, scs, shs, g, wr_hi, wr_lo, br, tm):
    d = xs[0].shape[-1]
    starts = [0]
    in_specs, args = [], []
    for x, sc, sh in zip(xs, scs, shs):
        b, t, _ = x.shape
        n_i = t // tm
        s0 = starts[-1]
        starts.append(s0 + b * n_i)

        def local(i, s0=s0, n=b * n_i):
            return jnp.clip(i - s0, 0, n - 1)

        in_specs += [pl.BlockSpec((1, tm, d), lambda i, f=local, n_i=n_i: (f(i) // n_i, f(i) % n_i, 0)),
                     pl.BlockSpec((1, 1, d), lambda i, f=local, n_i=n_i: (f(i) // n_i, 0, 0)),
                     pl.BlockSpec((1, 1, d), lambda i, f=local, n_i=n_i: (f(i) // n_i, 0, 0))]
        args += [x, sc, sh]
    n_all = starts[-1] * tm
    in_specs += [pl.BlockSpec((1, d), lambda i: (0, 0)), pl.BlockSpec((d, 128), lambda i: (0, 0)),
                 pl.BlockSpec((d, 128), lambda i: (0, 0)), pl.BlockSpec((1, 128), lambda i: (0, 0))]
    args += [g.reshape(1, d), wr_hi, wr_lo, br]
    flat = lambda width: pl.BlockSpec((tm, width), lambda i: (i, 0))
    return pl.pallas_call(
        functools.partial(_router_kernel, starts=tuple(starts)),
        grid=(starts[-1],),
        in_specs=in_specs,
        out_specs=[flat(d), flat(128), flat(128)],
        out_shape=[jax.ShapeDtypeStruct((n_all, d), F32), jax.ShapeDtypeStruct((n_all, 128), jnp.int32),
                   jax.ShapeDtypeStruct((n_all, 128), F32)],
        compiler_params=_cparams("parallel"),
        name="moe_router",
    )(*args)


def _row_copy(src_hbm, dst_vmem, sem, src_row, dst_row):
    return pltpu.make_async_copy(src_hbm.at[pl.ds(src_row, 1)], dst_vmem.at[pl.ds(dst_row, 1)], sem)


def _dispatch_kernel(tok_ref, src_hbm, dst_hbm, sem):
    rows = tok_ref.shape[2]
    i = pl.program_id(0)
    base = i * rows
    for r in range(rows):
        pltpu.make_async_copy(src_hbm.at[pl.ds(tok_ref[0, 0, r], 1)], dst_hbm.at[pl.ds(base + r, 1)],
                              sem.at[i % 2]).start(priority=r % 2)

    def wait_step(k):
        def wait(r, c):
            pltpu.make_async_copy(src_hbm.at[pl.ds(0, 1)], dst_hbm.at[pl.ds(0, 1)], sem.at[k]).wait()
            return c
        lax.fori_loop(0, rows, wait, 0, unroll=8)

    @pl.when(i > 0)
    def _():
        wait_step((i + 1) % 2)

    @pl.when(i == pl.num_programs(0) - 1)
    def _():
        wait_step(i % 2)


def _dispatch(buf_tok, x_rows, rows):
    n_pad = buf_tok.shape[0]
    d = x_rows.shape[1]
    while n_pad % rows:
        rows //= 2
    return pl.pallas_call(
        _dispatch_kernel,
        grid=(n_pad // rows,),
        in_specs=[pl.BlockSpec((1, 1, rows), lambda i: (i, 0, 0), memory_space=pltpu.SMEM),
                  pl.BlockSpec(memory_space=pl.ANY)],
        out_specs=pl.BlockSpec(memory_space=pl.ANY),
        out_shape=jax.ShapeDtypeStruct((n_pad, d), x_rows.dtype),
        scratch_shapes=[pltpu.SemaphoreType.DMA((2,))],
        compiler_params=_cparams("arbitrary"),
        name="moe_dispatch",
    )(buf_tok.reshape(n_pad // rows, 1, rows), x_rows)


def _expert_kernel(blk_e_ref, x_ref, w1_ref, w3_ref, w2_ref, o_ref):
    del blk_e_ref
    xb = x_ref[...].astype(BF16)
    h1 = jnp.dot(xb, w1_ref[0], preferred_element_type=F32)
    h3 = jnp.dot(xb, w3_ref[0], preferred_element_type=F32)
    hid = (h1 * jax.nn.sigmoid(h1) * h3).astype(BF16)
    o_ref[...] = jnp.dot(hid, w2_ref[0], preferred_element_type=F32)


def _expert_blocks(blk_e, x_sorted, w1, w3, w2):
    n_blk = blk_e.shape[0]
    tm = MOE_TM
    d = x_sorted.shape[1]
    grid_spec = pltpu.PrefetchScalarGridSpec(
        num_scalar_prefetch=1,
        grid=(n_blk,),
        in_specs=[
            pl.BlockSpec((tm, d), lambda i, e: (i, 0)),
            pl.BlockSpec((1, d, D_EXPERT), lambda i, e: (e[i], 0, 0)),
            pl.BlockSpec((1, d, D_EXPERT), lambda i, e: (e[i], 0, 0)),
            pl.BlockSpec((1, D_EXPERT, d), lambda i, e: (e[i], 0, 0)),
        ],
        out_specs=pl.BlockSpec((tm, d), lambda i, e: (i, 0)),
    )
    return pl.pallas_call(
        _expert_kernel,
        grid_spec=grid_spec,
        out_shape=jax.ShapeDtypeStruct((n_blk * tm, d), F32),
        compiler_params=_cparams("arbitrary"),
        name="moe_experts",
    )(blk_e, x_sorted, w1, w3, w2)


def _combine_kernel(slot_ref, slot_next_ref, y_hbm, x_ref, gt_ref, wt_ref, o_ref, a0, b0, a1, b1, sem):
    tm = a0.shape[0]
    i = pl.program_id(0)
    n = pl.num_programs(0)
    bufs = ((a0, b0), (a1, b1))

    def wait_rows(p):
        def wait(r, c):
            _row_copy(y_hbm, bufs[p][0], sem.at[p], 0, r).wait()
            _row_copy(y_hbm, bufs[p][1], sem.at[p], 0, r).wait()
            return c
        lax.fori_loop(0, tm, wait, 0, unroll=8)

    @pl.when(i == 0)
    def _():
        def issue(r, c):
            _row_copy(y_hbm, a0, sem.at[0], slot_ref[0, 0, 2 * r], r).start()
            _row_copy(y_hbm, b0, sem.at[0], slot_ref[0, 0, 2 * r + 1], r).start()
            return c
        lax.fori_loop(0, tm, issue, 0, unroll=8)

    def block(cur):
        wait_rows(cur)
        nxt_a, nxt_b = bufs[1 - cur]
        for r in range(tm):
            _row_copy(y_hbm, nxt_a, sem.at[1 - cur], slot_next_ref[0, 0, 2 * r], r).start(priority=0)
            _row_copy(y_hbm, nxt_b, sem.at[1 - cur], slot_next_ref[0, 0, 2 * r + 1], r).start(priority=1)
        wt = wt_ref[...]
        f = bufs[cur][0][...] * wt[:, 0:1] + bufs[cur][1][...] * wt[:, 1:2]
        o_ref[...] = x_ref[...] + gt_ref[0] * f

    for parity in range(2):
        @pl.when(i % 2 == parity)
        def _(parity=parity):
            block(parity)

        @pl.when((i == n - 1) & (i % 2 == parity))
        def _(parity=parity):
            wait_rows(1 - parity)


def _combine(slots, y, x, gt, wt, tok0, tm):
    b, t, d = x.shape
    n_i = t // tm
    n = b * n_i
    x2 = x.reshape(b * t, d)
    blk0 = tok0 // tm
    slots3 = slots.reshape(-1, 1, 2 * tm)
    out = pl.pallas_call(
        _combine_kernel,
        grid=(n,),
        in_specs=[
            pl.BlockSpec((1, 1, 2 * tm), lambda i: (blk0 + i, 0, 0), memory_space=pltpu.SMEM),
            pl.BlockSpec((1, 1, 2 * tm), lambda i: (blk0 + jnp.minimum(i + 1, n - 1), 0, 0),
                         memory_space=pltpu.SMEM),
            pl.BlockSpec(memory_space=pl.ANY),
            pl.BlockSpec((tm, d), lambda i: (i, 0)),
            pl.BlockSpec((1, 1, d), lambda i: (i // n_i, 0, 0)),
            pl.BlockSpec((tm, 128), lambda i: (blk0 + i, 0)),
        ],
        out_specs=pl.BlockSpec((tm, d), lambda i: (i, 0)),
        out_shape=jax.ShapeDtypeStruct((b * t, d), F32),
        scratch_shapes=[pltpu.VMEM((tm, d), F32)] * 4 + [pltpu.SemaphoreType.DMA((2,))],
        compiler_params=_cparams("arbitrary"),
        name="moe_combine",
    )(slots3, slots3, y, x2, gt, wt)
    return out.reshape(b, t, d)


def _routing_tables(eid):
    m = eid.shape[0]
    tm = MOE_TM
    i32 = jnp.int32
    iota = jnp.arange(m, dtype=i32)
    se, order = lax.sort_key_val(eid, iota)
    onehot = (se[:, None] == jnp.arange(N_EXPERTS, dtype=i32)[None, :]).astype(i32)
    counts = jnp.sum(onehot, axis=0)
    starts = jnp.cumsum(counts) - counts
    pcounts = (counts + tm - 1) // tm * tm
    pends = jnp.cumsum(pcounts)
    pstarts = pends - pcounts
    dest = iota + jnp.sum(onehot * (pstarts - starts)[None, :], axis=1)
    _, slot = lax.sort_key_val(order, dest)
    n_blk = (m + N_EXPERTS * (tm - 1) + tm - 1) // tm
    blk_start = jnp.arange(n_blk, dtype=i32) * tm
    blk_e = jnp.minimum(jnp.sum((pends[None, :] <= blk_start[:, None]).astype(i32), axis=1), N_EXPERTS - 1)
    off = (blk_start - pstarts[blk_e])[:, None] + jnp.arange(tm, dtype=i32)[None, :]
    valid = off < counts[blk_e][:, None]
    src = jnp.clip(starts[blk_e][:, None] + off, 0, m - 1)
    buf_tok = jnp.where(valid, order[src] // TOP_K, 0).reshape(-1)
    return blk_e.astype(i32), buf_tok, slot


def _moe(xs, scs, shs, gts, norm_g, wr_hi, wr_lo, br, w1, w3, w2):
    h_all, eid_all, wt_all = _router(xs, scs, shs, norm_g, wr_hi, wr_lo, br, 512)
    blk_e, buf_tok, slot = _routing_tables(eid_all[:, :TOP_K].reshape(-1))
    y = _expert_blocks(blk_e, _dispatch(buf_tok, h_all, MOE_DISPATCH_ROWS), w1, w3, w2)
    outs = []
    tok0 = 0
    for x, gt in zip(xs, gts):
        outs.append(_combine(slot, y, x, gt, wt_all, tok0, tm=256))
        tok0 += x.shape[0] * x.shape[1]
    return outs


def kernel(x, c, ctx, c_ctx, w_mod, b_mod, norm1_g, norm2_g, w_in, na_q_g, na_k_g, na_rpb, df_q_g, df_k_g, df_lam, df_sub_g, rg_conv_w, rg_conv_b, rg_w_a, rg_b_a, rg_w_x, rg_b_x, rg_lam, w_branch, w_out, w_router_g, b_router_g, w_router_e, b_router_e, w1, w3, w2):
    B, S, D = x.shape
    C = ctx.shape[1]
    n_rows = S // GRID_W
    rope_cos, rope_sin = _rope_tables(S)
    na_lo, na_hi = _na_bias_tables(na_rpb)

    pad = (-(B + 1)) % 8
    rows = jnp.concatenate([c, c_ctx[None, :], jnp.zeros((pad, D), F32)], axis=0)
    mod_all = _modulation(rows, w_mod, b_mod)

    xc = ctx.reshape(1, B * C, D)
    for l in range(DEPTH):
        need_ctx = l < DEPTH - 1
        lam_init = 0.8 - 0.6 * float(np.exp(-0.3 * l))
        mod = mod_all[l]
        sh1, sc1, gt1, sh2, sc2, gt2 = [mod[:B, k * D:(k + 1) * D][:, None, :] for k in range(6)]
        csh1, csc1, cgt1, csh2, csc2, cgt2 = [mod[B:B + 1, k * D:(k + 1) * D][:, None, :] for k in range(6)]

        w_in_l = w_in[l].astype(BF16)
        zx = _norm_mod_matmul(x, norm1_g[l], sc1, sh1, w_in_l, tm=1024, tn=1024)
        n_cc = IN_COLS if need_ctx else KV_COLS
        zc = _norm_mod_matmul(xc, norm1_g[l], csc1, csh1, w_in_l[:, :n_cc], tm=1024, tn=1024).reshape(B, C, n_cc)

        y_a = _na_attention(zx, zc, na_lo[l], na_hi[l], na_q_g[l], na_k_g[l])

        lp = df_lam[l]
        lam = jnp.exp(jnp.sum(lp[0] * lp[1])) - jnp.exp(jnp.sum(lp[2] * lp[3])) + lam_init
        y_b = _df_attention(lam, zx, zx, zc, rope_cos, rope_sin, df_q_g[l], df_k_g[l], df_sub_g[l], 1.0 - lam_init)

        y_c, y_cc = _rg_lru(zx, zc, rg_conv_w[l], rg_conv_b[l], rg_w_a[l].astype(BF16), rg_b_a[l],
                            rg_w_x[l].astype(BF16), rg_b_x[l], rg_lam[l], need_ctx)

        wb_l = w_branch[l].astype(BF16)
        wo_l = w_out[l].astype(BF16)
        m_x = _branch_merge(y_a, y_b, y_c, zx, wb_l, tm=1024, tn=1024)
        x = _out_proj_residual(m_x, wo_l, x, gt1, tm=1024, tn=1024)
        if need_ctx:
            y_ac = _na_ctx_attention(zc, na_q_g[l], na_k_g[l])
            y_bc = _df_attention(lam, zc, None, zc, None, None, df_q_g[l], df_k_g[l], df_sub_g[l], 1.0 - lam_init)
            m_c = _branch_merge(y_ac, y_bc, y_cc, zc, wb_l, tm=C, tn=1024)
            xc = _out_proj_residual(m_c.reshape(1, B * C, D), wo_l, xc, cgt1, tm=1024, tn=1024)

        wr = jnp.concatenate([w_router_g[l], w_router_e[l],
                              jnp.zeros((D, 128 - N_GROUPS - N_EXPERTS), F32)], axis=1)
        wr_hi = wr.astype(BF16)
        wr_lo = (wr - wr_hi.astype(F32)).astype(BF16)
        br = jnp.concatenate([b_router_g[l], b_router_e[l],
                              jnp.zeros((128 - N_GROUPS - N_EXPERTS,), F32)]).reshape(1, 128)
        w1_l, w3_l, w2_l = w1[l].astype(BF16), w3[l].astype(BF16), w2[l].astype(BF16)
        if need_ctx:
            xc, x = _moe([xc, x], [csc2, sc2], [csh2, sh2], [cgt2, gt2], norm2_g[l],
                         wr_hi, wr_lo, br, w1_l, w3_l, w2_l)
        else:
            (x,) = _moe([x], [sc2], [sh2], [gt2], norm2_g[l], wr_hi, wr_lo, br, w1_l, w3_l, w2_l)
    return x
```

```python
import functools

import jax
import jax.numpy as jnp
import numpy as np
from jax import lax
from jax.experimental import pallas as pl
from jax.experimental.pallas import tpu as pltpu

F32 = jnp.float32
BF16 = jnp.bfloat16

D_MODEL = 2048
DEPTH = 4
GRID_W = 64
HEAD_DIM = 128
N_HEADS = 4
NA_WIN_R = 8
NA_WIN_C = 16
NA_QCB = 16
NA_KCB = 32
DF_DIM = 64
RG_WIDTH = 1024
RG_BLOCKS = 8
RG_BW = 128
RG_C = 8.0
N_GROUPS = 4
EXPERTS_PER_GROUP = 8
N_EXPERTS = 32
TOP_K = 2
D_EXPERT = 512
ROPE_BASE = 10000.0
EPS = 1e-6
NEG = -1e30
LOG2_E = 1.4426950408889634

COL_NA_K, COL_NA_V, COL_DF_K, COL_DF_V, COL_RG_X = 0, 512, 1024, 1536, 2048
COL_NA_Q, COL_DF_Q, COL_RG_G, COL_GATE = 3072, 3584, 4096, 5120
KV_COLS = 3072
MIX_COLS = 5120
IN_COLS = MIX_COLS + 3 * D_MODEL

VMEM_LIMIT_BYTES = 56 * 1024 * 1024

NA_ROWS_PER_STEP = 4
NA_KEY_ROWS = NA_ROWS_PER_STEP + NA_WIN_R
DF_QBLK = 1024
DF_KCHUNK = 768
MOE_TM = 256
MOE_DISPATCH_ROWS = 512
SCAN_CHUNKS = 8
SCAN_PAD_ROWS = 8


def _cparams(*sem):
    return pltpu.CompilerParams(dimension_semantics=sem, vmem_limit_bytes=VMEM_LIMIT_BYTES)


def _rms(x, g):
    x = x.astype(F32)
    return x * lax.rsqrt(jnp.mean(x * x, axis=-1, keepdims=True) + EPS) * g


def _mod_kernel(s_ref, w_ref, b_ref, o_ref):
    s = s_ref[...]
    a = (s * jax.nn.sigmoid(s)).astype(BF16)
    o_ref[0] = jnp.dot(a, w_ref[0].astype(BF16), preferred_element_type=F32) + b_ref[0]


def _modulation(rows, w_mod, b_mod, tn=1024):
    n_l, d, n = w_mod.shape
    r = rows.shape[0]
    return pl.pallas_call(
        _mod_kernel,
        grid=(n_l, n // tn),
        in_specs=[
            pl.BlockSpec((r, d), lambda l, j: (0, 0)),
            pl.BlockSpec((1, d, tn), lambda l, j: (l, 0, j)),
            pl.BlockSpec((1, 1, tn), lambda l, j: (l, 0, j)),
        ],
        out_specs=pl.BlockSpec((1, r, tn), lambda l, j: (l, 0, j)),
        out_shape=jax.ShapeDtypeStruct((n_l, r, n), F32),
        compiler_params=_cparams("parallel", "parallel"),
        name="modulation",
    )(rows, w_mod, b_mod.reshape(n_l, 1, n))


def _nmm_kernel(x_ref, g_ref, sc_ref, sh_ref, w_ref, o_ref, h_ref):
    @pl.when(pl.program_id(2) == 0)
    def _():
        h = _rms(x_ref[0], g_ref[...]) * (1.0 + sc_ref[0]) + sh_ref[0]
        h_ref[...] = h.astype(BF16)

    o_ref[0] = jnp.dot(h_ref[...], w_ref[...], preferred_element_type=F32).astype(o_ref.dtype)


def _norm_mod_matmul(x, g, sc, sh, w, tm, tn):
    b, t, d = x.shape
    n = w.shape[1]
    return pl.pallas_call(
        _nmm_kernel,
        grid=(b, t // tm, n // tn),
        in_specs=[
            pl.BlockSpec((1, tm, d), lambda bi, i, j: (bi, i, 0)),
            pl.BlockSpec((1, d), lambda bi, i, j: (0, 0)),
            pl.BlockSpec((1, 1, d), lambda bi, i, j: (bi, 0, 0)),
            pl.BlockSpec((1, 1, d), lambda bi, i, j: (bi, 0, 0)),
            pl.BlockSpec((d, tn), lambda bi, i, j: (0, j)),
        ],
        out_specs=pl.BlockSpec((1, tm, tn), lambda bi, i, j: (bi, i, j)),
        out_shape=jax.ShapeDtypeStruct((b, t, n), BF16),
        scratch_shapes=[pltpu.VMEM((tm, d), BF16)],
        compiler_params=_cparams("parallel", "parallel", "arbitrary"),
        name="norm_mod_proj",
    )(x, g.reshape(1, d), sc, sh, w)


def _na_window_start(r0):
    lower = jnp.clip(r0 - NA_WIN_R // 2, 0, (2048 // GRID_W) - NA_WIN_R)
    return jnp.minimum(lower, (2048 // GRID_W) - NA_KEY_ROWS)


def _na_bias_tables(rpb):
    w = GRID_W
    qc = np.arange(w)
    kc = np.arange(w)
    qwin = np.clip(qc - NA_WIN_C // 2, 0, w - NA_WIN_C)
    kcol_start = np.clip((qc // NA_QCB) * NA_QCB - (NA_KCB - NA_QCB) // 2, 0, w - NA_KCB)
    col_ok = (kc[None, :] >= qwin[:, None]) & (kc[None, :] < qwin[:, None] + NA_WIN_C)
    col_ok &= (kc[None, :] >= kcol_start[:, None]) & (kc[None, :] < kcol_start[:, None] + NA_KCB)
    edge = w - NA_WIN_C
    lead = rpb.shape[:-1]
    ext = jnp.concatenate([jnp.repeat(rpb[..., :1], edge, axis=-1), rpb, jnp.repeat(rpb[..., -1:], edge, axis=-1),
                           jnp.zeros(lead + (1,), F32)], axis=-1)
    skew = jnp.tile(ext, (1,) * len(lead) + (w,))[..., :w * (2 * w - 1)].reshape(lead + (w, 2 * w - 1))
    toep = jnp.where(col_ok, skew[..., w - 1:], NEG)
    toep = jnp.concatenate([toep, jnp.full(toep.shape[:-3] + (1, w, w), NEG, F32)], axis=-3)
    zeros = jnp.zeros_like(toep)
    return jnp.concatenate([toep, zeros], axis=-1), jnp.concatenate([zeros, toep], axis=-1)


def _na_bias_blocks(n_rows):
    rb, kw = NA_ROWS_PER_STEP, NA_KEY_ROWS
    rows = np.arange(n_rows)
    row_start = np.clip(rows - NA_WIN_R // 2, 0, n_rows - NA_WIN_R)
    n_steps = n_rows // rb
    blk = np.full((n_steps, rb, kw), 2 * NA_WIN_R - 1, np.int32)
    for s in range(n_steps):
        ws = min(int(np.clip(s * rb - NA_WIN_R // 2, 0, n_rows - NA_WIN_R)), n_rows - kw)
        for j in range(rb):
            r = s * rb + j
            for i in range(kw):
                if row_start[r] <= ws + i < row_start[r] + NA_WIN_R:
                    blk[s, j, i] = ws + i - r + (NA_WIN_R - 1)
    return blk.reshape(-1)


def _na_kernel(blk_ref, q_ref, k_ref, v_ref, kc_ref, vc_ref, lo_ref, hi_ref, gq_ref, gk_ref, o_ref,
               kn, vn, knc, vnc):
    step = pl.program_id(1)

    def bias(h):
        rows = []
        for j in range(NA_ROWS_PER_STEP):
            base = (step * NA_ROWS_PER_STEP + j) * NA_KEY_ROWS
            rows.append(jnp.concatenate(
                [lo_ref[h, blk_ref[base + i]] + hi_ref[h, blk_ref[base + i + 1]] for i in range(0, NA_KEY_ROWS, 2)],
                axis=-1))
        return jnp.concatenate(rows, axis=0)

    @pl.when(step == 0)
    def _():
        for h in range(N_HEADS):
            sl = slice(h * HEAD_DIM, (h + 1) * HEAD_DIM)
            kn[:, sl] = _rms(k_ref[0, :, sl], gk_ref[...]).astype(BF16)
            knc[:, sl] = _rms(kc_ref[0, :, sl], gk_ref[...]).astype(BF16)
        vn[...] = v_ref[0].astype(BF16)
        vnc[...] = vc_ref[0].astype(BF16)

    ws = _na_window_start(step * NA_ROWS_PER_STEP)
    start = pl.multiple_of(ws * GRID_W, GRID_W)
    nk = NA_KEY_ROWS * GRID_W
    scale = HEAD_DIM ** -0.5
    nt = (((1,), (1,)), ((), ()))
    for h in range(N_HEADS):
        sl = slice(h * HEAD_DIM, (h + 1) * HEAD_DIM)
        qn = (_rms(q_ref[0, :, sl], gq_ref[...]) * scale).astype(BF16)
        s_loc = lax.dot_general(qn, kn[pl.ds(start, nk), sl], nt, preferred_element_type=F32) + bias(h)
        s_ctx = lax.dot_general(qn, knc[:, sl], nt, preferred_element_type=F32)
        m = jnp.maximum(jnp.max(s_loc, axis=-1, keepdims=True), jnp.max(s_ctx, axis=-1, keepdims=True))
        p_loc = jnp.exp(s_loc - m)
        p_ctx = jnp.exp(s_ctx - m)
        l = jnp.sum(p_loc, axis=-1, keepdims=True) + jnp.sum(p_ctx, axis=-1, keepdims=True)
        o = jnp.dot(p_loc.astype(BF16), vn[pl.ds(start, nk), sl], preferred_element_type=F32)
        o = o + jnp.dot(p_ctx.astype(BF16), vnc[:, sl], preferred_element_type=F32)
        o_ref[0, :, sl] = (o * (1.0 / l)).astype(BF16)


def _na_attention(zx, zc, bias_lo, bias_hi, gq, gk):
    b, s, _ = zx.shape
    c = zc.shape[1]
    w = N_HEADS * HEAD_DIM
    tq = NA_ROWS_PER_STEP * GRID_W
    blk = jnp.asarray(_na_bias_blocks(s // GRID_W))
    table = pl.BlockSpec(bias_lo.shape, lambda bi, i, blk: (0, 0, 0, 0))
    grid_spec = pltpu.PrefetchScalarGridSpec(
        num_scalar_prefetch=1,
        grid=(b, s // tq),
        in_specs=[
            pl.BlockSpec((1, tq, w), lambda bi, i, blk: (bi, i, COL_NA_Q // w)),
            pl.BlockSpec((1, s, w), lambda bi, i, blk: (bi, 0, COL_NA_K // w)),
            pl.BlockSpec((1, s, w), lambda bi, i, blk: (bi, 0, COL_NA_V // w)),
            pl.BlockSpec((1, c, w), lambda bi, i, blk: (bi, 0, COL_NA_K // w)),
            pl.BlockSpec((1, c, w), lambda bi, i, blk: (bi, 0, COL_NA_V // w)),
            table, table,
            pl.BlockSpec((1, HEAD_DIM), lambda bi, i, blk: (0, 0)),
            pl.BlockSpec((1, HEAD_DIM), lambda bi, i, blk: (0, 0)),
        ],
        out_specs=pl.BlockSpec((1, tq, w), lambda bi, i, blk: (bi, i, 0)),
        scratch_shapes=[pltpu.VMEM((s, w), BF16), pltpu.VMEM((s, w), BF16),
                        pltpu.VMEM((c, w), BF16), pltpu.VMEM((c, w), BF16)],
    )
    return pl.pallas_call(
        _na_kernel,
        grid_spec=grid_spec,
        out_shape=jax.ShapeDtypeStruct((b, s, w), BF16),
        compiler_params=_cparams("parallel", "arbitrary"),
        name="na_attention",
    )(blk, zx, zx, zx, zc, zc, bias_lo, bias_hi, gq.reshape(1, HEAD_DIM), gk.reshape(1, HEAD_DIM))


def _na_ctx_kernel(q_ref, k_ref, v_ref, gq_ref, gk_ref, o_ref):
    scale = HEAD_DIM ** -0.5
    nt = (((1,), (1,)), ((), ()))
    for h in range(N_HEADS):
        sl = slice(h * HEAD_DIM, (h + 1) * HEAD_DIM)
        qn = (_rms(q_ref[0, :, sl], gq_ref[...]) * scale).astype(BF16)
        kn = _rms(k_ref[0, :, sl], gk_ref[...]).astype(BF16)
        s = lax.dot_general(qn, kn, nt, preferred_element_type=F32)
        p = jnp.exp(s - jnp.max(s, axis=-1, keepdims=True))
        l = jnp.sum(p, axis=-1, keepdims=True)
        o = jnp.dot(p.astype(BF16), v_ref[0, :, sl].astype(BF16), preferred_element_type=F32)
        o_ref[0, :, sl] = (o * (1.0 / l)).astype(BF16)


def _na_ctx_attention(zc, gq, gk):
    b, c, _ = zc.shape
    w = N_HEADS * HEAD_DIM
    return pl.pallas_call(
        _na_ctx_kernel,
        grid=(b,),
        in_specs=[
            pl.BlockSpec((1, c, w), lambda bi: (bi, 0, COL_NA_Q // w)),
            pl.BlockSpec((1, c, w), lambda bi: (bi, 0, COL_NA_K // w)),
            pl.BlockSpec((1, c, w), lambda bi: (bi, 0, COL_NA_V // w)),
            pl.BlockSpec((1, HEAD_DIM), lambda bi: (0, 0)),
            pl.BlockSpec((1, HEAD_DIM), lambda bi: (0, 0)),
        ],
        out_specs=pl.BlockSpec((1, c, w), lambda bi: (bi, 0, 0)),
        out_shape=jax.ShapeDtypeStruct((b, c, w), BF16),
        compiler_params=_cparams("parallel"),
        name="na_ctx_attention",
    )(zc, zc, zc, gq.reshape(1, HEAD_DIM), gk.reshape(1, HEAD_DIM))


def _rope_tables(n_tok):
    t = np.arange(n_tok)
    pos = np.stack([t // GRID_W, t % GRID_W], axis=0).astype(np.float32)
    n_freq = DF_DIM // 4
    inv = (np.float32(ROPE_BASE) ** (-np.arange(n_freq, dtype=np.float32) / n_freq)).astype(np.float32)
    ang = jnp.asarray(pos[:, :, None] * inv)
    cos, sin = jnp.cos(ang), jnp.sin(ang)
    cos64 = jnp.concatenate([cos[0], cos[0], cos[1], cos[1]], axis=-1)
    sin64 = jnp.concatenate([-sin[0], sin[0], -sin[1], sin[1]], axis=-1)
    return jnp.concatenate([cos64, cos64], axis=-1), jnp.concatenate([sin64, sin64], axis=-1)


def _df_kernel(lam_ref, q_ref, *refs, n_x, n_c, rope, out_scale):
    if n_x:
        kx_ref, vx_ref, kc_ref, vc_ref, cq_ref, sq_ref, ck_ref, sk_ref, gq_ref, gk_ref, gs_ref, o_ref, kn, vn = refs
    else:
        kc_ref, vc_ref, gq_ref, gk_ref, gs_ref, o_ref, kn, vn = refs
    lane = lax.broadcasted_iota(jnp.int32, (1, 2 * DF_DIM), 1)
    lo = lane < DF_DIM
    first = (lane % (DF_DIM // 2)) < (DF_DIM // 4)

    def norm64(x, g):
        x = x.astype(F32)
        x2 = x * x
        s0 = jnp.sum(jnp.where(lo, x2, 0.0), axis=-1, keepdims=True)
        s1 = jnp.sum(jnp.where(lo, 0.0, x2), axis=-1, keepdims=True)
        ms = jnp.where(lo, s0, s1) * (1.0 / DF_DIM)
        return x * lax.rsqrt(ms + EPS) * g

    def rot(x, cos, sin):
        partner = jnp.where(first, pltpu.roll(x, 2 * DF_DIM - DF_DIM // 4, 1), pltpu.roll(x, DF_DIM // 4, 1))
        return x * cos + partner * sin

    @pl.when(pl.program_id(2) == 0)
    def _():
        if n_x:
            kx = rot(norm64(kx_ref[0], gk_ref[...]), ck_ref[...], sk_ref[...])
            kn[0:n_x, :] = kx.astype(BF16)
            vn[0:n_x, :] = vx_ref[0].astype(BF16)
        kn[n_x:n_x + n_c, :] = norm64(kc_ref[0], gk_ref[...]).astype(BF16)
        vn[n_x:n_x + n_c, :] = vc_ref[0].astype(BF16)

    q = norm64(q_ref[0], gq_ref[...])
    if rope:
        q = rot(q, cq_ref[...], sq_ref[...])
    q = q * (DF_DIM ** -0.5 * LOG2_E)
    nt = (((1,), (1,)), ((), ()))

    n_k = n_x + n_c
    qs = (jnp.where(lo, q, 0.0).astype(BF16), jnp.where(lo, 0.0, q).astype(BF16))
    tq = q.shape[0]
    ms = [jnp.full((tq, 1), -jnp.inf, F32)] * 2
    ls = [jnp.zeros((tq, 1), F32)] * 2
    accs = [jnp.zeros((tq, 2 * DF_DIM), F32)] * 2
    for c0 in range(0, n_k, DF_KCHUNK):
        c1 = min(c0 + DF_KCHUNK, n_k)
        for mi in range(2):
            s = lax.dot_general(qs[mi], kn[c0:c1, :], nt, preferred_element_type=F32)
            m_new = jnp.maximum(ms[mi], jnp.max(s, axis=-1, keepdims=True))
            alpha = jnp.exp2(ms[mi] - m_new)
            p = jnp.exp2(s - m_new)
            ls[mi] = alpha * ls[mi] + jnp.sum(p, axis=-1, keepdims=True)
            accs[mi] = alpha * accs[mi] + jnp.dot(p.astype(BF16), vn[c0:c1, :], preferred_element_type=F32)
            ms[mi] = m_new
    o = accs[0] * (1.0 / ls[0]) - accs[1] * (lam_ref[0] / ls[1])
    o_ref[0] = (_rms(o, gs_ref[...]) * out_scale).astype(BF16)


def _df_attention(lam, zq, zx, zc, cos, sin, gq, gk, gs, out_scale):
    b, t, _ = zq.shape
    c = zc.shape[1]
    n_x = 0 if zx is None else zx.shape[1]
    w = 2 * DF_DIM
    tq = min(DF_QBLK, t)
    gq2 = jnp.concatenate([gq, gq]).reshape(1, w)
    gk2 = jnp.concatenate([gk, gk]).reshape(1, w)
    col = lambda base: (lambda bi, h, i: (bi, 0, base // w + h))
    vec = pl.BlockSpec((1, w), lambda bi, h, i: (0, 0))
    in_specs = [pl.BlockSpec(memory_space=pltpu.SMEM),
                pl.BlockSpec((1, tq, w), lambda bi, h, i: (bi, i, COL_DF_Q // w + h))]
    args = [lam.reshape(1), zq]
    if n_x:
        in_specs += [pl.BlockSpec((1, n_x, w), col(COL_DF_K)), pl.BlockSpec((1, n_x, w), col(COL_DF_V))]
        args += [zx, zx]
    in_specs += [pl.BlockSpec((1, c, w), col(COL_DF_K)), pl.BlockSpec((1, c, w), col(COL_DF_V))]
    args += [zc, zc]
    if n_x:
        in_specs += [pl.BlockSpec((tq, w), lambda bi, h, i: (i, 0)), pl.BlockSpec((tq, w), lambda bi, h, i: (i, 0)),
                     pl.BlockSpec((n_x, w), lambda bi, h, i: (0, 0)), pl.BlockSpec((n_x, w), lambda bi, h, i: (0, 0))]
        args += [cos, sin, cos, sin]
    in_specs += [vec, vec, vec]
    args += [gq2, gk2, gs.reshape(1, w)]
    return pl.pallas_call(
        functools.partial(_df_kernel, n_x=n_x, n_c=c, rope=bool(n_x), out_scale=out_scale),
        grid=(b, N_HEADS, t // tq),
        in_specs=in_specs,
        out_specs=pl.BlockSpec((1, tq, w), lambda bi, h, i: (bi, i, h)),
        out_shape=jax.ShapeDtypeStruct((b, t, N_HEADS * w), BF16),
        scratch_shapes=[pltpu.VMEM((n_x + c, w), BF16), pltpu.VMEM((n_x + c, w), BF16)],
        compiler_params=_cparams("parallel", "parallel", "arbitrary"),
        name="diff_attention" if n_x else "diff_ctx_attention",
    )(*args)


def _rg_kernel(*refs, n_x, n_c, ctx_out):
    if ctx_out:
        (ux_ref, uc_ref, gx_ref, gc_ref, cw_ref, cb_ref, wa_ref, ba_ref, wx_ref, bx_ref, lam_ref,
         ox_ref, oc_ref, a_s, b_s, p_s) = refs
    else:
        (ux_ref, uc_ref, gx_ref, cw_ref, cb_ref, wa_ref, ba_ref, wx_ref, bx_ref, lam_ref,
         ox_ref, a_s, b_s, p_s) = refs
    n_t = n_x + n_c
    clen = n_t // SCAN_CHUNKS
    cstride = clen + SCAN_PAD_ROWS

    def conv(z):
        z = z.astype(F32)
        n = z.shape[0]
        t = lax.broadcasted_iota(jnp.int32, (n, 1), 0)
        zm2 = jnp.where(t >= 2, pltpu.roll(z, 2, 0), 0.0)
        zm1 = jnp.where(t >= 1, pltpu.roll(z, 1, 0), 0.0)
        zp1 = jnp.where(t < n - 1, pltpu.roll(z, n - 1, 0), 0.0)
        return (zm2 * cw_ref[0:1, :] + zm1 * cw_ref[1:2, :] + z * cw_ref[2:3, :] + zp1 * cw_ref[3:4, :]
                + cb_ref[...])

    def pieces(t0, n):
        out, t = [], t0
        while t < t0 + n:
            ch = t // clen
            stop = min((ch + 1) * clen, t0 + n)
            out.append((t - t0, ch * cstride + (t - ch * clen), stop - t))
            t = stop
        return out

    def put(ref, d, t0, val):
        for off, row, ln in pieces(t0, val.shape[0]):
            ref[d, row:row + ln, :] = val[off:off + ln]

    def get(ref, d, t0, n):
        return jnp.concatenate([ref[d, row:row + ln, :] for _, row, ln in pieces(t0, n)], axis=0)

    def coeffs(u, d, t0):
        ub = u.astype(BF16)
        r = jax.nn.sigmoid(jnp.dot(ub, wa_ref[d, 0], preferred_element_type=F32) + ba_ref[d:d + 1, :])
        gi = jax.nn.sigmoid(jnp.dot(ub, wx_ref[d, 0], preferred_element_type=F32) + bx_ref[d:d + 1, :])
        log_a = (-RG_C) * r * jax.nn.softplus(-lam_ref[d:d + 1, :])
        a = jnp.exp(log_a)
        put(a_s, d, t0, a)
        put(b_s, d, t0, jnp.sqrt(-jnp.tanh(log_a) * (a * a + 1.0)) * (gi * u))

    u_c = conv(uc_ref[0])
    coeffs(u_c, 0, 0)
    coeffs(u_c, 1, n_x)
    u_x = conv(ux_ref[0])
    coeffs(u_x, 0, n_c)
    coeffs(u_x, 1, 0)

    def step(tau, carry):
        h_f, p_f, h_r, p_r = carry
        i_f = pl.ds(tau, SCAN_CHUNKS, stride=cstride)
        i_r = pl.ds(clen - 1 - tau, SCAN_CHUNKS, stride=cstride)
        a_f = a_s[0, i_f, :]
        a_r = a_s[1, i_r, :]
        h_f = a_f * h_f + b_s[0, i_f, :]
        h_r = a_r * h_r + b_s[1, i_r, :]
        p_f = a_f * p_f
        p_r = a_r * p_r
        b_s[0, i_f, :] = h_f
        b_s[1, i_r, :] = h_r
        p_s[0, i_f, :] = p_f
        p_s[1, i_r, :] = p_r
        return h_f, p_f, h_r, p_r

    zeros = jnp.zeros((SCAN_CHUNKS, RG_BW), F32)
    ones = jnp.ones((SCAN_CHUNKS, RG_BW), F32)
    h_f, p_f, h_r, p_r = lax.fori_loop(0, clen, step, (zeros, ones, zeros, ones), unroll=2)

    carry = jnp.zeros((1, RG_BW), F32)
    for ch in range(1, SCAN_CHUNKS):
        carry = p_f[ch - 1:ch, :] * carry + h_f[ch - 1:ch, :]
        rows = slice(ch * cstride, ch * cstride + clen)
        b_s[0, rows, :] = b_s[0, rows, :] + p_s[0, rows, :] * carry
    carry = jnp.zeros((1, RG_BW), F32)
    for ch in range(SCAN_CHUNKS - 2, -1, -1):
        carry = p_r[ch + 1:ch + 2, :] * carry + h_r[ch + 1:ch + 2, :]
        rows = slice(ch * cstride, ch * cstride + clen)
        b_s[1, rows, :] = b_s[1, rows, :] + p_s[1, rows, :] * carry

    gx = jax.nn.gelu(gx_ref[0].astype(F32), approximate=True)
    ox_ref[0] = ((get(b_s, 0, n_c, n_x) + get(b_s, 1, 0, n_x)) * gx).astype(BF16)
    if ctx_out:
        gc = jax.nn.gelu(gc_ref[0].astype(F32), approximate=True)
        oc_ref[0] = ((get(b_s, 0, 0, n_c) + get(b_s, 1, n_x, n_c)) * gc).astype(BF16)


def _rg_lru(zx, zc, conv_w, conv_b, w_a, b_a, w_x, b_x, lam, ctx_out):
    b, n_x, _ = zx.shape
    n_c = zc.shape[1]
    bw = RG_BW
    col = lambda base: (lambda bi, n: (bi, 0, base // bw + n))
    vec2 = pl.BlockSpec((2, bw), lambda bi, n: (0, n))
    wspec = pl.BlockSpec((2, 1, bw, bw), lambda bi, n: (0, n, 0, 0))
    in_specs = [pl.BlockSpec((1, n_x, bw), col(COL_RG_X)), pl.BlockSpec((1, n_c, bw), col(COL_RG_X)),
                pl.BlockSpec((1, n_x, bw), col(COL_RG_G))]
    args = [zx, zc, zx]
    if ctx_out:
        in_specs.append(pl.BlockSpec((1, n_c, bw), col(COL_RG_G)))
        args.append(zc)
    in_specs += [pl.BlockSpec((4, bw), lambda bi, n: (0, n)), pl.BlockSpec((1, bw), lambda bi, n: (0, n)),
                 wspec, vec2, wspec, vec2, vec2]
    args += [conv_w, conv_b.reshape(1, RG_WIDTH), w_a, b_a, w_x, b_x, lam]
    out_specs = [pl.BlockSpec((1, n_x, bw), lambda bi, n: (bi, 0, n))]
    out_shape = [jax.ShapeDtypeStruct((b, n_x, RG_WIDTH), BF16)]
    if ctx_out:
        out_specs.append(pl.BlockSpec((1, n_c, bw), lambda bi, n: (bi, 0, n)))
        out_shape.append(jax.ShapeDtypeStruct((b, n_c, RG_WIDTH), BF16))
    n_s = SCAN_CHUNKS * ((n_x + n_c) // SCAN_CHUNKS + SCAN_PAD_ROWS)
    outs = pl.pallas_call(
        functools.partial(_rg_kernel, n_x=n_x, n_c=n_c, ctx_out=ctx_out),
        grid=(b, RG_BLOCKS),
        in_specs=in_specs,
        out_specs=out_specs,
        out_shape=out_shape,
        scratch_shapes=[pltpu.VMEM((2, n_s, bw), F32), pltpu.VMEM((2, n_s, bw), F32), pltpu.VMEM((2, n_s, bw), F32)],
        compiler_params=_cparams("parallel", "parallel"),
        name="rg_lru",
    )(*args)
    return outs if ctx_out else (outs[0], None)


def _merge_kernel(ya_ref, yb_ref, yc_ref, ga_ref, gb_ref, gc_ref, wa_ref, wb_ref, wc_ref, o_ref):
    gate = lambda ref: jax.nn.sigmoid(ref[0].astype(F32))
    m = gate(ga_ref) * jnp.dot(ya_ref[0], wa_ref[...], preferred_element_type=F32)
    m = m + gate(gb_ref) * jnp.dot(yb_ref[0], wb_ref[...], preferred_element_type=F32)
    m = m + gate(gc_ref) * jnp.dot(yc_ref[0], wc_ref[...], preferred_element_type=F32)
    o_ref[0] = m.astype(BF16)


def _branch_merge(ya, yb, yc, z, w_branch, tm, tn):
    b, t, _ = ya.shape
    d = D_MODEL
    wa, wb, wc = w_branch[:512], w_branch[512:1024], w_branch[1024:]
    gate = lambda k: (lambda bi, i, j: (bi, i, (COL_GATE + k * d) // tn + j))
    return pl.pallas_call(
        _merge_kernel,
        grid=(b, t // tm, d // tn),
        in_specs=[
            pl.BlockSpec((1, tm, 512), lambda bi, i, j: (bi, i, 0)),
            pl.BlockSpec((1, tm, 512), lambda bi, i, j: (bi, i, 0)),
            pl.BlockSpec((1, tm, 1024), lambda bi, i, j: (bi, i, 0)),
            pl.BlockSpec((1, tm, tn), gate(0)),
            pl.BlockSpec((1, tm, tn), gate(1)),
            pl.BlockSpec((1, tm, tn), gate(2)),
            pl.BlockSpec((512, tn), lambda bi, i, j: (0, j)),
            pl.BlockSpec((512, tn), lambda bi, i, j: (0, j)),
            pl.BlockSpec((1024, tn), lambda bi, i, j: (0, j)),
        ],
        out_specs=pl.BlockSpec((1, tm, tn), lambda bi, i, j: (bi, i, j)),
        out_shape=jax.ShapeDtypeStruct((b, t, d), BF16),
        compiler_params=_cparams("parallel", "parallel", "arbitrary"),
        name="branch_merge",
    )(ya, yb, yc, z, z, z, wa, wb, wc)


def _resid_kernel(m_ref, w_ref, x_ref, gt_ref, o_ref):
    o_ref[0] = x_ref[0] + gt_ref[0] * jnp.dot(m_ref[0], w_ref[...], preferred_element_type=F32)


def _out_proj_residual(m, w_out, x, gt, tm, tn):
    b, t, d = x.shape
    return pl.pallas_call(
        _resid_kernel,
        grid=(b, t // tm, d // tn),
        in_specs=[
            pl.BlockSpec((1, tm, d), lambda bi, i, j: (bi, i, 0)),
            pl.BlockSpec((d, tn), lambda bi, i, j: (0, j)),
            pl.BlockSpec((1, tm, tn), lambda bi, i, j: (bi, i, j)),
            pl.BlockSpec((1, 1, tn), lambda bi, i, j: (bi, 0, j)),
        ],
        out_specs=pl.BlockSpec((1, tm, tn), lambda bi, i, j: (bi, i, j)),
        out_shape=jax.ShapeDtypeStruct((b, t, d), F32),
        compiler_params=_cparams("parallel", "parallel", "arbitrary"),
        name="out_proj_residual",
    )(m, w_out, x, gt)


def _router_kernel(*refs, starts):
    n_s = len(starts) - 1
    g_ref, wh_ref, wl_ref, br_ref, h_ref, eid_ref, wt_ref = refs[3 * n_s:]
    i = pl.program_id(0)
    for k in range(n_s):
        x_ref, sc_ref, sh_ref = refs[3 * k:3 * k + 3]

        @pl.when((i >= starts[k]) & (i < starts[k + 1]))
        def _(x_ref=x_ref, sc_ref=sc_ref, sh_ref=sh_ref):
            _route_rows(x_ref, sc_ref, sh_ref, g_ref, wh_ref, wl_ref, br_ref, h_ref, eid_ref, wt_ref)


def _route_rows(x_ref, sc_ref, sh_ref, g_ref, wh_ref, wl_ref, br_ref, h_ref, eid_ref, wt_ref):
    h = _rms(x_ref[0], g_ref[...]) * (1.0 + sc_ref[0]) + sh_ref[0]
    h_ref[...] = h
    hh = h.astype(BF16)
    hl = (h - hh.astype(F32)).astype(BF16)
    logits = (jnp.dot(hh, wh_ref[...], preferred_element_type=F32)
              + jnp.dot(hl, wh_ref[...], preferred_element_type=F32)
              + jnp.dot(hh, wl_ref[...], preferred_element_type=F32)) + br_ref[...]
    lane = lax.broadcasted_iota(jnp.int32, logits.shape, 1)
    lane_f = lane.astype(F32)

    def first_argmax(v, valid):
        vm = jnp.where(valid, v, -jnp.inf)
        mx = jnp.max(vm, axis=-1, keepdims=True)
        idx = jnp.min(jnp.where(valid & (vm == mx), lane_f, 1e9), axis=-1, keepdims=True)
        return mx, idx.astype(jnp.int32)

    is_g = lane < N_GROUPS
    gmax, grp = first_argmax(logits, is_g)
    p_grp = 1.0 / jnp.sum(jnp.where(is_g, jnp.exp(logits - gmax), 0.0), axis=-1, keepdims=True)
    e_lo = N_GROUPS + grp * EXPERTS_PER_GROUP
    in_grp = (lane >= e_lo) & (lane < e_lo + EXPERTS_PER_GROUP)
    v0, i0 = first_argmax(logits, in_grp)
    v1, i1 = first_argmax(logits, in_grp & (lane != i0))
    e1 = jnp.exp(v1 - v0)
    w0 = p_grp / (1.0 + e1)
    w1 = p_grp * e1 / (1.0 + e1)
    eid_ref[...] = jnp.where(lane == 0, i0 - N_GROUPS, jnp.where(lane == 1, i1 - N_GROUPS, 0))
    wt_ref[...] = jnp.where(lane == 0, w0, jnp.where(lane == 1, w1, 0.0))


def _router(xs, scs, shs, g, wr_hi, wr_lo, br, tm):
    d = xs[0].shape[-1]
    starts = [0]
    in_specs, args = [], []
    for x, sc, sh in zip(xs, scs, shs):
        b, t, _ = x.shape
        n_i = t // tm
        s0 = starts[-1]
        starts.append(s0 + b * n_i)

        def local(i, s0=s0, n=b * n_i):
            return jnp.clip(i - s0, 0, n - 1)

        in_specs += [pl.BlockSpec((1, tm, d), lambda i, f=local, n_i=n_i: (f(i) // n_i, f(i) % n_i, 0)),
                     pl.BlockSpec((1, 1, d), lambda i, f=local, n_i=n_i: (f(i) // n_i, 0, 0)),
                     pl.BlockSpec((1, 1, d), lambda i, f=local, n_i=n_i: (f(i) // n_i, 0, 0))]
        args += [x, sc, sh]
    n_all = starts[-1] * tm
    in_specs += [pl.BlockSpec((1, d), lambda i: (0, 0)), pl.BlockSpec((d, 128), lambda i: (0, 0)),
                 pl.BlockSpec((d, 128), lambda i: (0, 0)), pl.BlockSpec((1, 128), lambda i: (0, 0))]
    args += [g.reshape(1, d), wr_hi, wr_lo, br]
    flat = lambda width: pl.BlockSpec((tm, width), lambda i: (i, 0))
    return pl.pallas_call(
        functools.partial(_router_kernel, starts=tuple(starts)),
        grid=(starts[-1],),
        in_specs=in_specs,
        out_specs=[flat(d), flat(128), flat(128)],
        out_shape=[jax.ShapeDtypeStruct((n_all, d), F32), jax.ShapeDtypeStruct((n_all, 128), jnp.int32),
                   jax.ShapeDtypeStruct((n_all, 128), F32)],
        compiler_params=_cparams("parallel"),
        name="moe_router",
    )(*args)


def _row_copy(src_hbm, dst_vmem, sem, src_row, dst_row):
    return pltpu.make_async_copy(src_hbm.at[pl.ds(src_row, 1)], dst_vmem.at[pl.ds(dst_row, 1)], sem)


def _dispatch_kernel(tok_ref, tok_next_ref, src_hbm, o_ref, buf0, buf1, sem):
    rows = buf0.shape[0]
    i = pl.program_id(0)
    n = pl.num_programs(0)
    bufs = (buf0, buf1)

    def wait_rows(p):
        def wait(r, c):
            _row_copy(src_hbm, bufs[p], sem.at[p], 0, r).wait()
            return c
        lax.fori_loop(0, rows, wait, 0, unroll=8)

    @pl.when(i == 0)
    def _():
        def issue(r, c):
            _row_copy(src_hbm, buf0, sem.at[0], tok_ref[0, 0, r], r).start()
            return c
        lax.fori_loop(0, rows, issue, 0, unroll=8)

    def block(cur):
        wait_rows(cur)
        for r in range(rows):
            _row_copy(src_hbm, bufs[1 - cur], sem.at[1 - cur], tok_next_ref[0, 0, r], r).start(priority=r % 2)
        o_ref[...] = bufs[cur][...].astype(BF16)

    for parity in range(2):
        @pl.when(i % 2 == parity)
        def _(parity=parity):
            block(parity)

        @pl.when((i == n - 1) & (i % 2 == parity))
        def _(parity=parity):
            wait_rows(1 - parity)


def _dispatch(buf_tok, x_rows, rows):
    n_pad = buf_tok.shape[0]
    d = x_rows.shape[1]
    while n_pad % rows:
        rows //= 2
    n = n_pad // rows
    tok = buf_tok.reshape(n, 1, rows)
    return pl.pallas_call(
        _dispatch_kernel,
        grid=(n,),
        in_specs=[pl.BlockSpec((1, 1, rows), lambda i: (i, 0, 0), memory_space=pltpu.SMEM),
                  pl.BlockSpec((1, 1, rows), lambda i: (jnp.minimum(i + 1, n - 1), 0, 0), memory_space=pltpu.SMEM),
                  pl.BlockSpec(memory_space=pl.ANY)],
        out_specs=pl.BlockSpec((rows, d), lambda i: (i, 0)),
        out_shape=jax.ShapeDtypeStruct((n_pad, d), BF16),
        scratch_shapes=[pltpu.VMEM((rows, d), x_rows.dtype), pltpu.VMEM((rows, d), x_rows.dtype),
                        pltpu.SemaphoreType.DMA((2,))],
        compiler_params=_cparams("arbitrary"),
        name="moe_dispatch",
    )(tok, tok, x_rows)


def _expert_kernel(blk_e_ref, x_ref, w1_ref, w3_ref, w2_ref, o_ref):
    del blk_e_ref
    xb = x_ref[...].astype(BF16)
    h1 = jnp.dot(xb, w1_ref[0], preferred_element_type=F32)
    h3 = jnp.dot(xb, w3_ref[0], preferred_element_type=F32)
    hid = (h1 * jax.nn.sigmoid(h1) * h3).astype(BF16)
    o_ref[...] = jnp.dot(hid, w2_ref[0], preferred_element_type=F32)


def _expert_blocks(blk_e, x_sorted, w1, w3, w2):
    n_blk = blk_e.shape[0]
    tm = MOE_TM
    d = x_sorted.shape[1]
    grid_spec = pltpu.PrefetchScalarGridSpec(
        num_scalar_prefetch=1,
        grid=(n_blk,),
        in_specs=[
            pl.BlockSpec((tm, d), lambda i, e: (i, 0)),
            pl.BlockSpec((1, d, D_EXPERT), lambda i, e: (e[i], 0, 0)),
            pl.BlockSpec((1, d, D_EXPERT), lambda i, e: (e[i], 0, 0)),
            pl.BlockSpec((1, D_EXPERT, d), lambda i, e: (e[i], 0, 0)),
        ],
        out_specs=pl.BlockSpec((tm, d), lambda i, e: (i, 0)),
    )
    return pl.pallas_call(
        _expert_kernel,
        grid_spec=grid_spec,
        out_shape=jax.ShapeDtypeStruct((n_blk * tm, d), F32),
        compiler_params=_cparams("arbitrary"),
        name="moe_experts",
    )(blk_e, x_sorted, w1, w3, w2)


def _combine_kernel(slot_ref, slot_next_ref, y_hbm, x_ref, gt_ref, wt_ref, o_ref, a0, b0, a1, b1, sem):
    tm = a0.shape[0]
    i = pl.program_id(0)
    n = pl.num_programs(0)
    bufs = ((a0, b0), (a1, b1))

    def wait_rows(p):
        def wait(r, c):
            _row_copy(y_hbm, bufs[p][0], sem.at[p], 0, r).wait()
            _row_copy(y_hbm, bufs[p][1], sem.at[p], 0, r).wait()
            return c
        lax.fori_loop(0, tm, wait, 0, unroll=8)

    @pl.when(i == 0)
    def _():
        def issue(r, c):
            _row_copy(y_hbm, a0, sem.at[0], slot_ref[0, 0, 2 * r], r).start()
            _row_copy(y_hbm, b0, sem.at[0], slot_ref[0, 0, 2 * r + 1], r).start()
            return c
        lax.fori_loop(0, tm, issue, 0, unroll=8)

    def block(cur):
        wait_rows(cur)
        nxt_a, nxt_b = bufs[1 - cur]
        for r in range(tm):
            _row_copy(y_hbm, nxt_a, sem.at[1 - cur], slot_next_ref[0, 0, 2 * r], r).start(priority=0)
            _row_copy(y_hbm, nxt_b, sem.at[1 - cur], slot_next_ref[0, 0, 2 * r + 1], r).start(priority=1)
        wt = wt_ref[...]
        f = bufs[cur][0][...] * wt[:, 0:1] + bufs[cur][1][...] * wt[:, 1:2]
        o_ref[...] = x_ref[...] + gt_ref[0] * f

    for parity in range(2):
        @pl.when(i % 2 == parity)
        def _(parity=parity):
            block(parity)

        @pl.when((i == n - 1) & (i % 2 == parity))
        def _(parity=parity):
            wait_rows(1 - parity)


def _combine(slots, y, x, gt, wt, tok0, tm):
    b, t, d = x.shape
    n_i = t // tm
    n = b * n_i
    x2 = x.reshape(b * t, d)
    blk0 = tok0 // tm
    slots3 = slots.reshape(-1, 1, 2 * tm)
    out = pl.pallas_call(
        _combine_kernel,
        grid=(n,),
        in_specs=[
            pl.BlockSpec((1, 1, 2 * tm), lambda i: (blk0 + i, 0, 0), memory_space=pltpu.SMEM),
            pl.BlockSpec((1, 1, 2 * tm), lambda i: (blk0 + jnp.minimum(i + 1, n - 1), 0, 0),
                         memory_space=pltpu.SMEM),
            pl.BlockSpec(memory_space=pl.ANY),
            pl.BlockSpec((tm, d), lambda i: (i, 0)),
            pl.BlockSpec((1, 1, d), lambda i: (i // n_i, 0, 0)),
            pl.BlockSpec((tm, 128), lambda i: (blk0 + i, 0)),
        ],
        out_specs=pl.BlockSpec((tm, d), lambda i: (i, 0)),
        out_shape=jax.ShapeDtypeStruct((b * t, d), F32),
        scratch_shapes=[pltpu.VMEM((tm, d), F32)] * 4 + [pltpu.SemaphoreType.DMA((2,))],
        compiler_params=_cparams("arbitrary"),
        name="moe_combine",
    )(slots3, slots3, y, x2, gt, wt)
    return out.reshape(b, t, d)


def _routing_tables(eid):
    m = eid.shape[0]
    tm = MOE_TM
    i32 = jnp.int32
    iota = jnp.arange(m, dtype=i32)
    se, order = lax.sort_key_val(eid, iota)
    onehot = (se[:, None] == jnp.arange(N_EXPERTS, dtype=i32)[None, :]).astype(i32)
    counts = jnp.sum(onehot, axis=0)
    starts = jnp.cumsum(counts) - counts
    pcounts = (counts + tm - 1) // tm * tm
    pends = jnp.cumsum(pcounts)
    pstarts = pends - pcounts
    dest = iota + jnp.sum(onehot * (pstarts - starts)[None, :], axis=1)
    _, slot = lax.sort_key_val(order, dest)
    n_blk = (m + N_EXPERTS * (tm - 1) + tm - 1) // tm
    blk_start = jnp.arange(n_blk, dtype=i32) * tm
    blk_e = jnp.minimum(jnp.sum((pends[None, :] <= blk_start[:, None]).astype(i32), axis=1), N_EXPERTS - 1)
    off = (blk_start - pstarts[blk_e])[:, None] + jnp.arange(tm, dtype=i32)[None, :]
    valid = off < counts[blk_e][:, None]
    src = jnp.clip(starts[blk_e][:, None] + off, 0, m - 1)
    buf_tok = jnp.where(valid, order[src] // TOP_K, 0).reshape(-1)
    return blk_e.astype(i32), buf_tok, slot


def _moe(xs, scs, shs, gts, norm_g, wr_hi, wr_lo, br, w1, w3, w2):
    h_all, eid_all, wt_all = _router(xs, scs, shs, norm_g, wr_hi, wr_lo, br, 512)
    blk_e, buf_tok, slot = _routing_tables(eid_all[:, :TOP_K].reshape(-1))
    y = _expert_blocks(blk_e, _dispatch(buf_tok, h_all, MOE_DISPATCH_ROWS), w1, w3, w2)
    outs = []
    tok0 = 0
    for x, gt in zip(xs, gts):
        outs.append(_combine(slot, y, x, gt, wt_all, tok0, tm=256))
        tok0 += x.shape[0] * x.shape[1]
    return outs


def kernel(x, c, ctx, c_ctx, w_mod, b_mod, norm1_g, norm2_g, w_in, na_q_g, na_k_g, na_rpb, df_q_g, df_k_g, df_lam, df_sub_g, rg_conv_w, rg_conv_b, rg_w_a, rg_b_a, rg_w_x, rg_b_x, rg_lam, w_branch, w_out, w_router_g, b_router_g, w_router_e, b_router_e, w1, w3, w2):
    B, S, D = x.shape
    C = ctx.shape[1]
    n_rows = S // GRID_W
    rope_cos, rope_sin = _rope_tables(S)
    na_lo, na_hi = _na_bias_tables(na_rpb)

    pad = (-(B + 1)) % 8
    rows = jnp.concatenate([c, c_ctx[None, :], jnp.zeros((pad, D), F32)], axis=0)
    mod_all = _modulation(rows, w_mod, b_mod)

    xc = ctx.reshape(1, B * C, D)
    for l in range(DEPTH):
        need_ctx = l < DEPTH - 1
        lam_init = 0.8 - 0.6 * float(np.exp(-0.3 * l))
        mod = mod_all[l]
        sh1, sc1, gt1, sh2, sc2, gt2 = [mod[:B, k * D:(k + 1) * D][:, None, :] for k in range(6)]
        csh1, csc1, cgt1, csh2, csc2, cgt2 = [mod[B:B + 1, k * D:(k + 1) * D][:, None, :] for k in range(6)]

        w_in_l = w_in[l].astype(BF16)
        zx = _norm_mod_matmul(x, norm1_g[l], sc1, sh1, w_in_l, tm=1024, tn=1024)
        n_cc = IN_COLS if need_ctx else KV_COLS
        zc = _norm_mod_matmul(xc, norm1_g[l], csc1, csh1, w_in_l[:, :n_cc], tm=1024, tn=1024).reshape(B, C, n_cc)

        y_a = _na_attention(zx, zc, na_lo[l], na_hi[l], na_q_g[l], na_k_g[l])

        lp = df_lam[l]
        lam = jnp.exp(jnp.sum(lp[0] * lp[1])) - jnp.exp(jnp.sum(lp[2] * lp[3])) + lam_init
        y_b = _df_attention(lam, zx, zx, zc, rope_cos, rope_sin, df_q_g[l], df_k_g[l], df_sub_g[l], 1.0 - lam_init)

        y_c, y_cc = _rg_lru(zx, zc, rg_conv_w[l], rg_conv_b[l], rg_w_a[l].astype(BF16), rg_b_a[l],
                            rg_w_x[l].astype(BF16), rg_b_x[l], rg_lam[l], need_ctx)

        wb_l = w_branch[l].astype(BF16)
        wo_l = w_out[l].astype(BF16)
        m_x = _branch_merge(y_a, y_b, y_c, zx, wb_l, tm=1024, tn=1024)
        x = _out_proj_residual(m_x, wo_l, x, gt1, tm=1024, tn=1024)
        if need_ctx:
            y_ac = _na_ctx_attention(zc, na_q_g[l], na_k_g[l])
            y_bc = _df_attention(lam, zc, None, zc, None, None, df_q_g[l], df_k_g[l], df_sub_g[l], 1.0 - lam_init)
            m_c = _branch_merge(y_ac, y_bc, y_cc, zc, wb_l, tm=C, tn=1024)
            xc = _out_proj_residual(m_c.reshape(1, B * C, D), wo_l, xc, cgt1, tm=1024, tn=1024)

        wr = jnp.concatenate([w_router_g[l], w_router_e[l],
                              jnp.zeros((D, 128 - N_GROUPS - N_EXPERTS), F32)], axis=1)
        wr_hi = wr.astype(BF16)
        wr_lo = (wr - wr_hi.astype(F32)).astype(BF16)
        br = jnp.concatenate([b_router_g[l], b_router_e[l],
                              jnp.zeros((128 - N_GROUPS - N_EXPERTS,), F32)]).reshape(1, 128)
        w1_l, w3_l, w2_l = w1[l].astype(BF16), w3[l].astype(BF16), w2[l].astype(BF16)
        if need_ctx:
            xc, x = _moe([xc, x], [csc2, sc2], [csh2, sh2], [cgt2, gt2], norm2_g[l],
                         wr_hi, wr_lo, br, w1_l, w3_l, w2_l)
        else:
            (x,) = _moe([x], [sc2], [sh2], [gt2], norm2_g[l], wr_hi, wr_lo, br, w1_l, w3_l, w2_l)
    return x
```

```python
import functools

import jax
import jax.numpy as jnp
import numpy as np
from jax import lax
from jax.experimental import pallas as pl
from jax.experimental.pallas import tpu as pltpu

F32 = jnp.float32
BF16 = jnp.bfloat16

D_MODEL = 2048
DEPTH = 4
GRID_W = 64
HEAD_DIM = 128
N_HEADS = 4
NA_WIN_R = 8
NA_WIN_C = 16
NA_QCB = 16
NA_KCB = 32
DF_DIM = 64
RG_WIDTH = 1024
RG_BLOCKS = 8
RG_BW = 128
RG_C = 8.0
N_GROUPS = 4
EXPERTS_PER_GROUP = 8
N_EXPERTS = 32
TOP_K = 2
D_EXPERT = 512
ROPE_BASE = 10000.0
EPS = 1e-6
NEG = -1e30
LOG2_E = 1.4426950408889634

COL_NA_K, COL_NA_V, COL_DF_K, COL_DF_V, COL_RG_X = 0, 512, 1024, 1536, 2048
COL_NA_Q, COL_DF_Q, COL_RG_G, COL_GATE = 3072, 3584, 4096, 5120
KV_COLS = 3072
MIX_COLS = 5120
IN_COLS = MIX_COLS + 3 * D_MODEL

VMEM_LIMIT_BYTES = 56 * 1024 * 1024

NA_ROWS_PER_STEP = 4
NA_KEY_ROWS = NA_ROWS_PER_STEP + NA_WIN_R
DF_QBLK = 1024
DF_KCHUNK = 768
MOE_TM = 256
MOE_DISPATCH_TOKENS = 256
SCAN_CHUNKS = 8
SCAN_PAD_ROWS = 8


def _cparams(*sem):
    return pltpu.CompilerParams(dimension_semantics=sem, vmem_limit_bytes=VMEM_LIMIT_BYTES)


def _rms(x, g):
    x = x.astype(F32)
    return x * lax.rsqrt(jnp.mean(x * x, axis=-1, keepdims=True) + EPS) * g


def _mod_kernel(s_ref, w_ref, b_ref, o_ref):
    s = s_ref[...]
    a = (s * jax.nn.sigmoid(s)).astype(BF16)
    o_ref[0] = jnp.dot(a, w_ref[0].astype(BF16), preferred_element_type=F32) + b_ref[0]


def _modulation(rows, w_mod, b_mod, tn=1024):
    n_l, d, n = w_mod.shape
    r = rows.shape[0]
    return pl.pallas_call(
        _mod_kernel,
        grid=(n_l, n // tn),
        in_specs=[
            pl.BlockSpec((r, d), lambda l, j: (0, 0)),
            pl.BlockSpec((1, d, tn), lambda l, j: (l, 0, j)),
            pl.BlockSpec((1, 1, tn), lambda l, j: (l, 0, j)),
        ],
        out_specs=pl.BlockSpec((1, r, tn), lambda l, j: (l, 0, j)),
        out_shape=jax.ShapeDtypeStruct((n_l, r, n), F32),
        compiler_params=_cparams("parallel", "parallel"),
        name="modulation",
    )(rows, w_mod, b_mod.reshape(n_l, 1, n))


def _nmm_kernel(x_ref, g_ref, sc_ref, sh_ref, w_ref, o_ref, h_ref):
    @pl.when(pl.program_id(2) == 0)
    def _():
        h = _rms(x_ref[0], g_ref[...]) * (1.0 + sc_ref[0]) + sh_ref[0]
        h_ref[...] = h.astype(BF16)

    o_ref[0] = jnp.dot(h_ref[...], w_ref[...], preferred_element_type=F32).astype(o_ref.dtype)


def _norm_mod_matmul(x, g, sc, sh, w, tm, tn):
    b, t, d = x.shape
    n = w.shape[1]
    return pl.pallas_call(
        _nmm_kernel,
        grid=(b, t // tm, n // tn),
        in_specs=[
            pl.BlockSpec((1, tm, d), lambda bi, i, j: (bi, i, 0)),
            pl.BlockSpec((1, d), lambda bi, i, j: (0, 0)),
            pl.BlockSpec((1, 1, d), lambda bi, i, j: (bi, 0, 0)),
            pl.BlockSpec((1, 1, d), lambda bi, i, j: (bi, 0, 0)),
            pl.BlockSpec((d, tn), lambda bi, i, j: (0, j)),
        ],
        out_specs=pl.BlockSpec((1, tm, tn), lambda bi, i, j: (bi, i, j)),
        out_shape=jax.ShapeDtypeStruct((b, t, n), BF16),
        scratch_shapes=[pltpu.VMEM((tm, d), BF16)],
        compiler_params=_cparams("parallel", "parallel", "arbitrary"),
        name="norm_mod_proj",
    )(x, g.reshape(1, d), sc, sh, w)


def _na_window_start(r0):
    lower = jnp.clip(r0 - NA_WIN_R // 2, 0, (2048 // GRID_W) - NA_WIN_R)
    return jnp.minimum(lower, (2048 // GRID_W) - NA_KEY_ROWS)


def _na_bias_tables(rpb):
    w = GRID_W
    qc = np.arange(w)
    kc = np.arange(w)
    qwin = np.clip(qc - NA_WIN_C // 2, 0, w - NA_WIN_C)
    kcol_start = np.clip((qc // NA_QCB) * NA_QCB - (NA_KCB - NA_QCB) // 2, 0, w - NA_KCB)
    col_ok = (kc[None, :] >= qwin[:, None]) & (kc[None, :] < qwin[:, None] + NA_WIN_C)
    col_ok &= (kc[None, :] >= kcol_start[:, None]) & (kc[None, :] < kcol_start[:, None] + NA_KCB)
    edge = w - NA_WIN_C
    lead = rpb.shape[:-1]
    ext = jnp.concatenate([jnp.repeat(rpb[..., :1], edge, axis=-1), rpb, jnp.repeat(rpb[..., -1:], edge, axis=-1),
                           jnp.zeros(lead + (1,), F32)], axis=-1)
    skew = jnp.tile(ext, (1,) * len(lead) + (w,))[..., :w * (2 * w - 1)].reshape(lead + (w, 2 * w - 1))
    toep = jnp.where(col_ok, skew[..., w - 1:], NEG)
    toep = jnp.concatenate([toep, jnp.full(toep.shape[:-3] + (1, w, w), NEG, F32)], axis=-3)
    zeros = jnp.zeros_like(toep)
    return jnp.concatenate([toep, zeros], axis=-1), jnp.concatenate([zeros, toep], axis=-1)


def _na_bias_blocks(n_rows):
    rb, kw = NA_ROWS_PER_STEP, NA_KEY_ROWS
    rows = np.arange(n_rows)
    row_start = np.clip(rows - NA_WIN_R // 2, 0, n_rows - NA_WIN_R)
    n_steps = n_rows // rb
    blk = np.full((n_steps, rb, kw), 2 * NA_WIN_R - 1, np.int32)
    for s in range(n_steps):
        ws = min(int(np.clip(s * rb - NA_WIN_R // 2, 0, n_rows - NA_WIN_R)), n_rows - kw)
        for j in range(rb):
            r = s * rb + j
            for i in range(kw):
                if row_start[r] <= ws + i < row_start[r] + NA_WIN_R:
                    blk[s, j, i] = ws + i - r + (NA_WIN_R - 1)
    return blk.reshape(-1)


def _na_kernel(blk_ref, q_ref, k_ref, v_ref, kc_ref, vc_ref, lo_ref, hi_ref, gq_ref, gk_ref, o_ref,
               kn, vn, knc, vnc):
    step = pl.program_id(1)

    def bias(h):
        rows = []
        for j in range(NA_ROWS_PER_STEP):
            base = (step * NA_ROWS_PER_STEP + j) * NA_KEY_ROWS
            rows.append(jnp.concatenate(
                [lo_ref[h, blk_ref[base + i]] + hi_ref[h, blk_ref[base + i + 1]] for i in range(0, NA_KEY_ROWS, 2)],
                axis=-1))
        return jnp.concatenate(rows, axis=0)

    @pl.when(step == 0)
    def _():
        for h in range(N_HEADS):
            sl = slice(h * HEAD_DIM, (h + 1) * HEAD_DIM)
            kn[:, sl] = _rms(k_ref[0, :, sl], gk_ref[...]).astype(BF16)
            knc[:, sl] = _rms(kc_ref[0, :, sl], gk_ref[...]).astype(BF16)
        vn[...] = v_ref[0].astype(BF16)
        vnc[...] = vc_ref[0].astype(BF16)

    ws = _na_window_start(step * NA_ROWS_PER_STEP)
    start = pl.multiple_of(ws * GRID_W, GRID_W)
    nk = NA_KEY_ROWS * GRID_W
    scale = HEAD_DIM ** -0.5
    nt = (((1,), (1,)), ((), ()))
    for h in range(N_HEADS):
        sl = slice(h * HEAD_DIM, (h + 1) * HEAD_DIM)
        qn = (_rms(q_ref[0, :, sl], gq_ref[...]) * scale).astype(BF16)
        s_loc = lax.dot_general(qn, kn[pl.ds(start, nk), sl], nt, preferred_element_type=F32) + bias(h)
        s_ctx = lax.dot_general(qn, knc[:, sl], nt, preferred_element_type=F32)
        m = jnp.maximum(jnp.max(s_loc, axis=-1, keepdims=True), jnp.max(s_ctx, axis=-1, keepdims=True))
        p_loc = jnp.exp(s_loc - m)
        p_ctx = jnp.exp(s_ctx - m)
        l = jnp.sum(p_loc, axis=-1, keepdims=True) + jnp.sum(p_ctx, axis=-1, keepdims=True)
        o = jnp.dot(p_loc.astype(BF16), vn[pl.ds(start, nk), sl], preferred_element_type=F32)
        o = o + jnp.dot(p_ctx.astype(BF16), vnc[:, sl], preferred_element_type=F32)
        o_ref[0, :, sl] = (o * (1.0 / l)).astype(BF16)


def _na_attention(zx, zc, bias_lo, bias_hi, gq, gk):
    b, s, _ = zx.shape
    c = zc.shape[1]
    w = N_HEADS * HEAD_DIM
    tq = NA_ROWS_PER_STEP * GRID_W
    blk = jnp.asarray(_na_bias_blocks(s // GRID_W))
    table = pl.BlockSpec(bias_lo.shape, lambda bi, i, blk: (0, 0, 0, 0))
    grid_spec = pltpu.PrefetchScalarGridSpec(
        num_scalar_prefetch=1,
        grid=(b, s // tq),
        in_specs=[
            pl.BlockSpec((1, tq, w), lambda bi, i, blk: (bi, i, COL_NA_Q // w)),
            pl.BlockSpec((1, s, w), lambda bi, i, blk: (bi, 0, COL_NA_K // w)),
            pl.BlockSpec((1, s, w), lambda bi, i, blk: (bi, 0, COL_NA_V // w)),
            pl.BlockSpec((1, c, w), lambda bi, i, blk: (bi, 0, COL_NA_K // w)),
            pl.BlockSpec((1, c, w), lambda bi, i, blk: (bi, 0, COL_NA_V // w)),
            table, table,
            pl.BlockSpec((1, HEAD_DIM), lambda bi, i, blk: (0, 0)),
            pl.BlockSpec((1, HEAD_DIM), lambda bi, i, blk: (0, 0)),
        ],
        out_specs=pl.BlockSpec((1, tq, w), lambda bi, i, blk: (bi, i, 0)),
        scratch_shapes=[pltpu.VMEM((s, w), BF16), pltpu.VMEM((s, w), BF16),
                        pltpu.VMEM((c, w), BF16), pltpu.VMEM((c, w), BF16)],
    )
    return pl.pallas_call(
        _na_kernel,
        grid_spec=grid_spec,
        out_shape=jax.ShapeDtypeStruct((b, s, w), BF16),
        compiler_params=_cparams("parallel", "arbitrary"),
        name="na_attention",
    )(blk, zx, zx, zx, zc, zc, bias_lo, bias_hi, gq.reshape(1, HEAD_DIM), gk.reshape(1, HEAD_DIM))


def _na_ctx_kernel(q_ref, k_ref, v_ref, gq_ref, gk_ref, o_ref):
    scale = HEAD_DIM ** -0.5
    nt = (((1,), (1,)), ((), ()))
    for h in range(N_HEADS):
        sl = slice(h * HEAD_DIM, (h + 1) * HEAD_DIM)
        qn = (_rms(q_ref[0, :, sl], gq_ref[...]) * scale).astype(BF16)
        kn = _rms(k_ref[0, :, sl], gk_ref[...]).astype(BF16)
        s = lax.dot_general(qn, kn, nt, preferred_element_type=F32)
        p = jnp.exp(s - jnp.max(s, axis=-1, keepdims=True))
        l = jnp.sum(p, axis=-1, keepdims=True)
        o = jnp.dot(p.astype(BF16), v_ref[0, :, sl].astype(BF16), preferred_element_type=F32)
        o_ref[0, :, sl] = (o * (1.0 / l)).astype(BF16)


def _na_ctx_attention(zc, gq, gk):
    b, c, _ = zc.shape
    w = N_HEADS * HEAD_DIM
    return pl.pallas_call(
        _na_ctx_kernel,
        grid=(b,),
        in_specs=[
            pl.BlockSpec((1, c, w), lambda bi: (bi, 0, COL_NA_Q // w)),
            pl.BlockSpec((1, c, w), lambda bi: (bi, 0, COL_NA_K // w)),
            pl.BlockSpec((1, c, w), lambda bi: (bi, 0, COL_NA_V // w)),
            pl.BlockSpec((1, HEAD_DIM), lambda bi: (0, 0)),
            pl.BlockSpec((1, HEAD_DIM), lambda bi: (0, 0)),
        ],
        out_specs=pl.BlockSpec((1, c, w), lambda bi: (bi, 0, 0)),
        out_shape=jax.ShapeDtypeStruct((b, c, w), BF16),
        compiler_params=_cparams("parallel"),
        name="na_ctx_attention",
    )(zc, zc, zc, gq.reshape(1, HEAD_DIM), gk.reshape(1, HEAD_DIM))


def _rope_tables(n_tok):
    t = np.arange(n_tok)
    pos = np.stack([t // GRID_W, t % GRID_W], axis=0).astype(np.float32)
    n_freq = DF_DIM // 4
    inv = (np.float32(ROPE_BASE) ** (-np.arange(n_freq, dtype=np.float32) / n_freq)).astype(np.float32)
    ang = jnp.asarray(pos[:, :, None] * inv)
    cos, sin = jnp.cos(ang), jnp.sin(ang)
    cos64 = jnp.concatenate([cos[0], cos[0], cos[1], cos[1]], axis=-1)
    sin64 = jnp.concatenate([-sin[0], sin[0], -sin[1], sin[1]], axis=-1)
    return jnp.concatenate([cos64, cos64], axis=-1), jnp.concatenate([sin64, sin64], axis=-1)


def _df_kernel(lam_ref, q_ref, *refs, n_x, n_c, rope, out_scale):
    if n_x:
        kx_ref, vx_ref, kc_ref, vc_ref, cq_ref, sq_ref, ck_ref, sk_ref, gq_ref, gk_ref, gs_ref, o_ref, kn, vn = refs
    else:
        kc_ref, vc_ref, gq_ref, gk_ref, gs_ref, o_ref, kn, vn = refs
    lane = lax.broadcasted_iota(jnp.int32, (1, 2 * DF_DIM), 1)
    lo = lane < DF_DIM
    first = (lane % (DF_DIM // 2)) < (DF_DIM // 4)

    def norm64(x, g):
        x = x.astype(F32)
        x2 = x * x
        s0 = jnp.sum(jnp.where(lo, x2, 0.0), axis=-1, keepdims=True)
        s1 = jnp.sum(jnp.where(lo, 0.0, x2), axis=-1, keepdims=True)
        ms = jnp.where(lo, s0, s1) * (1.0 / DF_DIM)
        return x * lax.rsqrt(ms + EPS) * g

    def rot(x, cos, sin):
        partner = jnp.where(first, pltpu.roll(x, 2 * DF_DIM - DF_DIM // 4, 1), pltpu.roll(x, DF_DIM // 4, 1))
        return x * cos + partner * sin

    @pl.when(pl.program_id(2) == 0)
    def _():
        if n_x:
            kx = rot(norm64(kx_ref[0], gk_ref[...]), ck_ref[...], sk_ref[...])
            kn[0:n_x, :] = kx.astype(BF16)
            vn[0:n_x, :] = vx_ref[0].astype(BF16)
        kn[n_x:n_x + n_c, :] = norm64(kc_ref[0], gk_ref[...]).astype(BF16)
        vn[n_x:n_x + n_c, :] = vc_ref[0].astype(BF16)

    q = norm64(q_ref[0], gq_ref[...])
    if rope:
        q = rot(q, cq_ref[...], sq_ref[...])
    q = q * (DF_DIM ** -0.5 * LOG2_E)
    nt = (((1,), (1,)), ((), ()))

    n_k = n_x + n_c
    qs = (jnp.where(lo, q, 0.0).astype(BF16), jnp.where(lo, 0.0, q).astype(BF16))
    tq = q.shape[0]
    ms = [jnp.full((tq, 1), -jnp.inf, F32)] * 2
    ls = [jnp.zeros((tq, 1), F32)] * 2
    accs = [jnp.zeros((tq, 2 * DF_DIM), F32)] * 2
    for c0 in range(0, n_k, DF_KCHUNK):
        c1 = min(c0 + DF_KCHUNK, n_k)
        for mi in range(2):
            s = lax.dot_general(qs[mi], kn[c0:c1, :], nt, preferred_element_type=F32)
            m_new = jnp.maximum(ms[mi], jnp.max(s, axis=-1, keepdims=True))
            alpha = jnp.exp2(ms[mi] - m_new)
            p = jnp.exp2(s - m_new)
            ls[mi] = alpha * ls[mi] + jnp.sum(p, axis=-1, keepdims=True)
            accs[mi] = alpha * accs[mi] + jnp.dot(p.astype(BF16), vn[c0:c1, :], preferred_element_type=F32)
            ms[mi] = m_new
    o = accs[0] * (1.0 / ls[0]) - accs[1] * (lam_ref[0] / ls[1])
    o_ref[0] = (_rms(o, gs_ref[...]) * out_scale).astype(BF16)


def _df_attention(lam, zq, zx, zc, cos, sin, gq, gk, gs, out_scale):
    b, t, _ = zq.shape
    c = zc.shape[1]
    n_x = 0 if zx is None else zx.shape[1]
    w = 2 * DF_DIM
    tq = min(DF_QBLK, t)
    gq2 = jnp.concatenate([gq, gq]).reshape(1, w)
    gk2 = jnp.concatenate([gk, gk]).reshape(1, w)
    col = lambda base: (lambda bi, h, i: (bi, 0, base // w + h))
    vec = pl.BlockSpec((1, w), lambda bi, h, i: (0, 0))
    in_specs = [pl.BlockSpec(memory_space=pltpu.SMEM),
                pl.BlockSpec((1, tq, w), lambda bi, h, i: (bi, i, COL_DF_Q // w + h))]
    args = [lam.reshape(1), zq]
    if n_x:
        in_specs += [pl.BlockSpec((1, n_x, w), col(COL_DF_K)), pl.BlockSpec((1, n_x, w), col(COL_DF_V))]
        args += [zx, zx]
    in_specs += [pl.BlockSpec((1, c, w), col(COL_DF_K)), pl.BlockSpec((1, c, w), col(COL_DF_V))]
    args += [zc, zc]
    if n_x:
        in_specs += [pl.BlockSpec((tq, w), lambda bi, h, i: (i, 0)), pl.BlockSpec((tq, w), lambda bi, h, i: (i, 0)),
                     pl.BlockSpec((n_x, w), lambda bi, h, i: (0, 0)), pl.BlockSpec((n_x, w), lambda bi, h, i: (0, 0))]
        args += [cos, sin, cos, sin]
    in_specs += [vec, vec, vec]
    args += [gq2, gk2, gs.reshape(1, w)]
    return pl.pallas_call(
        functools.partial(_df_kernel, n_x=n_x, n_c=c, rope=bool(n_x), out_scale=out_scale),
        grid=(b, N_HEADS, t // tq),
        in_specs=in_specs,
        out_specs=pl.BlockSpec((1, tq, w), lambda bi, h, i: (bi, i, h)),
        out_shape=jax.ShapeDtypeStruct((b, t, N_HEADS * w), BF16),
        scratch_shapes=[pltpu.VMEM((n_x + c, w), BF16), pltpu.VMEM((n_x + c, w), BF16)],
        compiler_params=_cparams("parallel", "parallel", "arbitrary"),
        name="diff_attention" if n_x else "diff_ctx_attention",
    )(*args)


def _rg_kernel(*refs, n_x, n_c, ctx_out):
    if ctx_out:
        (ux_ref, uc_ref, gx_ref, gc_ref, cw_ref, cb_ref, wa_ref, ba_ref, wx_ref, bx_ref, lam_ref,
         ox_ref, oc_ref, a_s, b_s, p_s) = refs
    else:
        (ux_ref, uc_ref, gx_ref, cw_ref, cb_ref, wa_ref, ba_ref, wx_ref, bx_ref, lam_ref,
         ox_ref, a_s, b_s, p_s) = refs
    n_t = n_x + n_c
    clen = n_t // SCAN_CHUNKS
    cstride = clen + SCAN_PAD_ROWS

    def conv(z):
        z = z.astype(F32)
        n = z.shape[0]
        t = lax.broadcasted_iota(jnp.int32, (n, 1), 0)
        zm2 = jnp.where(t >= 2, pltpu.roll(z, 2, 0), 0.0)
        zm1 = jnp.where(t >= 1, pltpu.roll(z, 1, 0), 0.0)
        zp1 = jnp.where(t < n - 1, pltpu.roll(z, n - 1, 0), 0.0)
        return (zm2 * cw_ref[0:1, :] + zm1 * cw_ref[1:2, :] + z * cw_ref[2:3, :] + zp1 * cw_ref[3:4, :]
                + cb_ref[...])

    def pieces(t0, n):
        out, t = [], t0
        while t < t0 + n:
            ch = t // clen
            stop = min((ch + 1) * clen, t0 + n)
            out.append((t - t0, ch * cstride + (t - ch * clen), stop - t))
            t = stop
        return out

    def put(ref, d, t0, val):
        for off, row, ln in pieces(t0, val.shape[0]):
            ref[d, row:row + ln, :] = val[off:off + ln]

    def get(ref, d, t0, n):
        return jnp.concatenate([ref[d, row:row + ln, :] for _, row, ln in pieces(t0, n)], axis=0)

    def coeffs(u, d, t0):
        ub = u.astype(BF16)
        r = jax.nn.sigmoid(jnp.dot(ub, wa_ref[d, 0], preferred_element_type=F32) + ba_ref[d:d + 1, :])
        gi = jax.nn.sigmoid(jnp.dot(ub, wx_ref[d, 0], preferred_element_type=F32) + bx_ref[d:d + 1, :])
        log_a = (-RG_C) * r * jax.nn.softplus(-lam_ref[d:d + 1, :])
        a = jnp.exp(log_a)
        put(a_s, d, t0, a)
        put(b_s, d, t0, jnp.sqrt(-jnp.tanh(log_a) * (a * a + 1.0)) * (gi * u))

    u_c = conv(uc_ref[0])
    coeffs(u_c, 0, 0)
    coeffs(u_c, 1, n_x)
    u_x = conv(ux_ref[0])
    coeffs(u_x, 0, n_c)
    coeffs(u_x, 1, 0)

    def step(tau, carry):
        h_f, p_f, h_r, p_r = carry
        i_f = pl.ds(tau, SCAN_CHUNKS, stride=cstride)
        i_r = pl.ds(clen - 1 - tau, SCAN_CHUNKS, stride=cstride)
        a_f = a_s[0, i_f, :]
        a_r = a_s[1, i_r, :]
        h_f = a_f * h_f + b_s[0, i_f, :]
        h_r = a_r * h_r + b_s[1, i_r, :]
        p_f = a_f * p_f
        p_r = a_r * p_r
        b_s[0, i_f, :] = h_f
        b_s[1, i_r, :] = h_r
        p_s[0, i_f, :] = p_f
        p_s[1, i_r, :] = p_r
        return h_f, p_f, h_r, p_r

    zeros = jnp.zeros((SCAN_CHUNKS, RG_BW), F32)
    ones = jnp.ones((SCAN_CHUNKS, RG_BW), F32)
    h_f, p_f, h_r, p_r = lax.fori_loop(0, clen, step, (zeros, ones, zeros, ones), unroll=2)

    carry = jnp.zeros((1, RG_BW), F32)
    for ch in range(1, SCAN_CHUNKS):
        carry = p_f[ch - 1:ch, :] * carry + h_f[ch - 1:ch, :]
        rows = slice(ch * cstride, ch * cstride + clen)
        b_s[0, rows, :] = b_s[0, rows, :] + p_s[0, rows, :] * carry
    carry = jnp.zeros((1, RG_BW), F32)
    for ch in range(SCAN_CHUNKS - 2, -1, -1):
        carry = p_r[ch + 1:ch + 2, :] * carry + h_r[ch + 1:ch + 2, :]
        rows = slice(ch * cstride, ch * cstride + clen)
        b_s[1, rows, :] = b_s[1, rows, :] + p_s[1, rows, :] * carry

    gx = jax.nn.gelu(gx_ref[0].astype(F32), approximate=True)
    ox_ref[0] = ((get(b_s, 0, n_c, n_x) + get(b_s, 1, 0, n_x)) * gx).astype(BF16)
    if ctx_out:
        gc = jax.nn.gelu(gc_ref[0].astype(F32), approximate=True)
        oc_ref[0] = ((get(b_s, 0, 0, n_c) + get(b_s, 1, n_x, n_c)) * gc).astype(BF16)


def _rg_lru(zx, zc, conv_w, conv_b, w_a, b_a, w_x, b_x, lam, ctx_out):
    b, n_x, _ = zx.shape
    n_c = zc.shape[1]
    bw = RG_BW
    col = lambda base: (lambda bi, n: (bi, 0, base // bw + n))
    vec2 = pl.BlockSpec((2, bw), lambda bi, n: (0, n))
    wspec = pl.BlockSpec((2, 1, bw, bw), lambda bi, n: (0, n, 0, 0))
    in_specs = [pl.BlockSpec((1, n_x, bw), col(COL_RG_X)), pl.BlockSpec((1, n_c, bw), col(COL_RG_X)),
                pl.BlockSpec((1, n_x, bw), col(COL_RG_G))]
    args = [zx, zc, zx]
    if ctx_out:
        in_specs.append(pl.BlockSpec((1, n_c, bw), col(COL_RG_G)))
        args.append(zc)
    in_specs += [pl.BlockSpec((4, bw), lambda bi, n: (0, n)), pl.BlockSpec((1, bw), lambda bi, n: (0, n)),
                 wspec, vec2, wspec, vec2, vec2]
    args += [conv_w, conv_b.reshape(1, RG_WIDTH), w_a, b_a, w_x, b_x, lam]
    out_specs = [pl.BlockSpec((1, n_x, bw), lambda bi, n: (bi, 0, n))]
    out_shape = [jax.ShapeDtypeStruct((b, n_x, RG_WIDTH), BF16)]
    if ctx_out:
        out_specs.append(pl.BlockSpec((1, n_c, bw), lambda bi, n: (bi, 0, n)))
        out_shape.append(jax.ShapeDtypeStruct((b, n_c, RG_WIDTH), BF16))
    n_s = SCAN_CHUNKS * ((n_x + n_c) // SCAN_CHUNKS + SCAN_PAD_ROWS)
    outs = pl.pallas_call(
        functools.partial(_rg_kernel, n_x=n_x, n_c=n_c, ctx_out=ctx_out),
        grid=(b, RG_BLOCKS),
        in_specs=in_specs,
        out_specs=out_specs,
        out_shape=out_shape,
        scratch_shapes=[pltpu.VMEM((2, n_s, bw), F32), pltpu.VMEM((2, n_s, bw), F32), pltpu.VMEM((2, n_s, bw), F32)],
        compiler_params=_cparams("parallel", "parallel"),
        name="rg_lru",
    )(*args)
    return outs if ctx_out else (outs[0], None)


def _merge_kernel(ya_ref, yb_ref, yc_ref, ga_ref, gb_ref, gc_ref, wa_ref, wb_ref, wc_ref, o_ref):
    gate = lambda ref: jax.nn.sigmoid(ref[0].astype(F32))
    m = gate(ga_ref) * jnp.dot(ya_ref[0], wa_ref[...], preferred_element_type=F32)
    m = m + gate(gb_ref) * jnp.dot(yb_ref[0], wb_ref[...], preferred_element_type=F32)
    m = m + gate(gc_ref) * jnp.dot(yc_ref[0], wc_ref[...], preferred_element_type=F32)
    o_ref[0] = m.astype(BF16)


def _branch_merge(ya, yb, yc, z, w_branch, tm, tn):
    b, t, _ = ya.shape
    d = D_MODEL
    wa, wb, wc = w_branch[:512], w_branch[512:1024], w_branch[1024:]
    gate = lambda k: (lambda bi, i, j: (bi, i, (COL_GATE + k * d) // tn + j))
    return pl.pallas_call(
        _merge_kernel,
        grid=(b, t // tm, d // tn),
        in_specs=[
            pl.BlockSpec((1, tm, 512), lambda bi, i, j: (bi, i, 0)),
            pl.BlockSpec((1, tm, 512), lambda bi, i, j: (bi, i, 0)),
            pl.BlockSpec((1, tm, 1024), lambda bi, i, j: (bi, i, 0)),
            pl.BlockSpec((1, tm, tn), gate(0)),
            pl.BlockSpec((1, tm, tn), gate(1)),
            pl.BlockSpec((1, tm, tn), gate(2)),
            pl.BlockSpec((512, tn), lambda bi, i, j: (0, j)),
            pl.BlockSpec((512, tn), lambda bi, i, j: (0, j)),
            pl.BlockSpec((1024, tn), lambda bi, i, j: (0, j)),
        ],
        out_specs=pl.BlockSpec((1, tm, tn), lambda bi, i, j: (bi, i, j)),
        out_shape=jax.ShapeDtypeStruct((b, t, d), BF16),
        compiler_params=_cparams("parallel", "parallel", "arbitrary"),
        name="branch_merge",
    )(ya, yb, yc, z, z, z, wa, wb, wc)


def _resid_kernel(m_ref, w_ref, x_ref, gt_ref, o_ref):
    o_ref[0] = x_ref[0] + gt_ref[0] * jnp.dot(m_ref[0], w_ref[...], preferred_element_type=F32)


def _out_proj_residual(m, w_out, x, gt, tm, tn):
    b, t, d = x.shape
    return pl.pallas_call(
        _resid_kernel,
        grid=(b, t // tm, d // tn),
        in_specs=[
            pl.BlockSpec((1, tm, d), lambda bi, i, j: (bi, i, 0)),
            pl.BlockSpec((d, tn), lambda bi, i, j: (0, j)),
            pl.BlockSpec((1, tm, tn), lambda bi, i, j: (bi, i, j)),
            pl.BlockSpec((1, 1, tn), lambda bi, i, j: (bi, 0, j)),
        ],
        out_specs=pl.BlockSpec((1, tm, tn), lambda bi, i, j: (bi, i, j)),
        out_shape=jax.ShapeDtypeStruct((b, t, d), F32),
        compiler_params=_cparams("parallel", "parallel", "arbitrary"),
        name="out_proj_residual",
    )(m, w_out, x, gt)


def _router_kernel(*refs, starts):
    n_s = len(starts) - 1
    g_ref, wh_ref, wl_ref, br_ref, h_ref, eid_ref, wt_ref = refs[3 * n_s:]
    i = pl.program_id(0)
    for k in range(n_s):
        x_ref, sc_ref, sh_ref = refs[3 * k:3 * k + 3]

        @pl.when((i >= starts[k]) & (i < starts[k + 1]))
        def _(x_ref=x_ref, sc_ref=sc_ref, sh_ref=sh_ref):
            _route_rows(x_ref, sc_ref, sh_ref, g_ref, wh_ref, wl_ref, br_ref, h_ref, eid_ref, wt_ref)


def _route_rows(x_ref, sc_ref, sh_ref, g_ref, wh_ref, wl_ref, br_ref, h_ref, eid_ref, wt_ref):
    h = _rms(x_ref[0], g_ref[...]) * (1.0 + sc_ref[0]) + sh_ref[0]
    h_ref[...] = h
    hh = h.astype(BF16)
    hl = (h - hh.astype(F32)).astype(BF16)
    logits = (jnp.dot(hh, wh_ref[...], preferred_element_type=F32)
              + jnp.dot(hl, wh_ref[...], preferred_element_type=F32)
              + jnp.dot(hh, wl_ref[...], preferred_element_type=F32)) + br_ref[...]
    lane = lax.broadcasted_iota(jnp.int32, logits.shape, 1)
    lane_f = lane.astype(F32)

    def first_argmax(v, valid):
        vm = jnp.where(valid, v, -jnp.inf)
        mx = jnp.max(vm, axis=-1, keepdims=True)
        idx = jnp.min(jnp.where(valid & (vm == mx), lane_f, 1e9), axis=-1, keepdims=True)
        return mx, idx.astype(jnp.int32)

    is_g = lane < N_GROUPS
    gmax, grp = first_argmax(logits, is_g)
    p_grp = 1.0 / jnp.sum(jnp.where(is_g, jnp.exp(logits - gmax), 0.0), axis=-1, keepdims=True)
    e_lo = N_GROUPS + grp * EXPERTS_PER_GROUP
    in_grp = (lane >= e_lo) & (lane < e_lo + EXPERTS_PER_GROUP)
    v0, i0 = first_argmax(logits, in_grp)
    v1, i1 = first_argmax(logits, in_grp & (lane != i0))
    e1 = jnp.exp(v1 - v0)
    w0 = p_grp / (1.0 + e1)
    w1 = p_grp * e1 / (1.0 + e1)
    eid_ref[...] = jnp.where(lane == 0, i0 - N_GROUPS, jnp.where(lane == 1, i1 - N_GROUPS, 0))
    wt_ref[...] = jnp.where(lane == 0, w0, jnp.where(lane == 1, w1, 0.0))


def _router(xs, scs, shs, g, wr_hi, wr_lo, br, tm):
    d = xs[0].shape[-1]
    starts = [0]
    in_specs, args = [], []
    for x, sc, sh in zip(xs, scs, shs):
        b, t, _ = x.shape
        n_i = t // tm
        s0 = starts[-1]
        starts.append(s0 + b * n_i)

        def local(i, s0=s0, n=b * n_i):
            return jnp.clip(i - s0, 0, n - 1)

        in_specs += [pl.BlockSpec((1, tm, d), lambda i, f=local, n_i=n_i: (f(i) // n_i, f(i) % n_i, 0)),
                     pl.BlockSpec((1, 1, d), lambda i, f=local, n_i=n_i: (f(i) // n_i, 0, 0)),
                     pl.BlockSpec((1, 1, d), lambda i, f=local, n_i=n_i: (f(i) // n_i, 0, 0))]
        args += [x, sc, sh]
    n_all = starts[-1] * tm
    in_specs += [pl.BlockSpec((1, d), lambda i: (0, 0)), pl.BlockSpec((d, 128), lambda i: (0, 0)),
                 pl.BlockSpec((d, 128), lambda i: (0, 0)), pl.BlockSpec((1, 128), lambda i: (0, 0))]
    args += [g.reshape(1, d), wr_hi, wr_lo, br]
    flat = lambda width: pl.BlockSpec((tm, width), lambda i: (i, 0))
    return pl.pallas_call(
        functools.partial(_router_kernel, starts=tuple(starts)),
        grid=(starts[-1],),
        in_specs=in_specs,
        out_specs=[flat(d), flat(128), flat(128)],
        out_shape=[jax.ShapeDtypeStruct((n_all, d), F32), jax.ShapeDtypeStruct((n_all, 128), jnp.int32),
                   jax.ShapeDtypeStruct((n_all, 128), F32)],
        compiler_params=_cparams("parallel"),
        name="moe_router",
    )(*args)


def _row_copy(src_hbm, dst_vmem, sem, src_row, dst_row):
    return pltpu.make_async_copy(src_hbm.at[pl.ds(src_row, 1)], dst_vmem.at[pl.ds(dst_row, 1)], sem)


def _dispatch_kernel(slot_ref, pad_ref, x_ref, o_hbm, stage0, stage1, zrow, sem):
    tm = x_ref.shape[0]
    n_fill = pad_ref.shape[2]
    i = pl.program_id(0)
    n = pl.num_programs(0)
    stages = (stage0, stage1)

    def put(src, src_row, dst_row, s):
        return pltpu.make_async_copy(src.at[pl.ds(src_row, 1)], o_hbm.at[pl.ds(dst_row, 1)], s)

    def wait_step(p):
        def wait(r, c):
            put(stages[p], 0, 0, sem.at[p]).wait()
            return c
        lax.fori_loop(0, 2 * tm + n_fill, wait, 0, unroll=8)

    @pl.when(i == 0)
    def _():
        zrow[...] = jnp.zeros(zrow.shape, zrow.dtype)

    def block(cur):
        stage = stages[cur]
        stage[...] = x_ref[...]
        for r in range(tm):
            put(stage, r, slot_ref[0, 0, 2 * r], sem.at[cur]).start(priority=0)
            put(stage, r, slot_ref[0, 0, 2 * r + 1], sem.at[cur]).start(priority=1)
        for r in range(n_fill):
            put(zrow, 0, pad_ref[0, 0, r], sem.at[cur]).start(priority=r % 2)

        @pl.when(i > 0)
        def _():
            wait_step(1 - cur)

        @pl.when(i == n - 1)
        def _():
            wait_step(cur)

    for parity in range(2):
        @pl.when(i % 2 == parity)
        def _(parity=parity):
            block(parity)


def _dispatch(slot, pad_slots, x_rows, n_out, tm):
    n_tok, d = x_rows.shape
    n = n_tok // tm
    n_fill = pad_slots.shape[0] // n
    return pl.pallas_call(
        _dispatch_kernel,
        grid=(n,),
        in_specs=[pl.BlockSpec((1, 1, 2 * tm), lambda i: (i, 0, 0), memory_space=pltpu.SMEM),
                  pl.BlockSpec((1, 1, n_fill), lambda i: (i, 0, 0), memory_space=pltpu.SMEM),
                  pl.BlockSpec((tm, d), lambda i: (i, 0))],
        out_specs=pl.BlockSpec(memory_space=pl.ANY),
        out_shape=jax.ShapeDtypeStruct((n_out, d), x_rows.dtype),
        scratch_shapes=[pltpu.VMEM((tm, d), x_rows.dtype), pltpu.VMEM((tm, d), x_rows.dtype),
                        pltpu.VMEM((8, d), x_rows.dtype), pltpu.SemaphoreType.DMA((2,))],
        compiler_params=_cparams("arbitrary"),
        name="moe_dispatch",
    )(slot.reshape(n, 1, 2 * tm), pad_slots.reshape(n, 1, n_fill), x_rows)


def _expert_kernel(blk_e_ref, x_ref, w1_ref, w3_ref, w2_ref, o_ref):
    del blk_e_ref
    xb = x_ref[...].astype(BF16)
    h1 = jnp.dot(xb, w1_ref[0], preferred_element_type=F32)
    h3 = jnp.dot(xb, w3_ref[0], preferred_element_type=F32)
    hid = (h1 * jax.nn.sigmoid(h1) * h3).astype(BF16)
    o_ref[...] = jnp.dot(hid, w2_ref[0], preferred_element_type=F32)


def _expert_blocks(blk_e, x_sorted, w1, w3, w2):
    n_blk = blk_e.shape[0]
    tm = MOE_TM
    d = x_sorted.shape[1]
    grid_spec = pltpu.PrefetchScalarGridSpec(
        num_scalar_prefetch=1,
        grid=(n_blk,),
        in_specs=[
            pl.BlockSpec((tm, d), lambda i, e: (i, 0)),
            pl.BlockSpec((1, d, D_EXPERT), lambda i, e: (e[i], 0, 0)),
            pl.BlockSpec((1, d, D_EXPERT), lambda i, e: (e[i], 0, 0)),
            pl.BlockSpec((1, D_EXPERT, d), lambda i, e: (e[i], 0, 0)),
        ],
        out_specs=pl.BlockSpec((tm, d), lambda i, e: (i, 0)),
    )
    return pl.pallas_call(
        _expert_kernel,
        grid_spec=grid_spec,
        out_shape=jax.ShapeDtypeStruct((n_blk * tm, d), F32),
        compiler_params=_cparams("arbitrary"),
        name="moe_experts",
    )(blk_e, x_sorted, w1, w3, w2)


def _combine_kernel(slot_ref, slot_next_ref, y_hbm, x_ref, gt_ref, wt_ref, o_ref, a0, b0, a1, b1, sem):
    tm = a0.shape[0]
    i = pl.program_id(0)
    n = pl.num_programs(0)
    bufs = ((a0, b0), (a1, b1))

    def wait_rows(p):
        def wait(r, c):
            _row_copy(y_hbm, bufs[p][0], sem.at[p], 0, r).wait()
            _row_copy(y_hbm, bufs[p][1], sem.at[p], 0, r).wait()
            return c
        lax.fori_loop(0, tm, wait, 0, unroll=8)

    @pl.when(i == 0)
    def _():
        def issue(r, c):
            _row_copy(y_hbm, a0, sem.at[0], slot_ref[0, 0, 2 * r], r).start()
            _row_copy(y_hbm, b0, sem.at[0], slot_ref[0, 0, 2 * r + 1], r).start()
            return c
        lax.fori_loop(0, tm, issue, 0, unroll=8)

    def block(cur):
        wait_rows(cur)
        nxt_a, nxt_b = bufs[1 - cur]
        for r in range(tm):
            _row_copy(y_hbm, nxt_a, sem.at[1 - cur], slot_next_ref[0, 0, 2 * r], r).start(priority=0)
            _row_copy(y_hbm, nxt_b, sem.at[1 - cur], slot_next_ref[0, 0, 2 * r + 1], r).start(priority=1)
        wt = wt_ref[...]
        f = bufs[cur][0][...] * wt[:, 0:1] + bufs[cur][1][...] * wt[:, 1:2]
        o_ref[...] = x_ref[...] + gt_ref[0] * f

    for parity in range(2):
        @pl.when(i % 2 == parity)
        def _(parity=parity):
            block(parity)

        @pl.when((i == n - 1) & (i % 2 == parity))
        def _(parity=parity):
            wait_rows(1 - parity)


def _combine(slots, y, x, gt, wt, tok0, tm):
    b, t, d = x.shape
    n_i = t // tm
    n = b * n_i
    x2 = x.reshape(b * t, d)
    blk0 = tok0 // tm
    slots3 = slots.reshape(-1, 1, 2 * tm)
    out = pl.pallas_call(
        _combine_kernel,
        grid=(n,),
        in_specs=[
            pl.BlockSpec((1, 1, 2 * tm), lambda i: (blk0 + i, 0, 0), memory_space=pltpu.SMEM),
            pl.BlockSpec((1, 1, 2 * tm), lambda i: (blk0 + jnp.minimum(i + 1, n - 1), 0, 0),
                         memory_space=pltpu.SMEM),
            pl.BlockSpec(memory_space=pl.ANY),
            pl.BlockSpec((tm, d), lambda i: (i, 0)),
            pl.BlockSpec((1, 1, d), lambda i: (i // n_i, 0, 0)),
            pl.BlockSpec((tm, 128), lambda i: (blk0 + i, 0)),
        ],
        out_specs=pl.BlockSpec((tm, d), lambda i: (i, 0)),
        out_shape=jax.ShapeDtypeStruct((b * t, d), F32),
        scratch_shapes=[pltpu.VMEM((tm, d), F32)] * 4 + [pltpu.SemaphoreType.DMA((2,))],
        compiler_params=_cparams("arbitrary"),
        name="moe_combine",
    )(slots3, slots3, y, x2, gt, wt)
    return out.reshape(b, t, d)


def _routing_tables(eid, n_steps):
    m = eid.shape[0]
    tm = MOE_TM
    i32 = jnp.int32
    iota = jnp.arange(m, dtype=i32)
    se, order = lax.sort_key_val(eid, iota)
    onehot = (se[:, None] == jnp.arange(N_EXPERTS, dtype=i32)[None, :]).astype(i32)
    counts = jnp.sum(onehot, axis=0)
    starts = jnp.cumsum(counts) - counts
    pcounts = (counts + tm - 1) // tm * tm
    pends = jnp.cumsum(pcounts)
    pstarts = pends - pcounts
    dest = iota + jnp.sum(onehot * (pstarts - starts)[None, :], axis=1)
    _, slot = lax.sort_key_val(order, dest)
    n_blk = (m + N_EXPERTS * (tm - 1) + tm - 1) // tm
    n_pad = n_blk * tm
    blk_start = jnp.arange(n_blk, dtype=i32) * tm
    blk_e = jnp.minimum(jnp.sum((pends[None, :] <= blk_start[:, None]).astype(i32), axis=1), N_EXPERTS - 1)
    n_empty = n_pad - m
    n_fill = -(-n_empty // n_steps)
    n_fill = (n_fill + 7) // 8 * 8
    seg_len = jnp.concatenate([pcounts - counts, (n_pad - pends[-1:])])
    seg_first = jnp.concatenate([pstarts + counts, pends[-1:]])
    seg_end = jnp.cumsum(seg_len)
    seg_begin = seg_end - seg_len
    j = jnp.arange(n_empty, dtype=i32)
    in_seg = ((seg_begin[None, :] <= j[:, None]) & (j[:, None] < seg_end[None, :])).astype(i32)
    empty = j + jnp.sum(in_seg * (seg_first - seg_begin)[None, :], axis=1)
    n_spare = n_steps * n_fill - n_empty
    fill = jnp.concatenate([empty, n_pad + jnp.arange(n_spare, dtype=i32)])
    return blk_e.astype(i32), slot, fill, n_pad + n_spare


def _moe(xs, scs, shs, gts, norm_g, wr_hi, wr_lo, br, w1, w3, w2):
    h_all, eid_all, wt_all = _router(xs, scs, shs, norm_g, wr_hi, wr_lo, br, 512)
    n_steps = h_all.shape[0] // MOE_DISPATCH_TOKENS
    blk_e, slot, fill, n_rows = _routing_tables(eid_all[:, :TOP_K].reshape(-1), n_steps)
    y = _expert_blocks(blk_e, _dispatch(slot, fill, h_all, n_rows, MOE_DISPATCH_TOKENS), w1, w3, w2)
    outs = []
    tok0 = 0
    for x, gt in zip(xs, gts):
        outs.append(_combine(slot, y, x, gt, wt_all, tok0, tm=256))
        tok0 += x.shape[0] * x.shape[1]
    return outs


def kernel(x, c, ctx, c_ctx, w_mod, b_mod, norm1_g, norm2_g, w_in, na_q_g, na_k_g, na_rpb, df_q_g, df_k_g, df_lam, df_sub_g, rg_conv_w, rg_conv_b, rg_w_a, rg_b_a, rg_w_x, rg_b_x, rg_lam, w_branch, w_out, w_router_g, b_router_g, w_router_e, b_router_e, w1, w3, w2):
    B, S, D = x.shape
    C = ctx.shape[1]
    n_rows = S // GRID_W
    rope_cos, rope_sin = _rope_tables(S)
    na_lo, na_hi = _na_bias_tables(na_rpb)

    pad = (-(B + 1)) % 8
    rows = jnp.concatenate([c, c_ctx[None, :], jnp.zeros((pad, D), F32)], axis=0)
    mod_all = _modulation(rows, w_mod, b_mod)

    xc = ctx.reshape(1, B * C, D)
    for l in range(DEPTH):
        need_ctx = l < DEPTH - 1
        lam_init = 0.8 - 0.6 * float(np.exp(-0.3 * l))
        mod = mod_all[l]
        sh1, sc1, gt1, sh2, sc2, gt2 = [mod[:B, k * D:(k + 1) * D][:, None, :] for k in range(6)]
        csh1, csc1, cgt1, csh2, csc2, cgt2 = [mod[B:B + 1, k * D:(k + 1) * D][:, None, :] for k in range(6)]

        w_in_l = w_in[l].astype(BF16)
        zx = _norm_mod_matmul(x, norm1_g[l], sc1, sh1, w_in_l, tm=1024, tn=1024)
        n_cc = IN_COLS if need_ctx else KV_COLS
        zc = _norm_mod_matmul(xc, norm1_g[l], csc1, csh1, w_in_l[:, :n_cc], tm=1024, tn=1024).reshape(B, C, n_cc)

        y_a = _na_attention(zx, zc, na_lo[l], na_hi[l], na_q_g[l], na_k_g[l])

        lp = df_lam[l]
        lam = jnp.exp(jnp.sum(lp[0] * lp[1])) - jnp.exp(jnp.sum(lp[2] * lp[3])) + lam_init
        y_b = _df_attention(lam, zx, zx, zc, rope_cos, rope_sin, df_q_g[l], df_k_g[l], df_sub_g[l], 1.0 - lam_init)

        y_c, y_cc = _rg_lru(zx, zc, rg_conv_w[l], rg_conv_b[l], rg_w_a[l].astype(BF16), rg_b_a[l],
                            rg_w_x[l].astype(BF16), rg_b_x[l], rg_lam[l], need_ctx)

        wb_l = w_branch[l].astype(BF16)
        wo_l = w_out[l].astype(BF16)
        m_x = _branch_merge(y_a, y_b, y_c, zx, wb_l, tm=1024, tn=1024)
        x = _out_proj_residual(m_x, wo_l, x, gt1, tm=1024, tn=1024)
        if need_ctx:
            y_ac = _na_ctx_attention(zc, na_q_g[l], na_k_g[l])
            y_bc = _df_attention(lam, zc, None, zc, None, None, df_q_g[l], df_k_g[l], df_sub_g[l], 1.0 - lam_init)
            m_c = _branch_merge(y_ac, y_bc, y_cc, zc, wb_l, tm=C, tn=1024)
            xc = _out_proj_residual(m_c.reshape(1, B * C, D), wo_l, xc, cgt1, tm=1024, tn=1024)

        wr = jnp.concatenate([w_router_g[l], w_router_e[l],
                              jnp.zeros((D, 128 - N_GROUPS - N_EXPERTS), F32)], axis=1)
        wr_hi = wr.astype(BF16)
        wr_lo = (wr - wr_hi.astype(F32)).astype(BF16)
        br = jnp.concatenate([b_router_g[l], b_router_e[l],
                              jnp.zeros((128 - N_GROUPS - N_EXPERTS,), F32)]).reshape(1, 128)
        w1_l, w3_l, w2_l = w1[l].astype(BF16), w3[l].astype(BF16), w2[l].astype(BF16)
        if need_ctx:
            xc, x = _moe([xc, x], [csc2, sc2], [csh2, sh2], [cgt2, gt2], norm2_g[l],
                         wr_hi, wr_lo, br, w1_l, w3_l, w2_l)
        else:
            (x,) = _moe([x], [sc2], [sh2], [gt2], norm2_g[l], wr_hi, wr_lo, br, w1_l, w3_l, w2_l)
    return x
```

```python
import functools

import jax
import jax.numpy as jnp
import numpy as np
from jax import lax
from jax.experimental import pallas as pl
from jax.experimental.pallas import tpu as pltpu

F32 = jnp.float32
BF16 = jnp.bfloat16

D_MODEL = 2048
DEPTH = 4
GRID_W = 64
HEAD_DIM = 128
N_HEADS = 4
NA_WIN_R = 8
NA_WIN_C = 16
NA_QCB = 16
NA_KCB = 32
DF_DIM = 64
RG_WIDTH = 1024
RG_BLOCKS = 8
RG_BW = 128
RG_C = 8.0
N_GROUPS = 4
EXPERTS_PER_GROUP = 8
N_EXPERTS = 32
TOP_K = 2
D_EXPERT = 512
ROPE_BASE = 10000.0
EPS = 1e-6
NEG = -1e30
LOG2_E = 1.4426950408889634

COL_NA_K, COL_NA_V, COL_DF_K, COL_DF_V, COL_RG_X = 0, 512, 1024, 1536, 2048
COL_NA_Q, COL_DF_Q, COL_RG_G, COL_GATE = 3072, 3584, 4096, 5120
KV_COLS = 3072
MIX_COLS = 5120
IN_COLS = MIX_COLS + 3 * D_MODEL

VMEM_LIMIT_BYTES = 56 * 1024 * 1024

NA_ROWS_PER_STEP = 4
NA_KEY_ROWS = NA_ROWS_PER_STEP + NA_WIN_R
DF_QBLK = 1024
DF_KCHUNK = 768
MOE_TM = 256
MOE_DISPATCH_TOKENS = 256
SCAN_CHUNKS = 8
SCAN_PAD_ROWS = 8


def _cparams(*sem):
    return pltpu.CompilerParams(dimension_semantics=sem, vmem_limit_bytes=VMEM_LIMIT_BYTES)


def _rms(x, g):
    x = x.astype(F32)
    return x * lax.rsqrt(jnp.mean(x * x, axis=-1, keepdims=True) + EPS) * g


def _mod_kernel(s_ref, w_ref, b_ref, o_ref):
    s = s_ref[...]
    a = (s * jax.nn.sigmoid(s)).astype(BF16)
    o_ref[0] = jnp.dot(a, w_ref[0].astype(BF16), preferred_element_type=F32) + b_ref[0]


def _modulation(rows, w_mod, b_mod, tn=1024):
    n_l, d, n = w_mod.shape
    r = rows.shape[0]
    return pl.pallas_call(
        _mod_kernel,
        grid=(n_l, n // tn),
        in_specs=[
            pl.BlockSpec((r, d), lambda l, j: (0, 0)),
            pl.BlockSpec((1, d, tn), lambda l, j: (l, 0, j)),
            pl.BlockSpec((1, 1, tn), lambda l, j: (l, 0, j)),
        ],
        out_specs=pl.BlockSpec((1, r, tn), lambda l, j: (l, 0, j)),
        out_shape=jax.ShapeDtypeStruct((n_l, r, n), F32),
        compiler_params=_cparams("parallel", "parallel"),
        name="modulation",
    )(rows, w_mod, b_mod.reshape(n_l, 1, n))


def _nmm_kernel(x_ref, g_ref, sc_ref, sh_ref, w_ref, o_ref, h_ref):
    @pl.when(pl.program_id(2) == 0)
    def _():
        h = _rms(x_ref[0], g_ref[...]) * (1.0 + sc_ref[0]) + sh_ref[0]
        h_ref[...] = h.astype(BF16)

    o_ref[0] = jnp.dot(h_ref[...], w_ref[...], preferred_element_type=F32).astype(o_ref.dtype)


def _norm_mod_matmul(x, g, sc, sh, w, layer, n, tm, tn):
    b, t, d = x.shape
    return pl.pallas_call(
        _nmm_kernel,
        grid=(b, t // tm, n // tn),
        in_specs=[
            pl.BlockSpec((1, tm, d), lambda bi, i, j: (bi, i, 0)),
            pl.BlockSpec((1, d), lambda bi, i, j: (0, 0)),
            pl.BlockSpec((1, 1, d), lambda bi, i, j: (bi, 0, 0)),
            pl.BlockSpec((1, 1, d), lambda bi, i, j: (bi, 0, 0)),
            pl.BlockSpec((None, d, tn), lambda bi, i, j: (layer, 0, j)),
        ],
        out_specs=pl.BlockSpec((1, tm, tn), lambda bi, i, j: (bi, i, j)),
        out_shape=jax.ShapeDtypeStruct((b, t, n), BF16),
        scratch_shapes=[pltpu.VMEM((tm, d), BF16)],
        compiler_params=_cparams("parallel", "parallel", "arbitrary"),
        name="norm_mod_proj",
    )(x, g.reshape(1, d), sc, sh, w)


def _na_window_start(r0):
    lower = jnp.clip(r0 - NA_WIN_R // 2, 0, (2048 // GRID_W) - NA_WIN_R)
    return jnp.minimum(lower, (2048 // GRID_W) - NA_KEY_ROWS)


def _na_bias_tables(rpb):
    w = GRID_W
    qc = np.arange(w)
    kc = np.arange(w)
    qwin = np.clip(qc - NA_WIN_C // 2, 0, w - NA_WIN_C)
    kcol_start = np.clip((qc // NA_QCB) * NA_QCB - (NA_KCB - NA_QCB) // 2, 0, w - NA_KCB)
    col_ok = (kc[None, :] >= qwin[:, None]) & (kc[None, :] < qwin[:, None] + NA_WIN_C)
    col_ok &= (kc[None, :] >= kcol_start[:, None]) & (kc[None, :] < kcol_start[:, None] + NA_KCB)
    edge = w - NA_WIN_C
    lead = rpb.shape[:-1]
    ext = jnp.concatenate([jnp.repeat(rpb[..., :1], edge, axis=-1), rpb, jnp.repeat(rpb[..., -1:], edge, axis=-1),
                           jnp.zeros(lead + (1,), F32)], axis=-1)
    skew = jnp.tile(ext, (1,) * len(lead) + (w,))[..., :w * (2 * w - 1)].reshape(lead + (w, 2 * w - 1))
    toep = jnp.where(col_ok, skew[..., w - 1:], NEG)
    toep = jnp.concatenate([toep, jnp.full(toep.shape[:-3] + (1, w, w), NEG, F32)], axis=-3)
    zeros = jnp.zeros_like(toep)
    return jnp.concatenate([toep, zeros], axis=-1), jnp.concatenate([zeros, toep], axis=-1)


def _na_bias_blocks(n_rows):
    rb, kw = NA_ROWS_PER_STEP, NA_KEY_ROWS
    rows = np.arange(n_rows)
    row_start = np.clip(rows - NA_WIN_R // 2, 0, n_rows - NA_WIN_R)
    n_steps = n_rows // rb
    blk = np.full((n_steps, rb, kw), 2 * NA_WIN_R - 1, np.int32)
    for s in range(n_steps):
        ws = min(int(np.clip(s * rb - NA_WIN_R // 2, 0, n_rows - NA_WIN_R)), n_rows - kw)
        for j in range(rb):
            r = s * rb + j
            for i in range(kw):
                if row_start[r] <= ws + i < row_start[r] + NA_WIN_R:
                    blk[s, j, i] = ws + i - r + (NA_WIN_R - 1)
    return blk.reshape(-1)


def _na_kernel(blk_ref, q_ref, k_ref, v_ref, kc_ref, vc_ref, lo_ref, hi_ref, gq_ref, gk_ref, o_ref,
               kn, vn, knc, vnc):
    step = pl.program_id(1)

    def bias(h):
        rows = []
        for j in range(NA_ROWS_PER_STEP):
            base = (step * NA_ROWS_PER_STEP + j) * NA_KEY_ROWS
            rows.append(jnp.concatenate(
                [lo_ref[h, blk_ref[base + i]] + hi_ref[h, blk_ref[base + i + 1]] for i in range(0, NA_KEY_ROWS, 2)],
                axis=-1))
        return jnp.concatenate(rows, axis=0)

    @pl.when(step == 0)
    def _():
        for h in range(N_HEADS):
            sl = slice(h * HEAD_DIM, (h + 1) * HEAD_DIM)
            kn[:, sl] = _rms(k_ref[0, :, sl], gk_ref[...]).astype(BF16)
            knc[:, sl] = _rms(kc_ref[0, :, sl], gk_ref[...]).astype(BF16)
        vn[...] = v_ref[0].astype(BF16)
        vnc[...] = vc_ref[0].astype(BF16)

    ws = _na_window_start(step * NA_ROWS_PER_STEP)
    start = pl.multiple_of(ws * GRID_W, GRID_W)
    nk = NA_KEY_ROWS * GRID_W
    scale = HEAD_DIM ** -0.5
    nt = (((1,), (1,)), ((), ()))
    for h in range(N_HEADS):
        sl = slice(h * HEAD_DIM, (h + 1) * HEAD_DIM)
        qn = (_rms(q_ref[0, :, sl], gq_ref[...]) * scale).astype(BF16)
        s_loc = lax.dot_general(qn, kn[pl.ds(start, nk), sl], nt, preferred_element_type=F32) + bias(h)
        s_ctx = lax.dot_general(qn, knc[:, sl], nt, preferred_element_type=F32)
        m = jnp.maximum(jnp.max(s_loc, axis=-1, keepdims=True), jnp.max(s_ctx, axis=-1, keepdims=True))
        p_loc = jnp.exp(s_loc - m)
        p_ctx = jnp.exp(s_ctx - m)
        l = jnp.sum(p_loc, axis=-1, keepdims=True) + jnp.sum(p_ctx, axis=-1, keepdims=True)
        o = jnp.dot(p_loc.astype(BF16), vn[pl.ds(start, nk), sl], preferred_element_type=F32)
        o = o + jnp.dot(p_ctx.astype(BF16), vnc[:, sl], preferred_element_type=F32)
        o_ref[0, :, sl] = (o * (1.0 / l)).astype(BF16)


def _na_attention(zx, zc, bias_lo, bias_hi, gq, gk):
    b, s, _ = zx.shape
    c = zc.shape[1]
    w = N_HEADS * HEAD_DIM
    tq = NA_ROWS_PER_STEP * GRID_W
    blk = jnp.asarray(_na_bias_blocks(s // GRID_W))
    table = pl.BlockSpec(bias_lo.shape, lambda bi, i, blk: (0, 0, 0, 0))
    grid_spec = pltpu.PrefetchScalarGridSpec(
        num_scalar_prefetch=1,
        grid=(b, s // tq),
        in_specs=[
            pl.BlockSpec((1, tq, w), lambda bi, i, blk: (bi, i, COL_NA_Q // w)),
            pl.BlockSpec((1, s, w), lambda bi, i, blk: (bi, 0, COL_NA_K // w)),
            pl.BlockSpec((1, s, w), lambda bi, i, blk: (bi, 0, COL_NA_V // w)),
            pl.BlockSpec((1, c, w), lambda bi, i, blk: (bi, 0, COL_NA_K // w)),
            pl.BlockSpec((1, c, w), lambda bi, i, blk: (bi, 0, COL_NA_V // w)),
            table, table,
            pl.BlockSpec((1, HEAD_DIM), lambda bi, i, blk: (0, 0)),
            pl.BlockSpec((1, HEAD_DIM), lambda bi, i, blk: (0, 0)),
        ],
        out_specs=pl.BlockSpec((1, tq, w), lambda bi, i, blk: (bi, i, 0)),
        scratch_shapes=[pltpu.VMEM((s, w), BF16), pltpu.VMEM((s, w), BF16),
                        pltpu.VMEM((c, w), BF16), pltpu.VMEM((c, w), BF16)],
    )
    return pl.pallas_call(
        _na_kernel,
        grid_spec=grid_spec,
        out_shape=jax.ShapeDtypeStruct((b, s, w), BF16),
        compiler_params=_cparams("parallel", "arbitrary"),
        name="na_attention",
    )(blk, zx, zx, zx, zc, zc, bias_lo, bias_hi, gq.reshape(1, HEAD_DIM), gk.reshape(1, HEAD_DIM))


def _na_ctx_kernel(q_ref, k_ref, v_ref, gq_ref, gk_ref, o_ref):
    scale = HEAD_DIM ** -0.5
    nt = (((1,), (1,)), ((), ()))
    for h in range(N_HEADS):
        sl = slice(h * HEAD_DIM, (h + 1) * HEAD_DIM)
        qn = (_rms(q_ref[0, :, sl], gq_ref[...]) * scale).astype(BF16)
        kn = _rms(k_ref[0, :, sl], gk_ref[...]).astype(BF16)
        s = lax.dot_general(qn, kn, nt, preferred_element_type=F32)
        p = jnp.exp(s - jnp.max(s, axis=-1, keepdims=True))
        l = jnp.sum(p, axis=-1, keepdims=True)
        o = jnp.dot(p.astype(BF16), v_ref[0, :, sl].astype(BF16), preferred_element_type=F32)
        o_ref[0, :, sl] = (o * (1.0 / l)).astype(BF16)


def _na_ctx_attention(zc, gq, gk):
    b, c, _ = zc.shape
    w = N_HEADS * HEAD_DIM
    return pl.pallas_call(
        _na_ctx_kernel,
        grid=(b,),
        in_specs=[
            pl.BlockSpec((1, c, w), lambda bi: (bi, 0, COL_NA_Q // w)),
            pl.BlockSpec((1, c, w), lambda bi: (bi, 0, COL_NA_K // w)),
            pl.BlockSpec((1, c, w), lambda bi: (bi, 0, COL_NA_V // w)),
            pl.BlockSpec((1, HEAD_DIM), lambda bi: (0, 0)),
            pl.BlockSpec((1, HEAD_DIM), lambda bi: (0, 0)),
        ],
        out_specs=pl.BlockSpec((1, c, w), lambda bi: (bi, 0, 0)),
        out_shape=jax.ShapeDtypeStruct((b, c, w), BF16),
        compiler_params=_cparams("parallel"),
        name="na_ctx_attention",
    )(zc, zc, zc, gq.reshape(1, HEAD_DIM), gk.reshape(1, HEAD_DIM))


def _rope_tables(n_tok):
    t = np.arange(n_tok)
    pos = np.stack([t // GRID_W, t % GRID_W], axis=0).astype(np.float32)
    n_freq = DF_DIM // 4
    inv = (np.float32(ROPE_BASE) ** (-np.arange(n_freq, dtype=np.float32) / n_freq)).astype(np.float32)
    ang = jnp.asarray(pos[:, :, None] * inv)
    cos, sin = jnp.cos(ang), jnp.sin(ang)
    cos64 = jnp.concatenate([cos[0], cos[0], cos[1], cos[1]], axis=-1)
    sin64 = jnp.concatenate([-sin[0], sin[0], -sin[1], sin[1]], axis=-1)
    return jnp.concatenate([cos64, cos64], axis=-1), jnp.concatenate([sin64, sin64], axis=-1)


def _df_kernel(lam_ref, q_ref, *refs, n_x, n_c, rope, out_scale):
    if n_x:
        kx_ref, vx_ref, kc_ref, vc_ref, cq_ref, sq_ref, ck_ref, sk_ref, gq_ref, gk_ref, gs_ref, o_ref, kn, vn = refs
    else:
        kc_ref, vc_ref, gq_ref, gk_ref, gs_ref, o_ref, kn, vn = refs
    lane = lax.broadcasted_iota(jnp.int32, (1, 2 * DF_DIM), 1)
    lo = lane < DF_DIM
    first = (lane % (DF_DIM // 2)) < (DF_DIM // 4)

    def norm64(x, g):
        x = x.astype(F32)
        x2 = x * x
        s0 = jnp.sum(jnp.where(lo, x2, 0.0), axis=-1, keepdims=True)
        s1 = jnp.sum(jnp.where(lo, 0.0, x2), axis=-1, keepdims=True)
        ms = jnp.where(lo, s0, s1) * (1.0 / DF_DIM)
        return x * lax.rsqrt(ms + EPS) * g

    def rot(x, cos, sin):
        partner = jnp.where(first, pltpu.roll(x, 2 * DF_DIM - DF_DIM // 4, 1), pltpu.roll(x, DF_DIM // 4, 1))
        return x * cos + partner * sin

    @pl.when(pl.program_id(2) == 0)
    def _():
        if n_x:
            kx = rot(norm64(kx_ref[0], gk_ref[...]), ck_ref[...], sk_ref[...])
            kn[0:n_x, :] = kx.astype(BF16)
            vn[0:n_x, :] = vx_ref[0].astype(BF16)
        kn[n_x:n_x + n_c, :] = norm64(kc_ref[0], gk_ref[...]).astype(BF16)
        vn[n_x:n_x + n_c, :] = vc_ref[0].astype(BF16)

    q = norm64(q_ref[0], gq_ref[...])
    if rope:
        q = rot(q, cq_ref[...], sq_ref[...])
    q = q * (DF_DIM ** -0.5 * LOG2_E)
    nt = (((1,), (1,)), ((), ()))

    n_k = n_x + n_c
    qs = (jnp.where(lo, q, 0.0).astype(BF16), jnp.where(lo, 0.0, q).astype(BF16))
    tq = q.shape[0]
    ms = [jnp.full((tq, 1), -jnp.inf, F32)] * 2
    ls = [jnp.zeros((tq, 1), F32)] * 2
    accs = [jnp.zeros((tq, 2 * DF_DIM), F32)] * 2
    for c0 in range(0, n_k, DF_KCHUNK):
        c1 = min(c0 + DF_KCHUNK, n_k)
        for mi in range(2):
            s = lax.dot_general(qs[mi], kn[c0:c1, :], nt, preferred_element_type=F32)
            m_new = jnp.maximum(ms[mi], jnp.max(s, axis=-1, keepdims=True))
            alpha = jnp.exp2(ms[mi] - m_new)
            p = jnp.exp2(s - m_new)
            ls[mi] = alpha * ls[mi] + jnp.sum(p, axis=-1, keepdims=True)
            accs[mi] = alpha * accs[mi] + jnp.dot(p.astype(BF16), vn[c0:c1, :], preferred_element_type=F32)
            ms[mi] = m_new
    o = accs[0] * (1.0 / ls[0]) - accs[1] * (lam_ref[0] / ls[1])
    o_ref[0] = (_rms(o, gs_ref[...]) * out_scale).astype(BF16)


def _df_attention(lam, zq, zx, zc, cos, sin, gq, gk, gs, out_scale):
    b, t, _ = zq.shape
    c = zc.shape[1]
    n_x = 0 if zx is None else zx.shape[1]
    w = 2 * DF_DIM
    tq = min(DF_QBLK, t)
    gq2 = jnp.concatenate([gq, gq]).reshape(1, w)
    gk2 = jnp.concatenate([gk, gk]).reshape(1, w)
    col = lambda base: (lambda bi, h, i: (bi, 0, base // w + h))
    vec = pl.BlockSpec((1, w), lambda bi, h, i: (0, 0))
    in_specs = [pl.BlockSpec(memory_space=pltpu.SMEM),
                pl.BlockSpec((1, tq, w), lambda bi, h, i: (bi, i, COL_DF_Q // w + h))]
    args = [lam.reshape(1), zq]
    if n_x:
        in_specs += [pl.BlockSpec((1, n_x, w), col(COL_DF_K)), pl.BlockSpec((1, n_x, w), col(COL_DF_V))]
        args += [zx, zx]
    in_specs += [pl.BlockSpec((1, c, w), col(COL_DF_K)), pl.BlockSpec((1, c, w), col(COL_DF_V))]
    args += [zc, zc]
    if n_x:
        in_specs += [pl.BlockSpec((tq, w), lambda bi, h, i: (i, 0)), pl.BlockSpec((tq, w), lambda bi, h, i: (i, 0)),
                     pl.BlockSpec((n_x, w), lambda bi, h, i: (0, 0)), pl.BlockSpec((n_x, w), lambda bi, h, i: (0, 0))]
        args += [cos, sin, cos, sin]
    in_specs += [vec, vec, vec]
    args += [gq2, gk2, gs.reshape(1, w)]
    return pl.pallas_call(
        functools.partial(_df_kernel, n_x=n_x, n_c=c, rope=bool(n_x), out_scale=out_scale),
        grid=(b, N_HEADS, t // tq),
        in_specs=in_specs,
        out_specs=pl.BlockSpec((1, tq, w), lambda bi, h, i: (bi, i, h)),
        out_shape=jax.ShapeDtypeStruct((b, t, N_HEADS * w), BF16),
        scratch_shapes=[pltpu.VMEM((n_x + c, w), BF16), pltpu.VMEM((n_x + c, w), BF16)],
        compiler_params=_cparams("parallel", "parallel", "arbitrary"),
        name="diff_attention" if n_x else "diff_ctx_attention",
    )(*args)


def _rg_kernel(*refs, n_x, n_c, ctx_out):
    if ctx_out:
        (ux_ref, uc_ref, gx_ref, gc_ref, cw_ref, cb_ref, wa_ref, ba_ref, wx_ref, bx_ref, lam_ref,
         ox_ref, oc_ref, a_s, b_s, p_s) = refs
    else:
        (ux_ref, uc_ref, gx_ref, cw_ref, cb_ref, wa_ref, ba_ref, wx_ref, bx_ref, lam_ref,
         ox_ref, a_s, b_s, p_s) = refs
    n_t = n_x + n_c
    clen = n_t // SCAN_CHUNKS
    cstride = clen + SCAN_PAD_ROWS

    def conv(z):
        z = z.astype(F32)
        n = z.shape[0]
        t = lax.broadcasted_iota(jnp.int32, (n, 1), 0)
        zm2 = jnp.where(t >= 2, pltpu.roll(z, 2, 0), 0.0)
        zm1 = jnp.where(t >= 1, pltpu.roll(z, 1, 0), 0.0)
        zp1 = jnp.where(t < n - 1, pltpu.roll(z, n - 1, 0), 0.0)
        return (zm2 * cw_ref[0:1, :] + zm1 * cw_ref[1:2, :] + z * cw_ref[2:3, :] + zp1 * cw_ref[3:4, :]
                + cb_ref[...])

    def pieces(t0, n):
        out, t = [], t0
        while t < t0 + n:
            ch = t // clen
            stop = min((ch + 1) * clen, t0 + n)
            out.append((t - t0, ch * cstride + (t - ch * clen), stop - t))
            t = stop
        return out

    def put(ref, d, t0, val):
        for off, row, ln in pieces(t0, val.shape[0]):
            ref[d, row:row + ln, :] = val[off:off + ln]

    def get(ref, d, t0, n):
        return jnp.concatenate([ref[d, row:row + ln, :] for _, row, ln in pieces(t0, n)], axis=0)

    def coeffs(u, d, t0):
        ub = u.astype(BF16)
        r = jax.nn.sigmoid(jnp.dot(ub, wa_ref[d, 0], preferred_element_type=F32) + ba_ref[d:d + 1, :])
        gi = jax.nn.sigmoid(jnp.dot(ub, wx_ref[d, 0], preferred_element_type=F32) + bx_ref[d:d + 1, :])
        log_a = (-RG_C) * r * jax.nn.softplus(-lam_ref[d:d + 1, :])
        a = jnp.exp(log_a)
        put(a_s, d, t0, a)
        put(b_s, d, t0, jnp.sqrt(-jnp.tanh(log_a) * (a * a + 1.0)) * (gi * u))

    u_c = conv(uc_ref[0])
    coeffs(u_c, 0, 0)
    coeffs(u_c, 1, n_x)
    u_x = conv(ux_ref[0])
    coeffs(u_x, 0, n_c)
    coeffs(u_x, 1, 0)

    def step(tau, carry):
        h_f, p_f, h_r, p_r = carry
        i_f = pl.ds(tau, SCAN_CHUNKS, stride=cstride)
        i_r = pl.ds(clen - 1 - tau, SCAN_CHUNKS, stride=cstride)
        a_f = a_s[0, i_f, :]
        a_r = a_s[1, i_r, :]
        h_f = a_f * h_f + b_s[0, i_f, :]
        h_r = a_r * h_r + b_s[1, i_r, :]
        p_f = a_f * p_f
        p_r = a_r * p_r
        b_s[0, i_f, :] = h_f
        b_s[1, i_r, :] = h_r
        p_s[0, i_f, :] = p_f
        p_s[1, i_r, :] = p_r
        return h_f, p_f, h_r, p_r

    zeros = jnp.zeros((SCAN_CHUNKS, RG_BW), F32)
    ones = jnp.ones((SCAN_CHUNKS, RG_BW), F32)
    h_f, p_f, h_r, p_r = lax.fori_loop(0, clen, step, (zeros, ones, zeros, ones), unroll=2)

    carry = jnp.zeros((1, RG_BW), F32)
    for ch in range(1, SCAN_CHUNKS):
        carry = p_f[ch - 1:ch, :] * carry + h_f[ch - 1:ch, :]
        rows = slice(ch * cstride, ch * cstride + clen)
        b_s[0, rows, :] = b_s[0, rows, :] + p_s[0, rows, :] * carry
    carry = jnp.zeros((1, RG_BW), F32)
    for ch in range(SCAN_CHUNKS - 2, -1, -1):
        carry = p_r[ch + 1:ch + 2, :] * carry + h_r[ch + 1:ch + 2, :]
        rows = slice(ch * cstride, ch * cstride + clen)
        b_s[1, rows, :] = b_s[1, rows, :] + p_s[1, rows, :] * carry

    gx = jax.nn.gelu(gx_ref[0].astype(F32), approximate=True)
    ox_ref[0] = ((get(b_s, 0, n_c, n_x) + get(b_s, 1, 0, n_x)) * gx).astype(BF16)
    if ctx_out:
        gc = jax.nn.gelu(gc_ref[0].astype(F32), approximate=True)
        oc_ref[0] = ((get(b_s, 0, 0, n_c) + get(b_s, 1, n_x, n_c)) * gc).astype(BF16)


def _rg_lru(zx, zc, conv_w, conv_b, w_a, b_a, w_x, b_x, lam, ctx_out):
    b, n_x, _ = zx.shape
    n_c = zc.shape[1]
    bw = RG_BW
    col = lambda base: (lambda bi, n: (bi, 0, base // bw + n))
    vec2 = pl.BlockSpec((2, bw), lambda bi, n: (0, n))
    wspec = pl.BlockSpec((2, 1, bw, bw), lambda bi, n: (0, n, 0, 0))
    in_specs = [pl.BlockSpec((1, n_x, bw), col(COL_RG_X)), pl.BlockSpec((1, n_c, bw), col(COL_RG_X)),
                pl.BlockSpec((1, n_x, bw), col(COL_RG_G))]
    args = [zx, zc, zx]
    if ctx_out:
        in_specs.append(pl.BlockSpec((1, n_c, bw), col(COL_RG_G)))
        args.append(zc)
    in_specs += [pl.BlockSpec((4, bw), lambda bi, n: (0, n)), pl.BlockSpec((1, bw), lambda bi, n: (0, n)),
                 wspec, vec2, wspec, vec2, vec2]
    args += [conv_w, conv_b.reshape(1, RG_WIDTH), w_a, b_a, w_x, b_x, lam]
    out_specs = [pl.BlockSpec((1, n_x, bw), lambda bi, n: (bi, 0, n))]
    out_shape = [jax.ShapeDtypeStruct((b, n_x, RG_WIDTH), BF16)]
    if ctx_out:
        out_specs.append(pl.BlockSpec((1, n_c, bw), lambda bi, n: (bi, 0, n)))
        out_shape.append(jax.ShapeDtypeStruct((b, n_c, RG_WIDTH), BF16))
    n_s = SCAN_CHUNKS * ((n_x + n_c) // SCAN_CHUNKS + SCAN_PAD_ROWS)
    outs = pl.pallas_call(
        functools.partial(_rg_kernel, n_x=n_x, n_c=n_c, ctx_out=ctx_out),
        grid=(b, RG_BLOCKS),
        in_specs=in_specs,
        out_specs=out_specs,
        out_shape=out_shape,
        scratch_shapes=[pltpu.VMEM((2, n_s, bw), F32), pltpu.VMEM((2, n_s, bw), F32), pltpu.VMEM((2, n_s, bw), F32)],
        compiler_params=_cparams("parallel", "parallel"),
        name="rg_lru",
    )(*args)
    return outs if ctx_out else (outs[0], None)


def _merge_kernel(ya_ref, yb_ref, yc_ref, ga_ref, gb_ref, gc_ref, wa_ref, wb_ref, wc_ref, o_ref):
    gate = lambda ref: jax.nn.sigmoid(ref[0].astype(F32))
    m = gate(ga_ref) * jnp.dot(ya_ref[0], wa_ref[...], preferred_element_type=F32)
    m = m + gate(gb_ref) * jnp.dot(yb_ref[0], wb_ref[...], preferred_element_type=F32)
    m = m + gate(gc_ref) * jnp.dot(yc_ref[0], wc_ref[...], preferred_element_type=F32)
    o_ref[0] = m.astype(BF16)


def _branch_merge(ya, yb, yc, z, w_branch, layer, tm, tn):
    b, t, _ = ya.shape
    d = D_MODEL
    gate = lambda k: (lambda bi, i, j: (bi, i, (COL_GATE + k * d) // tn + j))
    return pl.pallas_call(
        _merge_kernel,
        grid=(b, t // tm, d // tn),
        in_specs=[
            pl.BlockSpec((1, tm, 512), lambda bi, i, j: (bi, i, 0)),
            pl.BlockSpec((1, tm, 512), lambda bi, i, j: (bi, i, 0)),
            pl.BlockSpec((1, tm, 1024), lambda bi, i, j: (bi, i, 0)),
            pl.BlockSpec((1, tm, tn), gate(0)),
            pl.BlockSpec((1, tm, tn), gate(1)),
            pl.BlockSpec((1, tm, tn), gate(2)),
            pl.BlockSpec((None, 512, tn), lambda bi, i, j: (layer, 0, j)),
            pl.BlockSpec((None, 512, tn), lambda bi, i, j: (layer, 1, j)),
            pl.BlockSpec((None, 1024, tn), lambda bi, i, j: (layer, 1, j)),
        ],
        out_specs=pl.BlockSpec((1, tm, tn), lambda bi, i, j: (bi, i, j)),
        out_shape=jax.ShapeDtypeStruct((b, t, d), BF16),
        compiler_params=_cparams("parallel", "parallel", "arbitrary"),
        name="branch_merge",
    )(ya, yb, yc, z, z, z, w_branch, w_branch, w_branch)


def _resid_kernel(m_ref, w_ref, x_ref, gt_ref, o_ref):
    o_ref[0] = x_ref[0] + gt_ref[0] * jnp.dot(m_ref[0], w_ref[...], preferred_element_type=F32)


def _out_proj_residual(m, w_out, layer, x, gt, tm, tn):
    b, t, d = x.shape
    return pl.pallas_call(
        _resid_kernel,
        grid=(b, t // tm, d // tn),
        in_specs=[
            pl.BlockSpec((1, tm, d), lambda bi, i, j: (bi, i, 0)),
            pl.BlockSpec((None, d, tn), lambda bi, i, j: (layer, 0, j)),
            pl.BlockSpec((1, tm, tn), lambda bi, i, j: (bi, i, j)),
            pl.BlockSpec((1, 1, tn), lambda bi, i, j: (bi, 0, j)),
        ],
        out_specs=pl.BlockSpec((1, tm, tn), lambda bi, i, j: (bi, i, j)),
        out_shape=jax.ShapeDtypeStruct((b, t, d), F32),
        compiler_params=_cparams("parallel", "parallel", "arbitrary"),
        name="out_proj_residual",
    )(m, w_out, x, gt)


def _router_kernel(*refs, starts):
    n_s = len(starts) - 1
    g_ref, wh_ref, wl_ref, br_ref, h_ref, eid_ref, wt_ref = refs[3 * n_s:]
    i = pl.program_id(0)
    for k in range(n_s):
        x_ref, sc_ref, sh_ref = refs[3 * k:3 * k + 3]

        @pl.when((i >= starts[k]) & (i < starts[k + 1]))
        def _(x_ref=x_ref, sc_ref=sc_ref, sh_ref=sh_ref):
            _route_rows(x_ref, sc_ref, sh_ref, g_ref, wh_ref, wl_ref, br_ref, h_ref, eid_ref, wt_ref)


def _route_rows(x_ref, sc_ref, sh_ref, g_ref, wh_ref, wl_ref, br_ref, h_ref, eid_ref, wt_ref):
    h = _rms(x_ref[0], g_ref[...]) * (1.0 + sc_ref[0]) + sh_ref[0]
    h_ref[...] = h
    hh = h.astype(BF16)
    hl = (h - hh.astype(F32)).astype(BF16)
    logits = (jnp.dot(hh, wh_ref[...], preferred_element_type=F32)
              + jnp.dot(hl, wh_ref[...], preferred_element_type=F32)
              + jnp.dot(hh, wl_ref[...], preferred_element_type=F32)) + br_ref[...]
    lane = lax.broadcasted_iota(jnp.int32, logits.shape, 1)
    lane_f = lane.astype(F32)

    def first_argmax(v, valid):
        vm = jnp.where(valid, v, -jnp.inf)
        mx = jnp.max(vm, axis=-1, keepdims=True)
        idx = jnp.min(jnp.where(valid & (vm == mx), lane_f, 1e9), axis=-1, keepdims=True)
        return mx, idx.astype(jnp.int32)

    is_g = lane < N_GROUPS
    gmax, grp = first_argmax(logits, is_g)
    p_grp = 1.0 / jnp.sum(jnp.where(is_g, jnp.exp(logits - gmax), 0.0), axis=-1, keepdims=True)
    e_lo = N_GROUPS + grp * EXPERTS_PER_GROUP
    in_grp = (lane >= e_lo) & (lane < e_lo + EXPERTS_PER_GROUP)
    v0, i0 = first_argmax(logits, in_grp)
    v1, i1 = first_argmax(logits, in_grp & (lane != i0))
    e1 = jnp.exp(v1 - v0)
    w0 = p_grp / (1.0 + e1)
    w1 = p_grp * e1 / (1.0 + e1)
    eid_ref[...] = jnp.where(lane == 0, i0 - N_GROUPS, jnp.where(lane == 1, i1 - N_GROUPS, 0))
    wt_ref[...] = jnp.where(lane == 0, w0, jnp.where(lane == 1, w1, 0.0))


def _router(xs, scs, shs, g, wr_hi, wr_lo, br, tm):
    d = xs[0].shape[-1]
    starts = [0]
    in_specs, args = [], []
    for x, sc, sh in zip(xs, scs, shs):
        b, t, _ = x.shape
        n_i = t // tm
        s0 = starts[-1]
        starts.append(s0 + b * n_i)

        def local(i, s0=s0, n=b * n_i):
            return jnp.clip(i - s0, 0, n - 1)

        in_specs += [pl.BlockSpec((1, tm, d), lambda i, f=local, n_i=n_i: (f(i) // n_i, f(i) % n_i, 0)),
                     pl.BlockSpec((1, 1, d), lambda i, f=local, n_i=n_i: (f(i) // n_i, 0, 0)),
                     pl.BlockSpec((1, 1, d), lambda i, f=local, n_i=n_i: (f(i) // n_i, 0, 0))]
        args += [x, sc, sh]
    n_all = starts[-1] * tm
    in_specs += [pl.BlockSpec((1, d), lambda i: (0, 0)), pl.BlockSpec((d, 128), lambda i: (0, 0)),
                 pl.BlockSpec((d, 128), lambda i: (0, 0)), pl.BlockSpec((1, 128), lambda i: (0, 0))]
    args += [g.reshape(1, d), wr_hi, wr_lo, br]
    flat = lambda width: pl.BlockSpec((tm, width), lambda i: (i, 0))
    return pl.pallas_call(
        functools.partial(_router_kernel, starts=tuple(starts)),
        grid=(starts[-1],),
        in_specs=in_specs,
        out_specs=[flat(d), flat(128), flat(128)],
        out_shape=[jax.ShapeDtypeStruct((n_all, d), F32), jax.ShapeDtypeStruct((n_all, 128), jnp.int32),
                   jax.ShapeDtypeStruct((n_all, 128), F32)],
        compiler_params=_cparams("parallel"),
        name="moe_router",
    )(*args)


def _row_copy(src_hbm, dst_vmem, sem, src_row, dst_row):
    return pltpu.make_async_copy(src_hbm.at[pl.ds(src_row, 1)], dst_vmem.at[pl.ds(dst_row, 1)], sem)


def _dispatch_kernel(slot_ref, pad_ref, x_ref, o_hbm, stage0, stage1, zrow, sem):
    tm = x_ref.shape[0]
    n_fill = pad_ref.shape[2]
    i = pl.program_id(0)
    n = pl.num_programs(0)
    stages = (stage0, stage1)

    def put(src, src_row, dst_row, s):
        return pltpu.make_async_copy(src.at[pl.ds(src_row, 1)], o_hbm.at[pl.ds(dst_row, 1)], s)

    def wait_step(p):
        def wait(r, c):
            put(stages[p], 0, 0, sem.at[p]).wait()
            return c
        lax.fori_loop(0, 2 * tm + n_fill, wait, 0, unroll=8)

    @pl.when(i == 0)
    def _():
        zrow[...] = jnp.zeros(zrow.shape, zrow.dtype)

    def block(cur):
        stage = stages[cur]
        stage[...] = x_ref[...]
        for r in range(tm):
            put(stage, r, slot_ref[0, 0, 2 * r], sem.at[cur]).start(priority=0)
            put(stage, r, slot_ref[0, 0, 2 * r + 1], sem.at[cur]).start(priority=1)
        for r in range(n_fill):
            put(zrow, 0, pad_ref[0, 0, r], sem.at[cur]).start(priority=r % 2)

        @pl.when(i > 0)
        def _():
            wait_step(1 - cur)

        @pl.when(i == n - 1)
        def _():
            wait_step(cur)

    for parity in range(2):
        @pl.when(i % 2 == parity)
        def _(parity=parity):
            block(parity)


def _dispatch(slot, pad_slots, x_rows, n_out, tm):
    n_tok, d = x_rows.shape
    n = n_tok // tm
    n_fill = pad_slots.shape[0] // n
    return pl.pallas_call(
        _dispatch_kernel,
        grid=(n,),
        in_specs=[pl.BlockSpec((1, 1, 2 * tm), lambda i: (i, 0, 0), memory_space=pltpu.SMEM),
                  pl.BlockSpec((1, 1, n_fill), lambda i: (i, 0, 0), memory_space=pltpu.SMEM),
                  pl.BlockSpec((tm, d), lambda i: (i, 0))],
        out_specs=pl.BlockSpec(memory_space=pl.ANY),
        out_shape=jax.ShapeDtypeStruct((n_out, d), x_rows.dtype),
        scratch_shapes=[pltpu.VMEM((tm, d), x_rows.dtype), pltpu.VMEM((tm, d), x_rows.dtype),
                        pltpu.VMEM((8, d), x_rows.dtype), pltpu.SemaphoreType.DMA((2,))],
        compiler_params=_cparams("arbitrary"),
        name="moe_dispatch",
    )(slot.reshape(n, 1, 2 * tm), pad_slots.reshape(n, 1, n_fill), x_rows)


def _expert_kernel(blk_e_ref, x_ref, w1_ref, w3_ref, w2_ref, o_ref):
    del blk_e_ref
    xb = x_ref[...].astype(BF16)
    h1 = jnp.dot(xb, w1_ref[0], preferred_element_type=F32)
    h3 = jnp.dot(xb, w3_ref[0], preferred_element_type=F32)
    hid = (h1 * jax.nn.sigmoid(h1) * h3).astype(BF16)
    o_ref[...] = jnp.dot(hid, w2_ref[0], preferred_element_type=F32)


def _expert_blocks(blk_e, x_sorted, w1, w3, w2, layer):
    n_blk = blk_e.shape[0]
    tm = MOE_TM
    d = x_sorted.shape[1]
    grid_spec = pltpu.PrefetchScalarGridSpec(
        num_scalar_prefetch=1,
        grid=(n_blk,),
        in_specs=[
            pl.BlockSpec((tm, d), lambda i, e: (i, 0)),
            pl.BlockSpec((None, 1, d, D_EXPERT), lambda i, e: (layer, e[i], 0, 0)),
            pl.BlockSpec((None, 1, d, D_EXPERT), lambda i, e: (layer, e[i], 0, 0)),
            pl.BlockSpec((None, 1, D_EXPERT, d), lambda i, e: (layer, e[i], 0, 0)),
        ],
        out_specs=pl.BlockSpec((tm, d), lambda i, e: (i, 0)),
    )
    return pl.pallas_call(
        _expert_kernel,
        grid_spec=grid_spec,
        out_shape=jax.ShapeDtypeStruct((n_blk * tm, d), F32),
        compiler_params=_cparams("arbitrary"),
        name="moe_experts",
    )(blk_e, x_sorted, w1, w3, w2)


def _combine_kernel(slot_ref, slot_next_ref, y_hbm, x_ref, gt_ref, wt_ref, o_ref, a0, b0, a1, b1, sem):
    tm = a0.shape[0]
    i = pl.program_id(0)
    n = pl.num_programs(0)
    bufs = ((a0, b0), (a1, b1))

    def wait_rows(p):
        def wait(r, c):
            _row_copy(y_hbm, bufs[p][0], sem.at[p], 0, r).wait()
            _row_copy(y_hbm, bufs[p][1], sem.at[p], 0, r).wait()
            return c
        lax.fori_loop(0, tm, wait, 0, unroll=8)

    @pl.when(i == 0)
    def _():
        def issue(r, c):
            _row_copy(y_hbm, a0, sem.at[0], slot_ref[0, 0, 2 * r], r).start()
            _row_copy(y_hbm, b0, sem.at[0], slot_ref[0, 0, 2 * r + 1], r).start()
            return c
        lax.fori_loop(0, tm, issue, 0, unroll=8)

    def block(cur):
        wait_rows(cur)
        nxt_a, nxt_b = bufs[1 - cur]
        for r in range(tm):
            _row_copy(y_hbm, nxt_a, sem.at[1 - cur], slot_next_ref[0, 0, 2 * r], r).start(priority=0)
            _row_copy(y_hbm, nxt_b, sem.at[1 - cur], slot_next_ref[0, 0, 2 * r + 1], r).start(priority=1)
        wt = wt_ref[...]
        f = bufs[cur][0][...] * wt[:, 0:1] + bufs[cur][1][...] * wt[:, 1:2]
        o_ref[...] = x_ref[...] + gt_ref[0] * f

    for parity in range(2):
        @pl.when(i % 2 == parity)
        def _(parity=parity):
            block(parity)

        @pl.when((i == n - 1) & (i % 2 == parity))
        def _(parity=parity):
            wait_rows(1 - parity)


def _combine(slots, y, x, gt, wt, tok0, tm):
    b, t, d = x.shape
    n_i = t // tm
    n = b * n_i
    x2 = x.reshape(b * t, d)
    blk0 = tok0 // tm
    slots3 = slots.reshape(-1, 1, 2 * tm)
    out = pl.pallas_call(
        _combine_kernel,
        grid=(n,),
        in_specs=[
            pl.BlockSpec((1, 1, 2 * tm), lambda i: (blk0 + i, 0, 0), memory_space=pltpu.SMEM),
            pl.BlockSpec((1, 1, 2 * tm), lambda i: (blk0 + jnp.minimum(i + 1, n - 1), 0, 0),
                         memory_space=pltpu.SMEM),
            pl.BlockSpec(memory_space=pl.ANY),
            pl.BlockSpec((tm, d), lambda i: (i, 0)),
            pl.BlockSpec((1, 1, d), lambda i: (i // n_i, 0, 0)),
            pl.BlockSpec((tm, 128), lambda i: (blk0 + i, 0)),
        ],
        out_specs=pl.BlockSpec((tm, d), lambda i: (i, 0)),
        out_shape=jax.ShapeDtypeStruct((b * t, d), F32),
        scratch_shapes=[pltpu.VMEM((tm, d), F32)] * 4 + [pltpu.SemaphoreType.DMA((2,))],
        compiler_params=_cparams("arbitrary"),
        name="moe_combine",
    )(slots3, slots3, y, x2, gt, wt)
    return out.reshape(b, t, d)


def _routing_tables(eid, n_steps):
    m = eid.shape[0]
    tm = MOE_TM
    i32 = jnp.int32
    iota = jnp.arange(m, dtype=i32)
    se, order = lax.sort_key_val(eid, iota)
    onehot = (se[:, None] == jnp.arange(N_EXPERTS, dtype=i32)[None, :]).astype(i32)
    counts = jnp.sum(onehot, axis=0)
    starts = jnp.cumsum(counts) - counts
    pcounts = (counts + tm - 1) // tm * tm
    pends = jnp.cumsum(pcounts)
    pstarts = pends - pcounts
    dest = iota + jnp.sum(onehot * (pstarts - starts)[None, :], axis=1)
    _, slot = lax.sort_key_val(order, dest)
    n_blk = (m + N_EXPERTS * (tm - 1) + tm - 1) // tm
    n_pad = n_blk * tm
    blk_start = jnp.arange(n_blk, dtype=i32) * tm
    blk_e = jnp.minimum(jnp.sum((pends[None, :] <= blk_start[:, None]).astype(i32), axis=1), N_EXPERTS - 1)
    n_empty = n_pad - m
    n_fill = -(-n_empty // n_steps)
    n_fill = (n_fill + 7) // 8 * 8
    seg_len = jnp.concatenate([pcounts - counts, (n_pad - pends[-1:])])
    seg_first = jnp.concatenate([pstarts + counts, pends[-1:]])
    seg_end = jnp.cumsum(seg_len)
    seg_begin = seg_end - seg_len
    j = jnp.arange(n_empty, dtype=i32)
    in_seg = ((seg_begin[None, :] <= j[:, None]) & (j[:, None] < seg_end[None, :])).astype(i32)
    empty = j + jnp.sum(in_seg * (seg_first - seg_begin)[None, :], axis=1)
    n_spare = n_steps * n_fill - n_empty
    fill = jnp.concatenate([empty, n_pad + jnp.arange(n_spare, dtype=i32)])
    return blk_e.astype(i32), slot, fill, n_pad + n_spare


def _moe(xs, scs, shs, gts, norm_g, wr_hi, wr_lo, br, w1, w3, w2, layer):
    h_all, eid_all, wt_all = _router(xs, scs, shs, norm_g, wr_hi, wr_lo, br, 512)
    n_steps = h_all.shape[0] // MOE_DISPATCH_TOKENS
    blk_e, slot, fill, n_rows = _routing_tables(eid_all[:, :TOP_K].reshape(-1), n_steps)
    y = _expert_blocks(blk_e, _dispatch(slot, fill, h_all, n_rows, MOE_DISPATCH_TOKENS), w1, w3, w2, layer)
    outs = []
    tok0 = 0
    for x, gt in zip(xs, gts):
        outs.append(_combine(slot, y, x, gt, wt_all, tok0, tm=256))
        tok0 += x.shape[0] * x.shape[1]
    return outs


def kernel(x, c, ctx, c_ctx, w_mod, b_mod, norm1_g, norm2_g, w_in, na_q_g, na_k_g, na_rpb, df_q_g, df_k_g, df_lam, df_sub_g, rg_conv_w, rg_conv_b, rg_w_a, rg_b_a, rg_w_x, rg_b_x, rg_lam, w_branch, w_out, w_router_g, b_router_g, w_router_e, b_router_e, w1, w3, w2):
    B, S, D = x.shape
    C = ctx.shape[1]
    rope_cos, rope_sin = _rope_tables(S)
    na_lo, na_hi = _na_bias_tables(na_rpb)

    pad = (-(B + 1)) % 8
    rows = jnp.concatenate([c, c_ctx[None, :], jnp.zeros((pad, D), F32)], axis=0)
    mod_all = _modulation(rows, w_mod, b_mod)

    w_in_b, w_branch_b, w_out_b = w_in.astype(BF16), w_branch.astype(BF16), w_out.astype(BF16)
    w1_b, w3_b, w2_b = w1.astype(BF16), w3.astype(BF16), w2.astype(BF16)

    xc = ctx.reshape(1, B * C, D)
    for l in range(DEPTH):
        need_ctx = l < DEPTH - 1
        lam_init = 0.8 - 0.6 * float(np.exp(-0.3 * l))
        mod = mod_all[l]
        sh1, sc1, gt1, sh2, sc2, gt2 = [mod[:B, k * D:(k + 1) * D][:, None, :] for k in range(6)]
        csh1, csc1, cgt1, csh2, csc2, cgt2 = [mod[B:B + 1, k * D:(k + 1) * D][:, None, :] for k in range(6)]

        zx = _norm_mod_matmul(x, norm1_g[l], sc1, sh1, w_in_b, l, IN_COLS, tm=1024, tn=1024)
        n_cc = IN_COLS if need_ctx else KV_COLS
        zc = _norm_mod_matmul(xc, norm1_g[l], csc1, csh1, w_in_b, l, n_cc, tm=1024, tn=1024).reshape(B, C, n_cc)

        y_a = _na_attention(zx, zc, na_lo[l], na_hi[l], na_q_g[l], na_k_g[l])

        lp = df_lam[l]
        lam = jnp.exp(jnp.sum(lp[0] * lp[1])) - jnp.exp(jnp.sum(lp[2] * lp[3])) + lam_init
        y_b = _df_attention(lam, zx, zx, zc, rope_cos, rope_sin, df_q_g[l], df_k_g[l], df_sub_g[l], 1.0 - lam_init)

        y_c, y_cc = _rg_lru(zx, zc, rg_conv_w[l], rg_conv_b[l], rg_w_a[l].astype(BF16), rg_b_a[l],
                            rg_w_x[l].astype(BF16), rg_b_x[l], rg_lam[l], need_ctx)

        m_x = _branch_merge(y_a, y_b, y_c, zx, w_branch_b, l, tm=1024, tn=1024)
        x = _out_proj_residual(m_x, w_out_b, l, x, gt1, tm=1024, tn=1024)
        if need_ctx:
            y_ac = _na_ctx_attention(zc, na_q_g[l], na_k_g[l])
            y_bc = _df_attention(lam, zc, None, zc, None, None, df_q_g[l], df_k_g[l], df_sub_g[l], 1.0 - lam_init)
            m_c = _branch_merge(y_ac, y_bc, y_cc, zc, w_branch_b, l, tm=C, tn=1024)
            xc = _out_proj_residual(m_c.reshape(1, B * C, D), w_out_b, l, xc, cgt1, tm=1024, tn=1024)

        wr = jnp.concatenate([w_router_g[l], w_router_e[l],
                              jnp.zeros((D, 128 - N_GROUPS - N_EXPERTS), F32)], axis=1)
        wr_hi = wr.astype(BF16)
        wr_lo = (wr - wr_hi.astype(F32)).astype(BF16)
        br = jnp.concatenate([b_router_g[l], b_router_e[l],
                              jnp.zeros((128 - N_GROUPS - N_EXPERTS,), F32)]).reshape(1, 128)
        if need_ctx:
            xc, x = _moe([xc, x], [csc2, sc2], [csh2, sh2], [cgt2, gt2], norm2_g[l],
                         wr_hi, wr_lo, br, w1_b, w3_b, w2_b, l)
        else:
            (x,) = _moe([x], [sc2], [sh2], [gt2], norm2_g[l], wr_hi, wr_lo, br, w1_b, w3_b, w2_b, l)
    return x
```

```python
import functools

import jax
import jax.numpy as jnp
import numpy as np
from jax import lax
from jax.experimental import pallas as pl
from jax.experimental.pallas import tpu as pltpu

F32 = jnp.float32
BF16 = jnp.bfloat16

D_MODEL = 2048
DEPTH = 4
GRID_W = 64
HEAD_DIM = 128
N_HEADS = 4
NA_WIN_R = 8
NA_WIN_C = 16
NA_QCB = 16
NA_KCB = 32
DF_DIM = 64
RG_WIDTH = 1024
RG_BLOCKS = 8
RG_BW = 128
RG_C = 8.0
N_GROUPS = 4
EXPERTS_PER_GROUP = 8
N_EXPERTS = 32
TOP_K = 2
D_EXPERT = 512
ROPE_BASE = 10000.0
EPS = 1e-6
NEG = -1e30
LOG2_E = 1.4426950408889634

COL_NA_K, COL_NA_V, COL_DF_K, COL_DF_V, COL_RG_X = 0, 512, 1024, 1536, 2048
COL_NA_Q, COL_DF_Q, COL_RG_G, COL_GATE = 3072, 3584, 4096, 5120
KV_COLS = 3072
MIX_COLS = 5120
IN_COLS = MIX_COLS + 3 * D_MODEL

VMEM_LIMIT_BYTES = 56 * 1024 * 1024

NA_ROWS_PER_STEP = 4
NA_KEY_ROWS = NA_ROWS_PER_STEP + NA_WIN_R
DF_QBLK = 1024
DF_KCHUNK = 768
MOE_TM = 256
MOE_DISPATCH_TOKENS = 512
MOE_COMBINE_TOKENS = 512
SCAN_CHUNKS = 8
SCAN_PAD_ROWS = 8


def _cparams(*sem):
    return pltpu.CompilerParams(dimension_semantics=sem, vmem_limit_bytes=VMEM_LIMIT_BYTES)


def _rms(x, g):
    x = x.astype(F32)
    return x * lax.rsqrt(jnp.mean(x * x, axis=-1, keepdims=True) + EPS) * g


def _mod_kernel(s_ref, w_ref, b_ref, o_ref):
    s = s_ref[...]
    a = (s * jax.nn.sigmoid(s)).astype(BF16)
    o_ref[0] = jnp.dot(a, w_ref[0].astype(BF16), preferred_element_type=F32) + b_ref[0]


def _modulation(rows, w_mod, b_mod, tn=1024):
    n_l, d, n = w_mod.shape
    r = rows.shape[0]
    return pl.pallas_call(
        _mod_kernel,
        grid=(n_l, n // tn),
        in_specs=[
            pl.BlockSpec((r, d), lambda l, j: (0, 0)),
            pl.BlockSpec((1, d, tn), lambda l, j: (l, 0, j)),
            pl.BlockSpec((1, 1, tn), lambda l, j: (l, 0, j)),
        ],
        out_specs=pl.BlockSpec((1, r, tn), lambda l, j: (l, 0, j)),
        out_shape=jax.ShapeDtypeStruct((n_l, r, n), F32),
        compiler_params=_cparams("parallel", "parallel"),
        name="modulation",
    )(rows, w_mod, b_mod.reshape(n_l, 1, n))


def _nmm_kernel(x_ref, g_ref, sc_ref, sh_ref, w_ref, o_ref, h_ref):
    @pl.when(pl.program_id(2) == 0)
    def _():
        h = _rms(x_ref[0], g_ref[...]) * (1.0 + sc_ref[0]) + sh_ref[0]
        h_ref[...] = h.astype(BF16)

    o_ref[0] = jnp.dot(h_ref[...], w_ref[...], preferred_element_type=F32).astype(o_ref.dtype)


def _norm_mod_matmul(x, g, sc, sh, w, layer, n, tm, tn):
    b, t, d = x.shape
    return pl.pallas_call(
        _nmm_kernel,
        grid=(b, t // tm, n // tn),
        in_specs=[
            pl.BlockSpec((1, tm, d), lambda bi, i, j: (bi, i, 0)),
            pl.BlockSpec((1, d), lambda bi, i, j: (0, 0)),
            pl.BlockSpec((1, 1, d), lambda bi, i, j: (bi, 0, 0)),
            pl.BlockSpec((1, 1, d), lambda bi, i, j: (bi, 0, 0)),
            pl.BlockSpec((None, d, tn), lambda bi, i, j: (layer, 0, j)),
        ],
        out_specs=pl.BlockSpec((1, tm, tn), lambda bi, i, j: (bi, i, j)),
        out_shape=jax.ShapeDtypeStruct((b, t, n), BF16),
        scratch_shapes=[pltpu.VMEM((tm, d), BF16)],
        compiler_params=_cparams("parallel", "parallel", "arbitrary"),
        name="norm_mod_proj",
    )(x, g.reshape(1, d), sc, sh, w)


def _na_window_start(r0):
    lower = jnp.clip(r0 - NA_WIN_R // 2, 0, (2048 // GRID_W) - NA_WIN_R)
    return jnp.minimum(lower, (2048 // GRID_W) - NA_KEY_ROWS)


def _na_bias_tables(rpb):
    w = GRID_W
    qc = np.arange(w)
    kc = np.arange(w)
    qwin = np.clip(qc - NA_WIN_C // 2, 0, w - NA_WIN_C)
    kcol_start = np.clip((qc // NA_QCB) * NA_QCB - (NA_KCB - NA_QCB) // 2, 0, w - NA_KCB)
    col_ok = (kc[None, :] >= qwin[:, None]) & (kc[None, :] < qwin[:, None] + NA_WIN_C)
    col_ok &= (kc[None, :] >= kcol_start[:, None]) & (kc[None, :] < kcol_start[:, None] + NA_KCB)
    edge = w - NA_WIN_C
    lead = rpb.shape[:-1]
    ext = jnp.concatenate([jnp.repeat(rpb[..., :1], edge, axis=-1), rpb, jnp.repeat(rpb[..., -1:], edge, axis=-1),
                           jnp.zeros(lead + (1,), F32)], axis=-1)
    skew = jnp.tile(ext, (1,) * len(lead) + (w,))[..., :w * (2 * w - 1)].reshape(lead + (w, 2 * w - 1))
    toep = jnp.where(col_ok, skew[..., w - 1:], NEG)
    toep = jnp.concatenate([toep, jnp.full(toep.shape[:-3] + (1, w, w), NEG, F32)], axis=-3)
    zeros = jnp.zeros_like(toep)
    return jnp.concatenate([toep, zeros], axis=-1), jnp.concatenate([zeros, toep], axis=-1)


def _na_bias_blocks(n_rows):
    rb, kw = NA_ROWS_PER_STEP, NA_KEY_ROWS
    rows = np.arange(n_rows)
    row_start = np.clip(rows - NA_WIN_R // 2, 0, n_rows - NA_WIN_R)
    n_steps = n_rows // rb
    blk = np.full((n_steps, rb, kw), 2 * NA_WIN_R - 1, np.int32)
    for s in range(n_steps):
        ws = min(int(np.clip(s * rb - NA_WIN_R // 2, 0, n_rows - NA_WIN_R)), n_rows - kw)
        for j in range(rb):
            r = s * rb + j
            for i in range(kw):
                if row_start[r] <= ws + i < row_start[r] + NA_WIN_R:
                    blk[s, j, i] = ws + i - r + (NA_WIN_R - 1)
    return blk.reshape(-1)


def _na_kernel(blk_ref, q_ref, k_ref, v_ref, kc_ref, vc_ref, lo_ref, hi_ref, gq_ref, gk_ref, o_ref,
               kn, vn, knc, vnc):
    step = pl.program_id(1)

    def bias(h):
        rows = []
        for j in range(NA_ROWS_PER_STEP):
            base = (step * NA_ROWS_PER_STEP + j) * NA_KEY_ROWS
            rows.append(jnp.concatenate(
                [lo_ref[h, blk_ref[base + i]] + hi_ref[h, blk_ref[base + i + 1]] for i in range(0, NA_KEY_ROWS, 2)],
                axis=-1))
        return jnp.concatenate(rows, axis=0)

    @pl.when(step == 0)
    def _():
        for h in range(N_HEADS):
            sl = slice(h * HEAD_DIM, (h + 1) * HEAD_DIM)
            kn[:, sl] = _rms(k_ref[0, :, sl], gk_ref[...]).astype(BF16)
            knc[:, sl] = _rms(kc_ref[0, :, sl], gk_ref[...]).astype(BF16)
        vn[...] = v_ref[0].astype(BF16)
        vnc[...] = vc_ref[0].astype(BF16)

    ws = _na_window_start(step * NA_ROWS_PER_STEP)
    start = pl.multiple_of(ws * GRID_W, GRID_W)
    nk = NA_KEY_ROWS * GRID_W
    scale = HEAD_DIM ** -0.5
    nt = (((1,), (1,)), ((), ()))
    for h in range(N_HEADS):
        sl = slice(h * HEAD_DIM, (h + 1) * HEAD_DIM)
        qn = (_rms(q_ref[0, :, sl], gq_ref[...]) * scale).astype(BF16)
        s_loc = lax.dot_general(qn, kn[pl.ds(start, nk), sl], nt, preferred_element_type=F32) + bias(h)
        s_ctx = lax.dot_general(qn, knc[:, sl], nt, preferred_element_type=F32)
        m = jnp.maximum(jnp.max(s_loc, axis=-1, keepdims=True), jnp.max(s_ctx, axis=-1, keepdims=True))
        p_loc = jnp.exp(s_loc - m)
        p_ctx = jnp.exp(s_ctx - m)
        l = jnp.sum(p_loc, axis=-1, keepdims=True) + jnp.sum(p_ctx, axis=-1, keepdims=True)
        o = jnp.dot(p_loc.astype(BF16), vn[pl.ds(start, nk), sl], preferred_element_type=F32)
        o = o + jnp.dot(p_ctx.astype(BF16), vnc[:, sl], preferred_element_type=F32)
        o_ref[0, :, sl] = (o * (1.0 / l)).astype(BF16)


def _na_attention(zx, zc, bias_lo, bias_hi, gq, gk):
    b, s, _ = zx.shape
    c = zc.shape[1]
    w = N_HEADS * HEAD_DIM
    tq = NA_ROWS_PER_STEP * GRID_W
    blk = jnp.asarray(_na_bias_blocks(s // GRID_W))
    table = pl.BlockSpec(bias_lo.shape, lambda bi, i, blk: (0, 0, 0, 0))
    grid_spec = pltpu.PrefetchScalarGridSpec(
        num_scalar_prefetch=1,
        grid=(b, s // tq),
        in_specs=[
            pl.BlockSpec((1, tq, w), lambda bi, i, blk: (bi, i, COL_NA_Q // w)),
            pl.BlockSpec((1, s, w), lambda bi, i, blk: (bi, 0, COL_NA_K // w)),
            pl.BlockSpec((1, s, w), lambda bi, i, blk: (bi, 0, COL_NA_V // w)),
            pl.BlockSpec((1, c, w), lambda bi, i, blk: (bi, 0, COL_NA_K // w)),
            pl.BlockSpec((1, c, w), lambda bi, i, blk: (bi, 0, COL_NA_V // w)),
            table, table,
            pl.BlockSpec((1, HEAD_DIM), lambda bi, i, blk: (0, 0)),
            pl.BlockSpec((1, HEAD_DIM), lambda bi, i, blk: (0, 0)),
        ],
        out_specs=pl.BlockSpec((1, tq, w), lambda bi, i, blk: (bi, i, 0)),
        scratch_shapes=[pltpu.VMEM((s, w), BF16), pltpu.VMEM((s, w), BF16),
                        pltpu.VMEM((c, w), BF16), pltpu.VMEM((c, w), BF16)],
    )
    return pl.pallas_call(
        _na_kernel,
        grid_spec=grid_spec,
        out_shape=jax.ShapeDtypeStruct((b, s, w), BF16),
        compiler_params=_cparams("parallel", "arbitrary"),
        name="na_attention",
    )(blk, zx, zx, zx, zc, zc, bias_lo, bias_hi, gq.reshape(1, HEAD_DIM), gk.reshape(1, HEAD_DIM))


def _na_ctx_kernel(q_ref, k_ref, v_ref, gq_ref, gk_ref, o_ref):
    scale = HEAD_DIM ** -0.5
    nt = (((1,), (1,)), ((), ()))
    for h in range(N_HEADS):
        sl = slice(h * HEAD_DIM, (h + 1) * HEAD_DIM)
        qn = (_rms(q_ref[0, :, sl], gq_ref[...]) * scale).astype(BF16)
        kn = _rms(k_ref[0, :, sl], gk_ref[...]).astype(BF16)
        s = lax.dot_general(qn, kn, nt, preferred_element_type=F32)
        p = jnp.exp(s - jnp.max(s, axis=-1, keepdims=True))
        l = jnp.sum(p, axis=-1, keepdims=True)
        o = jnp.dot(p.astype(BF16), v_ref[0, :, sl].astype(BF16), preferred_element_type=F32)
        o_ref[0, :, sl] = (o * (1.0 / l)).astype(BF16)


def _na_ctx_attention(zc, gq, gk):
    b, c, _ = zc.shape
    w = N_HEADS * HEAD_DIM
    return pl.pallas_call(
        _na_ctx_kernel,
        grid=(b,),
        in_specs=[
            pl.BlockSpec((1, c, w), lambda bi: (bi, 0, COL_NA_Q // w)),
            pl.BlockSpec((1, c, w), lambda bi: (bi, 0, COL_NA_K // w)),
            pl.BlockSpec((1, c, w), lambda bi: (bi, 0, COL_NA_V // w)),
            pl.BlockSpec((1, HEAD_DIM), lambda bi: (0, 0)),
            pl.BlockSpec((1, HEAD_DIM), lambda bi: (0, 0)),
        ],
        out_specs=pl.BlockSpec((1, c, w), lambda bi: (bi, 0, 0)),
        out_shape=jax.ShapeDtypeStruct((b, c, w), BF16),
        compiler_params=_cparams("parallel"),
        name="na_ctx_attention",
    )(zc, zc, zc, gq.reshape(1, HEAD_DIM), gk.reshape(1, HEAD_DIM))


def _rope_tables(n_tok):
    t = np.arange(n_tok)
    pos = np.stack([t // GRID_W, t % GRID_W], axis=0).astype(np.float32)
    n_freq = DF_DIM // 4
    inv = (np.float32(ROPE_BASE) ** (-np.arange(n_freq, dtype=np.float32) / n_freq)).astype(np.float32)
    ang = jnp.asarray(pos[:, :, None] * inv)
    cos, sin = jnp.cos(ang), jnp.sin(ang)
    cos64 = jnp.concatenate([cos[0], cos[0], cos[1], cos[1]], axis=-1)
    sin64 = jnp.concatenate([-sin[0], sin[0], -sin[1], sin[1]], axis=-1)
    return jnp.concatenate([cos64, cos64], axis=-1), jnp.concatenate([sin64, sin64], axis=-1)


def _df_kernel(lam_ref, q_ref, *refs, n_x, n_c, rope, out_scale):
    if n_x:
        kx_ref, vx_ref, kc_ref, vc_ref, cq_ref, sq_ref, ck_ref, sk_ref, gq_ref, gk_ref, gs_ref, o_ref, kn, vn = refs
    else:
        kc_ref, vc_ref, gq_ref, gk_ref, gs_ref, o_ref, kn, vn = refs
    lane = lax.broadcasted_iota(jnp.int32, (1, 2 * DF_DIM), 1)
    lo = lane < DF_DIM
    first = (lane % (DF_DIM // 2)) < (DF_DIM // 4)

    def norm64(x, g):
        x = x.astype(F32)
        x2 = x * x
        s0 = jnp.sum(jnp.where(lo, x2, 0.0), axis=-1, keepdims=True)
        s1 = jnp.sum(jnp.where(lo, 0.0, x2), axis=-1, keepdims=True)
        ms = jnp.where(lo, s0, s1) * (1.0 / DF_DIM)
        return x * lax.rsqrt(ms + EPS) * g

    def rot(x, cos, sin):
        partner = jnp.where(first, pltpu.roll(x, 2 * DF_DIM - DF_DIM // 4, 1), pltpu.roll(x, DF_DIM // 4, 1))
        return x * cos + partner * sin

    @pl.when(pl.program_id(2) == 0)
    def _():
        if n_x:
            kx = rot(norm64(kx_ref[0], gk_ref[...]), ck_ref[...], sk_ref[...])
            kn[0:n_x, :] = kx.astype(BF16)
            vn[0:n_x, :] = vx_ref[0].astype(BF16)
        kn[n_x:n_x + n_c, :] = norm64(kc_ref[0], gk_ref[...]).astype(BF16)
        vn[n_x:n_x + n_c, :] = vc_ref[0].astype(BF16)

    q = norm64(q_ref[0], gq_ref[...])
    if rope:
        q = rot(q, cq_ref[...], sq_ref[...])
    q = q * (DF_DIM ** -0.5 * LOG2_E)
    nt = (((1,), (1,)), ((), ()))

    n_k = n_x + n_c
    qs = (jnp.where(lo, q, 0.0).astype(BF16), jnp.where(lo, 0.0, q).astype(BF16))
    tq = q.shape[0]
    ms = [jnp.full((tq, 1), -jnp.inf, F32)] * 2
    ls = [jnp.zeros((tq, 1), F32)] * 2
    accs = [jnp.zeros((tq, 2 * DF_DIM), F32)] * 2
    for c0 in range(0, n_k, DF_KCHUNK):
        c1 = min(c0 + DF_KCHUNK, n_k)
        for mi in range(2):
            s = lax.dot_general(qs[mi], kn[c0:c1, :], nt, preferred_element_type=F32)
            m_new = jnp.maximum(ms[mi], jnp.max(s, axis=-1, keepdims=True))
            alpha = jnp.exp2(ms[mi] - m_new)
            p = jnp.exp2(s - m_new)
            ls[mi] = alpha * ls[mi] + jnp.sum(p, axis=-1, keepdims=True)
            accs[mi] = alpha * accs[mi] + jnp.dot(p.astype(BF16), vn[c0:c1, :], preferred_element_type=F32)
            ms[mi] = m_new
    o = accs[0] * (1.0 / ls[0]) - accs[1] * (lam_ref[0] / ls[1])
    o_ref[0] = (_rms(o, gs_ref[...]) * out_scale).astype(BF16)


def _df_attention(lam, zq, zx, zc, cos, sin, gq, gk, gs, out_scale):
    b, t, _ = zq.shape
    c = zc.shape[1]
    n_x = 0 if zx is None else zx.shape[1]
    w = 2 * DF_DIM
    tq = min(DF_QBLK, t)
    gq2 = jnp.concatenate([gq, gq]).reshape(1, w)
    gk2 = jnp.concatenate([gk, gk]).reshape(1, w)
    col = lambda base: (lambda bi, h, i: (bi, 0, base // w + h))
    vec = pl.BlockSpec((1, w), lambda bi, h, i: (0, 0))
    in_specs = [pl.BlockSpec(memory_space=pltpu.SMEM),
                pl.BlockSpec((1, tq, w), lambda bi, h, i: (bi, i, COL_DF_Q // w + h))]
    args = [lam.reshape(1), zq]
    if n_x:
        in_specs += [pl.BlockSpec((1, n_x, w), col(COL_DF_K)), pl.BlockSpec((1, n_x, w), col(COL_DF_V))]
        args += [zx, zx]
    in_specs += [pl.BlockSpec((1, c, w), col(COL_DF_K)), pl.BlockSpec((1, c, w), col(COL_DF_V))]
    args += [zc, zc]
    if n_x:
        in_specs += [pl.BlockSpec((tq, w), lambda bi, h, i: (i, 0)), pl.BlockSpec((tq, w), lambda bi, h, i: (i, 0)),
                     pl.BlockSpec((n_x, w), lambda bi, h, i: (0, 0)), pl.BlockSpec((n_x, w), lambda bi, h, i: (0, 0))]
        args += [cos, sin, cos, sin]
    in_specs += [vec, vec, vec]
    args += [gq2, gk2, gs.reshape(1, w)]
    return pl.pallas_call(
        functools.partial(_df_kernel, n_x=n_x, n_c=c, rope=bool(n_x), out_scale=out_scale),
        grid=(b, N_HEADS, t // tq),
        in_specs=in_specs,
        out_specs=pl.BlockSpec((1, tq, w), lambda bi, h, i: (bi, i, h)),
        out_shape=jax.ShapeDtypeStruct((b, t, N_HEADS * w), BF16),
        scratch_shapes=[pltpu.VMEM((n_x + c, w), BF16), pltpu.VMEM((n_x + c, w), BF16)],
        compiler_params=_cparams("parallel", "parallel", "arbitrary"),
        name="diff_attention" if n_x else "diff_ctx_attention",
    )(*args)


def _rg_kernel(*refs, n_x, n_c, ctx_out):
    if ctx_out:
        (ux_ref, uc_ref, gx_ref, gc_ref, cw_ref, cb_ref, wa_ref, ba_ref, wx_ref, bx_ref, lam_ref,
         ox_ref, oc_ref, a_s, b_s, p_s) = refs
    else:
        (ux_ref, uc_ref, gx_ref, cw_ref, cb_ref, wa_ref, ba_ref, wx_ref, bx_ref, lam_ref,
         ox_ref, a_s, b_s, p_s) = refs
    n_t = n_x + n_c
    clen = n_t // SCAN_CHUNKS
    cstride = clen + SCAN_PAD_ROWS

    def conv(z):
        z = z.astype(F32)
        n = z.shape[0]
        t = lax.broadcasted_iota(jnp.int32, (n, 1), 0)
        zm2 = jnp.where(t >= 2, pltpu.roll(z, 2, 0), 0.0)
        zm1 = jnp.where(t >= 1, pltpu.roll(z, 1, 0), 0.0)
        zp1 = jnp.where(t < n - 1, pltpu.roll(z, n - 1, 0), 0.0)
        return (zm2 * cw_ref[0:1, :] + zm1 * cw_ref[1:2, :] + z * cw_ref[2:3, :] + zp1 * cw_ref[3:4, :]
                + cb_ref[...])

    def pieces(t0, n):
        out, t = [], t0
        while t < t0 + n:
            ch = t // clen
            stop = min((ch + 1) * clen, t0 + n)
            out.append((t - t0, ch * cstride + (t - ch * clen), stop - t))
            t = stop
        return out

    def put(ref, d, t0, val):
        for off, row, ln in pieces(t0, val.shape[0]):
            ref[d, row:row + ln, :] = val[off:off + ln]

    def get(ref, d, t0, n):
        return jnp.concatenate([ref[d, row:row + ln, :] for _, row, ln in pieces(t0, n)], axis=0)

    def coeffs(u, d, t0):
        ub = u.astype(BF16)
        r = jax.nn.sigmoid(jnp.dot(ub, wa_ref[d, 0], preferred_element_type=F32) + ba_ref[d:d + 1, :])
        gi = jax.nn.sigmoid(jnp.dot(ub, wx_ref[d, 0], preferred_element_type=F32) + bx_ref[d:d + 1, :])
        log_a = (-RG_C) * r * jax.nn.softplus(-lam_ref[d:d + 1, :])
        a = jnp.exp(log_a)
        put(a_s, d, t0, a)
        put(b_s, d, t0, jnp.sqrt(-jnp.tanh(log_a) * (a * a + 1.0)) * (gi * u))

    u_c = conv(uc_ref[0])
    coeffs(u_c, 0, 0)
    coeffs(u_c, 1, n_x)
    u_x = conv(ux_ref[0])
    coeffs(u_x, 0, n_c)
    coeffs(u_x, 1, 0)

    def step(tau, carry):
        h_f, p_f, h_r, p_r = carry
        i_f = pl.ds(tau, SCAN_CHUNKS, stride=cstride)
        i_r = pl.ds(clen - 1 - tau, SCAN_CHUNKS, stride=cstride)
        a_f = a_s[0, i_f, :]
        a_r = a_s[1, i_r, :]
        h_f = a_f * h_f + b_s[0, i_f, :]
        h_r = a_r * h_r + b_s[1, i_r, :]
        p_f = a_f * p_f
        p_r = a_r * p_r
        b_s[0, i_f, :] = h_f
        b_s[1, i_r, :] = h_r
        p_s[0, i_f, :] = p_f
        p_s[1, i_r, :] = p_r
        return h_f, p_f, h_r, p_r

    zeros = jnp.zeros((SCAN_CHUNKS, RG_BW), F32)
    ones = jnp.ones((SCAN_CHUNKS, RG_BW), F32)
    h_f, p_f, h_r, p_r = lax.fori_loop(0, clen, step, (zeros, ones, zeros, ones), unroll=2)

    carry = jnp.zeros((1, RG_BW), F32)
    for ch in range(1, SCAN_CHUNKS):
        carry = p_f[ch - 1:ch, :] * carry + h_f[ch - 1:ch, :]
        rows = slice(ch * cstride, ch * cstride + clen)
        b_s[0, rows, :] = b_s[0, rows, :] + p_s[0, rows, :] * carry
    carry = jnp.zeros((1, RG_BW), F32)
    for ch in range(SCAN_CHUNKS - 2, -1, -1):
        carry = p_r[ch + 1:ch + 2, :] * carry + h_r[ch + 1:ch + 2, :]
        rows = slice(ch * cstride, ch * cstride + clen)
        b_s[1, rows, :] = b_s[1, rows, :] + p_s[1, rows, :] * carry

    gx = jax.nn.gelu(gx_ref[0].astype(F32), approximate=True)
    ox_ref[0] = ((get(b_s, 0, n_c, n_x) + get(b_s, 1, 0, n_x)) * gx).astype(BF16)
    if ctx_out:
        gc = jax.nn.gelu(gc_ref[0].astype(F32), approximate=True)
        oc_ref[0] = ((get(b_s, 0, 0, n_c) + get(b_s, 1, n_x, n_c)) * gc).astype(BF16)


def _rg_lru(zx, zc, conv_w, conv_b, w_a, b_a, w_x, b_x, lam, ctx_out):
    b, n_x, _ = zx.shape
    n_c = zc.shape[1]
    bw = RG_BW
    col = lambda base: (lambda bi, n: (bi, 0, base // bw + n))
    vec2 = pl.BlockSpec((2, bw), lambda bi, n: (0, n))
    wspec = pl.BlockSpec((2, 1, bw, bw), lambda bi, n: (0, n, 0, 0))
    in_specs = [pl.BlockSpec((1, n_x, bw), col(COL_RG_X)), pl.BlockSpec((1, n_c, bw), col(COL_RG_X)),
                pl.BlockSpec((1, n_x, bw), col(COL_RG_G))]
    args = [zx, zc, zx]
    if ctx_out:
        in_specs.append(pl.BlockSpec((1, n_c, bw), col(COL_RG_G)))
        args.append(zc)
    in_specs += [pl.BlockSpec((4, bw), lambda bi, n: (0, n)), pl.BlockSpec((1, bw), lambda bi, n: (0, n)),
                 wspec, vec2, wspec, vec2, vec2]
    args += [conv_w, conv_b.reshape(1, RG_WIDTH), w_a, b_a, w_x, b_x, lam]
    out_specs = [pl.BlockSpec((1, n_x, bw), lambda bi, n: (bi, 0, n))]
    out_shape = [jax.ShapeDtypeStruct((b, n_x, RG_WIDTH), BF16)]
    if ctx_out:
        out_specs.append(pl.BlockSpec((1, n_c, bw), lambda bi, n: (bi, 0, n)))
        out_shape.append(jax.ShapeDtypeStruct((b, n_c, RG_WIDTH), BF16))
    n_s = SCAN_CHUNKS * ((n_x + n_c) // SCAN_CHUNKS + SCAN_PAD_ROWS)
    outs = pl.pallas_call(
        functools.partial(_rg_kernel, n_x=n_x, n_c=n_c, ctx_out=ctx_out),
        grid=(b, RG_BLOCKS),
        in_specs=in_specs,
        out_specs=out_specs,
        out_shape=out_shape,
        scratch_shapes=[pltpu.VMEM((2, n_s, bw), F32), pltpu.VMEM((2, n_s, bw), F32), pltpu.VMEM((2, n_s, bw), F32)],
        compiler_params=_cparams("parallel", "parallel"),
        name="rg_lru",
    )(*args)
    return outs if ctx_out else (outs[0], None)


def _merge_kernel(ya_ref, yb_ref, yc_ref, ga_ref, gb_ref, gc_ref, wa_ref, wb_ref, wc_ref, o_ref):
    gate = lambda ref: jax.nn.sigmoid(ref[0].astype(F32))
    m = gate(ga_ref) * jnp.dot(ya_ref[0], wa_ref[...], preferred_element_type=F32)
    m = m + gate(gb_ref) * jnp.dot(yb_ref[0], wb_ref[...], preferred_element_type=F32)
    m = m + gate(gc_ref) * jnp.dot(yc_ref[0], wc_ref[...], preferred_element_type=F32)
    o_ref[0] = m.astype(BF16)


def _branch_merge(ya, yb, yc, z, w_branch, layer, tm, tn):
    b, t, _ = ya.shape
    d = D_MODEL
    gate = lambda k: (lambda bi, i, j: (bi, i, (COL_GATE + k * d) // tn + j))
    return pl.pallas_call(
        _merge_kernel,
        grid=(b, t // tm, d // tn),
        in_specs=[
            pl.BlockSpec((1, tm, 512), lambda bi, i, j: (bi, i, 0)),
            pl.BlockSpec((1, tm, 512), lambda bi, i, j: (bi, i, 0)),
            pl.BlockSpec((1, tm, 1024), lambda bi, i, j: (bi, i, 0)),
            pl.BlockSpec((1, tm, tn), gate(0)),
            pl.BlockSpec((1, tm, tn), gate(1)),
            pl.BlockSpec((1, tm, tn), gate(2)),
            pl.BlockSpec((None, 512, tn), lambda bi, i, j: (layer, 0, j)),
            pl.BlockSpec((None, 512, tn), lambda bi, i, j: (layer, 1, j)),
            pl.BlockSpec((None, 1024, tn), lambda bi, i, j: (layer, 1, j)),
        ],
        out_specs=pl.BlockSpec((1, tm, tn), lambda bi, i, j: (bi, i, j)),
        out_shape=jax.ShapeDtypeStruct((b, t, d), BF16),
        compiler_params=_cparams("parallel", "parallel", "arbitrary"),
        name="branch_merge",
    )(ya, yb, yc, z, z, z, w_branch, w_branch, w_branch)


def _resid_kernel(m_ref, w_ref, x_ref, gt_ref, o_ref):
    o_ref[0] = x_ref[0] + gt_ref[0] * jnp.dot(m_ref[0], w_ref[...], preferred_element_type=F32)


def _out_proj_residual(m, w_out, layer, x, gt, tm, tn):
    b, t, d = x.shape
    return pl.pallas_call(
        _resid_kernel,
        grid=(b, t // tm, d // tn),
        in_specs=[
            pl.BlockSpec((1, tm, d), lambda bi, i, j: (bi, i, 0)),
            pl.BlockSpec((None, d, tn), lambda bi, i, j: (layer, 0, j)),
            pl.BlockSpec((1, tm, tn), lambda bi, i, j: (bi, i, j)),
            pl.BlockSpec((1, 1, tn), lambda bi, i, j: (bi, 0, j)),
        ],
        out_specs=pl.BlockSpec((1, tm, tn), lambda bi, i, j: (bi, i, j)),
        out_shape=jax.ShapeDtypeStruct((b, t, d), F32),
        compiler_params=_cparams("parallel", "parallel", "arbitrary"),
        name="out_proj_residual",
    )(m, w_out, x, gt)


def _router_kernel(*refs, starts):
    n_s = len(starts) - 1
    g_ref, wh_ref, wl_ref, br_ref, h_ref, eid_ref, wt_ref = refs[3 * n_s:]
    i = pl.program_id(0)
    for k in range(n_s):
        x_ref, sc_ref, sh_ref = refs[3 * k:3 * k + 3]

        @pl.when((i >= starts[k]) & (i < starts[k + 1]))
        def _(x_ref=x_ref, sc_ref=sc_ref, sh_ref=sh_ref):
            _route_rows(x_ref, sc_ref, sh_ref, g_ref, wh_ref, wl_ref, br_ref, h_ref, eid_ref, wt_ref)


def _route_rows(x_ref, sc_ref, sh_ref, g_ref, wh_ref, wl_ref, br_ref, h_ref, eid_ref, wt_ref):
    h = _rms(x_ref[0], g_ref[...]) * (1.0 + sc_ref[0]) + sh_ref[0]
    h_ref[...] = h
    hh = h.astype(BF16)
    hl = (h - hh.astype(F32)).astype(BF16)
    logits = (jnp.dot(hh, wh_ref[...], preferred_element_type=F32)
              + jnp.dot(hl, wh_ref[...], preferred_element_type=F32)
              + jnp.dot(hh, wl_ref[...], preferred_element_type=F32)) + br_ref[...]
    lane = lax.broadcasted_iota(jnp.int32, logits.shape, 1)
    lane_f = lane.astype(F32)

    def first_argmax(v, valid):
        vm = jnp.where(valid, v, -jnp.inf)
        mx = jnp.max(vm, axis=-1, keepdims=True)
        idx = jnp.min(jnp.where(valid & (vm == mx), lane_f, 1e9), axis=-1, keepdims=True)
        return mx, idx.astype(jnp.int32)

    is_g = lane < N_GROUPS
    gmax, grp = first_argmax(logits, is_g)
    p_grp = 1.0 / jnp.sum(jnp.where(is_g, jnp.exp(logits - gmax), 0.0), axis=-1, keepdims=True)
    e_lo = N_GROUPS + grp * EXPERTS_PER_GROUP
    in_grp = (lane >= e_lo) & (lane < e_lo + EXPERTS_PER_GROUP)
    v0, i0 = first_argmax(logits, in_grp)
    v1, i1 = first_argmax(logits, in_grp & (lane != i0))
    e1 = jnp.exp(v1 - v0)
    w0 = p_grp / (1.0 + e1)
    w1 = p_grp * e1 / (1.0 + e1)
    eid_ref[...] = jnp.where(lane == 0, i0 - N_GROUPS, jnp.where(lane == 1, i1 - N_GROUPS, 0))
    wt_ref[...] = jnp.where(lane == 0, w0, jnp.where(lane == 1, w1, 0.0))


def _router(xs, scs, shs, g, wr_hi, wr_lo, br, tm):
    d = xs[0].shape[-1]
    starts = [0]
    in_specs, args = [], []
    for x, sc, sh in zip(xs, scs, shs):
        b, t, _ = x.shape
        n_i = t // tm
        s0 = starts[-1]
        starts.append(s0 + b * n_i)

        def local(i, s0=s0, n=b * n_i):
            return jnp.clip(i - s0, 0, n - 1)

        in_specs += [pl.BlockSpec((1, tm, d), lambda i, f=local, n_i=n_i: (f(i) // n_i, f(i) % n_i, 0)),
                     pl.BlockSpec((1, 1, d), lambda i, f=local, n_i=n_i: (f(i) // n_i, 0, 0)),
                     pl.BlockSpec((1, 1, d), lambda i, f=local, n_i=n_i: (f(i) // n_i, 0, 0))]
        args += [x, sc, sh]
    n_all = starts[-1] * tm
    in_specs += [pl.BlockSpec((1, d), lambda i: (0, 0)), pl.BlockSpec((d, 128), lambda i: (0, 0)),
                 pl.BlockSpec((d, 128), lambda i: (0, 0)), pl.BlockSpec((1, 128), lambda i: (0, 0))]
    args += [g.reshape(1, d), wr_hi, wr_lo, br]
    flat = lambda width: pl.BlockSpec((tm, width), lambda i: (i, 0))
    return pl.pallas_call(
        functools.partial(_router_kernel, starts=tuple(starts)),
        grid=(starts[-1],),
        in_specs=in_specs,
        out_specs=[flat(d), flat(128), flat(128)],
        out_shape=[jax.ShapeDtypeStruct((n_all, d), F32), jax.ShapeDtypeStruct((n_all, 128), jnp.int32),
                   jax.ShapeDtypeStruct((n_all, 128), F32)],
        compiler_params=_cparams("parallel"),
        name="moe_router",
    )(*args)


def _row_copy(src_hbm, dst_vmem, sem, src_row, dst_row):
    return pltpu.make_async_copy(src_hbm.at[pl.ds(src_row, 1)], dst_vmem.at[pl.ds(dst_row, 1)], sem)


def _dispatch_kernel(slot_ref, pad_ref, x_ref, o_hbm, stage0, stage1, zrow, sem):
    tm = x_ref.shape[0]
    n_fill = pad_ref.shape[2]
    i = pl.program_id(0)
    n = pl.num_programs(0)
    stages = (stage0, stage1)

    def put(src, src_row, dst_row, s):
        return pltpu.make_async_copy(src.at[pl.ds(src_row, 1)], o_hbm.at[pl.ds(dst_row, 1)], s)

    def wait_step(p):
        def wait(r, c):
            put(stages[p], 0, 0, sem.at[p]).wait()
            return c
        lax.fori_loop(0, 2 * tm + n_fill, wait, 0, unroll=8)

    @pl.when(i == 0)
    def _():
        zrow[...] = jnp.zeros(zrow.shape, zrow.dtype)

    def block(cur):
        stage = stages[cur]
        stage[...] = x_ref[...]
        for r in range(tm):
            put(stage, r, slot_ref[0, 0, 2 * r], sem.at[cur]).start(priority=0)
            put(stage, r, slot_ref[0, 0, 2 * r + 1], sem.at[cur]).start(priority=1)
        for r in range(n_fill):
            put(zrow, 0, pad_ref[0, 0, r], sem.at[cur]).start(priority=r % 2)

        @pl.when(i > 0)
        def _():
            wait_step(1 - cur)

        @pl.when(i == n - 1)
        def _():
            wait_step(cur)

    for parity in range(2):
        @pl.when(i % 2 == parity)
        def _(parity=parity):
            block(parity)


def _dispatch(slot, pad_slots, x_rows, n_out, tm):
    n_tok, d = x_rows.shape
    n = n_tok // tm
    n_fill = pad_slots.shape[0] // n
    return pl.pallas_call(
        _dispatch_kernel,
        grid=(n,),
        in_specs=[pl.BlockSpec((1, 1, 2 * tm), lambda i: (i, 0, 0), memory_space=pltpu.SMEM),
                  pl.BlockSpec((1, 1, n_fill), lambda i: (i, 0, 0), memory_space=pltpu.SMEM),
                  pl.BlockSpec((tm, d), lambda i: (i, 0))],
        out_specs=pl.BlockSpec(memory_space=pl.ANY),
        out_shape=jax.ShapeDtypeStruct((n_out, d), x_rows.dtype),
        scratch_shapes=[pltpu.VMEM((tm, d), x_rows.dtype), pltpu.VMEM((tm, d), x_rows.dtype),
                        pltpu.VMEM((8, d), x_rows.dtype), pltpu.SemaphoreType.DMA((2,))],
        compiler_params=_cparams("arbitrary"),
        name="moe_dispatch",
    )(slot.reshape(n, 1, 2 * tm), pad_slots.reshape(n, 1, n_fill), x_rows)


def _expert_kernel(blk_e_ref, x_ref, w1_ref, w3_ref, w2_ref, o_ref):
    del blk_e_ref
    xb = x_ref[...].astype(BF16)
    h1 = jnp.dot(xb, w1_ref[0], preferred_element_type=F32)
    h3 = jnp.dot(xb, w3_ref[0], preferred_element_type=F32)
    hid = (h1 * jax.nn.sigmoid(h1) * h3).astype(BF16)
    o_ref[...] = jnp.dot(hid, w2_ref[0], preferred_element_type=F32)


def _expert_blocks(blk_e, x_sorted, w1, w3, w2, layer):
    n_blk = blk_e.shape[0]
    tm = MOE_TM
    d = x_sorted.shape[1]
    grid_spec = pltpu.PrefetchScalarGridSpec(
        num_scalar_prefetch=1,
        grid=(n_blk,),
        in_specs=[
            pl.BlockSpec((tm, d), lambda i, e: (i, 0)),
            pl.BlockSpec((None, 1, d, D_EXPERT), lambda i, e: (layer, e[i], 0, 0)),
            pl.BlockSpec((None, 1, d, D_EXPERT), lambda i, e: (layer, e[i], 0, 0)),
            pl.BlockSpec((None, 1, D_EXPERT, d), lambda i, e: (layer, e[i], 0, 0)),
        ],
        out_specs=pl.BlockSpec((tm, d), lambda i, e: (i, 0)),
    )
    return pl.pallas_call(
        _expert_kernel,
        grid_spec=grid_spec,
        out_shape=jax.ShapeDtypeStruct((n_blk * tm, d), F32),
        compiler_params=_cparams("arbitrary"),
        name="moe_experts",
    )(blk_e, x_sorted, w1, w3, w2)


def _combine_kernel(slot_ref, slot_next_ref, y_hbm, x_ref, gt_ref, wt_ref, o_ref, a0, b0, a1, b1, sem):
    tm = a0.shape[0]
    i = pl.program_id(0)
    n = pl.num_programs(0)
    bufs = ((a0, b0), (a1, b1))

    def wait_rows(p):
        def wait(r, c):
            _row_copy(y_hbm, bufs[p][0], sem.at[p], 0, r).wait()
            _row_copy(y_hbm, bufs[p][1], sem.at[p], 0, r).wait()
            return c
        lax.fori_loop(0, tm, wait, 0, unroll=8)

    @pl.when(i == 0)
    def _():
        def issue(r, c):
            _row_copy(y_hbm, a0, sem.at[0], slot_ref[0, 0, 2 * r], r).start()
            _row_copy(y_hbm, b0, sem.at[0], slot_ref[0, 0, 2 * r + 1], r).start()
            return c
        lax.fori_loop(0, tm, issue, 0, unroll=8)

    def block(cur):
        wait_rows(cur)
        nxt_a, nxt_b = bufs[1 - cur]
        for r in range(tm):
            _row_copy(y_hbm, nxt_a, sem.at[1 - cur], slot_next_ref[0, 0, 2 * r], r).start(priority=0)
            _row_copy(y_hbm, nxt_b, sem.at[1 - cur], slot_next_ref[0, 0, 2 * r + 1], r).start(priority=1)
        wt = wt_ref[...]
        f = bufs[cur][0][...] * wt[:, 0:1] + bufs[cur][1][...] * wt[:, 1:2]
        o_ref[...] = x_ref[...] + gt_ref[0] * f

    for parity in range(2):
        @pl.when(i % 2 == parity)
        def _(parity=parity):
            block(parity)

        @pl.when((i == n - 1) & (i % 2 == parity))
        def _(parity=parity):
            wait_rows(1 - parity)


def _combine(slots, y, x, gt, wt, tok0, tm):
    b, t, d = x.shape
    n_i = t // tm
    n = b * n_i
    x2 = x.reshape(b * t, d)
    blk0 = tok0 // tm
    slots3 = slots.reshape(-1, 1, 2 * tm)
    out = pl.pallas_call(
        _combine_kernel,
        grid=(n,),
        in_specs=[
            pl.BlockSpec((1, 1, 2 * tm), lambda i: (blk0 + i, 0, 0), memory_space=pltpu.SMEM),
            pl.BlockSpec((1, 1, 2 * tm), lambda i: (blk0 + jnp.minimum(i + 1, n - 1), 0, 0),
                         memory_space=pltpu.SMEM),
            pl.BlockSpec(memory_space=pl.ANY),
            pl.BlockSpec((tm, d), lambda i: (i, 0)),
            pl.BlockSpec((1, 1, d), lambda i: (i // n_i, 0, 0)),
            pl.BlockSpec((tm, 128), lambda i: (blk0 + i, 0)),
        ],
        out_specs=pl.BlockSpec((tm, d), lambda i: (i, 0)),
        out_shape=jax.ShapeDtypeStruct((b * t, d), F32),
        scratch_shapes=[pltpu.VMEM((tm, d), F32)] * 4 + [pltpu.SemaphoreType.DMA((2,))],
        compiler_params=_cparams("arbitrary"),
        name="moe_combine",
    )(slots3, slots3, y, x2, gt, wt)
    return out.reshape(b, t, d)


def _routing_tables(eid, n_steps):
    m = eid.shape[0]
    tm = MOE_TM
    i32 = jnp.int32
    iota = jnp.arange(m, dtype=i32)
    se, order = lax.sort_key_val(eid, iota)
    onehot = (se[:, None] == jnp.arange(N_EXPERTS, dtype=i32)[None, :]).astype(i32)
    counts = jnp.sum(onehot, axis=0)
    starts = jnp.cumsum(counts) - counts
    pcounts = (counts + tm - 1) // tm * tm
    pends = jnp.cumsum(pcounts)
    pstarts = pends - pcounts
    dest = iota + jnp.sum(onehot * (pstarts - starts)[None, :], axis=1)
    _, slot = lax.sort_key_val(order, dest)
    n_blk = (m + N_EXPERTS * (tm - 1) + tm - 1) // tm
    n_pad = n_blk * tm
    blk_start = jnp.arange(n_blk, dtype=i32) * tm
    blk_e = jnp.minimum(jnp.sum((pends[None, :] <= blk_start[:, None]).astype(i32), axis=1), N_EXPERTS - 1)
    n_empty = n_pad - m
    n_fill = -(-n_empty // n_steps)
    n_fill = (n_fill + 7) // 8 * 8
    seg_len = jnp.concatenate([pcounts - counts, (n_pad - pends[-1:])])
    seg_first = jnp.concatenate([pstarts + counts, pends[-1:]])
    seg_end = jnp.cumsum(seg_len)
    seg_begin = seg_end - seg_len
    j = jnp.arange(n_empty, dtype=i32)
    in_seg = ((seg_begin[None, :] <= j[:, None]) & (j[:, None] < seg_end[None, :])).astype(i32)
    empty = j + jnp.sum(in_seg * (seg_first - seg_begin)[None, :], axis=1)
    n_spare = n_steps * n_fill - n_empty
    fill = jnp.concatenate([empty, n_pad + jnp.arange(n_spare, dtype=i32)])
    return blk_e.astype(i32), slot, fill, n_pad + n_spare


def _moe(xs, scs, shs, gts, norm_g, wr_hi, wr_lo, br, w1, w3, w2, layer):
    h_all, eid_all, wt_all = _router(xs, scs, shs, norm_g, wr_hi, wr_lo, br, 512)
    n_steps = h_all.shape[0] // MOE_DISPATCH_TOKENS
    blk_e, slot, fill, n_rows = _routing_tables(eid_all[:, :TOP_K].reshape(-1), n_steps)
    y = _expert_blocks(blk_e, _dispatch(slot, fill, h_all, n_rows, MOE_DISPATCH_TOKENS), w1, w3, w2, layer)
    outs = []
    tok0 = 0
    for x, gt in zip(xs, gts):
        outs.append(_combine(slot, y, x, gt, wt_all, tok0, tm=MOE_COMBINE_TOKENS))
        tok0 += x.shape[0] * x.shape[1]
    return outs


def kernel(x, c, ctx, c_ctx, w_mod, b_mod, norm1_g, norm2_g, w_in, na_q_g, na_k_g, na_rpb, df_q_g, df_k_g, df_lam, df_sub_g, rg_conv_w, rg_conv_b, rg_w_a, rg_b_a, rg_w_x, rg_b_x, rg_lam, w_branch, w_out, w_router_g, b_router_g, w_router_e, b_router_e, w1, w3, w2):
    B, S, D = x.shape
    C = ctx.shape[1]
    rope_cos, rope_sin = _rope_tables(S)
    na_lo, na_hi = _na_bias_tables(na_rpb)

    pad = (-(B + 1)) % 8
    rows = jnp.concatenate([c, c_ctx[None, :], jnp.zeros((pad, D), F32)], axis=0)
    mod_all = _modulation(rows, w_mod, b_mod)

    w_in_b, w_branch_b, w_out_b = w_in.astype(BF16), w_branch.astype(BF16), w_out.astype(BF16)
    w1_b, w3_b, w2_b = w1.astype(BF16), w3.astype(BF16), w2.astype(BF16)

    xc = ctx.reshape(1, B * C, D)
    for l in range(DEPTH):
        need_ctx = l < DEPTH - 1
        lam_init = 0.8 - 0.6 * float(np.exp(-0.3 * l))
        mod = mod_all[l]
        sh1, sc1, gt1, sh2, sc2, gt2 = [mod[:B, k * D:(k + 1) * D][:, None, :] for k in range(6)]
        csh1, csc1, cgt1, csh2, csc2, cgt2 = [mod[B:B + 1, k * D:(k + 1) * D][:, None, :] for k in range(6)]

        zx = _norm_mod_matmul(x, norm1_g[l], sc1, sh1, w_in_b, l, IN_COLS, tm=1024, tn=1024)
        n_cc = IN_COLS if need_ctx else KV_COLS
        zc = _norm_mod_matmul(xc, norm1_g[l], csc1, csh1, w_in_b, l, n_cc, tm=1024, tn=1024).reshape(B, C, n_cc)

        y_a = _na_attention(zx, zc, na_lo[l], na_hi[l], na_q_g[l], na_k_g[l])

        lp = df_lam[l]
        lam = jnp.exp(jnp.sum(lp[0] * lp[1])) - jnp.exp(jnp.sum(lp[2] * lp[3])) + lam_init
        y_b = _df_attention(lam, zx, zx, zc, rope_cos, rope_sin, df_q_g[l], df_k_g[l], df_sub_g[l], 1.0 - lam_init)

        y_c, y_cc = _rg_lru(zx, zc, rg_conv_w[l], rg_conv_b[l], rg_w_a[l].astype(BF16), rg_b_a[l],
                            rg_w_x[l].astype(BF16), rg_b_x[l], rg_lam[l], need_ctx)

        m_x = _branch_merge(y_a, y_b, y_c, zx, w_branch_b, l, tm=1024, tn=1024)
        x = _out_proj_residual(m_x, w_out_b, l, x, gt1, tm=1024, tn=1024)
        if need_ctx:
            y_ac = _na_ctx_attention(zc, na_q_g[l], na_k_g[l])
            y_bc = _df_attention(lam, zc, None, zc, None, None, df_q_g[l], df_k_g[l], df_sub_g[l], 1.0 - lam_init)
            m_c = _branch_merge(y_ac, y_bc, y_cc, zc, w_branch_b, l, tm=C, tn=1024)
            xc = _out_proj_residual(m_c.reshape(1, B * C, D), w_out_b, l, xc, cgt1, tm=1024, tn=1024)

        wr = jnp.concatenate([w_router_g[l], w_router_e[l],
                              jnp.zeros((D, 128 - N_GROUPS - N_EXPERTS), F32)], axis=1)
        wr_hi = wr.astype(BF16)
        wr_lo = (wr - wr_hi.astype(F32)).astype(BF16)
        br = jnp.concatenate([b_router_g[l], b_router_e[l],
                              jnp.zeros((128 - N_GROUPS - N_EXPERTS,), F32)]).reshape(1, 128)
        if need_ctx:
            xc, x = _moe([xc, x], [csc2, sc2], [csh2, sh2], [cgt2, gt2], norm2_g[l],
                         wr_hi, wr_lo, br, w1_b, w3_b, w2_b, l)
        else:
            (x,) = _moe([x], [sc2], [sh2], [gt2], norm2_g[l], wr_hi, wr_lo, br, w1_b, w3_b, w2_b, l)
    return x
```

```python
import functools

import jax
import jax.numpy as jnp
import numpy as np
from jax import lax
from jax.experimental import pallas as pl
from jax.experimental.pallas import tpu as pltpu

F32 = jnp.float32
BF16 = jnp.bfloat16

D_MODEL = 2048
DEPTH = 4
GRID_W = 64
HEAD_DIM = 128
N_HEADS = 4
NA_WIN_R = 8
NA_WIN_C = 16
NA_QCB = 16
NA_KCB = 32
DF_DIM = 64
RG_WIDTH = 1024
RG_BLOCKS = 8
RG_BW = 128
RG_C = 8.0
N_GROUPS = 4
EXPERTS_PER_GROUP = 8
N_EXPERTS = 32
TOP_K = 2
D_EXPERT = 512
ROPE_BASE = 10000.0
EPS = 1e-6
NEG = -1e30
LOG2_E = 1.4426950408889634

COL_NA_K, COL_NA_V, COL_DF_K, COL_DF_V, COL_RG_X = 0, 512, 1024, 1536, 2048
COL_NA_Q, COL_DF_Q, COL_RG_G, COL_GATE = 3072, 3584, 4096, 5120
KV_COLS = 3072
MIX_COLS = 5120
IN_COLS = MIX_COLS + 3 * D_MODEL

VMEM_LIMIT_BYTES = 56 * 1024 * 1024

NA_ROWS_PER_STEP = 4
NA_KEY_ROWS = NA_ROWS_PER_STEP + NA_WIN_R
DF_QBLK = 1024
DF_KCHUNK = 768
MOE_TM = 256
MOE_DISPATCH_TOKENS = 512
MOE_COMBINE_TOKENS = 512
SCAN_CHUNKS = 8
SCAN_PAD_ROWS = 8


def _cparams(*sem):
    return pltpu.CompilerParams(dimension_semantics=sem, vmem_limit_bytes=VMEM_LIMIT_BYTES)


def _rms(x, g):
    x = x.astype(F32)
    return x * lax.rsqrt(jnp.mean(x * x, axis=-1, keepdims=True) + EPS) * g


def _mod_kernel(s_ref, w_ref, b_ref, o_ref):
    s = s_ref[...]
    a = (s * jax.nn.sigmoid(s)).astype(BF16)
    o_ref[0] = jnp.dot(a, w_ref[0].astype(BF16), preferred_element_type=F32) + b_ref[0]


def _modulation(rows, w_mod, b_mod, tn=1024):
    n_l, d, n = w_mod.shape
    r = rows.shape[0]
    return pl.pallas_call(
        _mod_kernel,
        grid=(n_l, n // tn),
        in_specs=[
            pl.BlockSpec((r, d), lambda l, j: (0, 0)),
            pl.BlockSpec((1, d, tn), lambda l, j: (l, 0, j)),
            pl.BlockSpec((1, 1, tn), lambda l, j: (l, 0, j)),
        ],
        out_specs=pl.BlockSpec((1, r, tn), lambda l, j: (l, 0, j)),
        out_shape=jax.ShapeDtypeStruct((n_l, r, n), F32),
        compiler_params=_cparams("parallel", "parallel"),
        name="modulation",
    )(rows, w_mod, b_mod.reshape(n_l, 1, n))


def _nmm_kernel(x_ref, g_ref, sc_ref, sh_ref, w_ref, o_ref, h_ref):
    @pl.when(pl.program_id(2) == 0)
    def _():
        h = _rms(x_ref[0], g_ref[...]) * (1.0 + sc_ref[0]) + sh_ref[0]
        h_ref[...] = h.astype(BF16)

    o_ref[0] = jnp.dot(h_ref[...], w_ref[...], preferred_element_type=F32).astype(o_ref.dtype)


def _norm_mod_matmul(x, g, sc, sh, w, layer, n, tm, tn):
    b, t, d = x.shape
    return pl.pallas_call(
        _nmm_kernel,
        grid=(b, t // tm, n // tn),
        in_specs=[
            pl.BlockSpec((1, tm, d), lambda bi, i, j: (bi, i, 0)),
            pl.BlockSpec((1, d), lambda bi, i, j: (0, 0)),
            pl.BlockSpec((1, 1, d), lambda bi, i, j: (bi, 0, 0)),
            pl.BlockSpec((1, 1, d), lambda bi, i, j: (bi, 0, 0)),
            pl.BlockSpec((None, d, tn), lambda bi, i, j: (layer, 0, j)),
        ],
        out_specs=pl.BlockSpec((1, tm, tn), lambda bi, i, j: (bi, i, j)),
        out_shape=jax.ShapeDtypeStruct((b, t, n), BF16),
        scratch_shapes=[pltpu.VMEM((tm, d), BF16)],
        compiler_params=_cparams("parallel", "parallel", "arbitrary"),
        name="norm_mod_proj",
    )(x, g.reshape(1, d), sc, sh, w)


def _na_window_start(r0):
    lower = jnp.clip(r0 - NA_WIN_R // 2, 0, (2048 // GRID_W) - NA_WIN_R)
    return jnp.minimum(lower, (2048 // GRID_W) - NA_KEY_ROWS)


def _na_bias_tables(rpb):
    w = GRID_W
    qc = np.arange(w)
    kc = np.arange(w)
    qwin = np.clip(qc - NA_WIN_C // 2, 0, w - NA_WIN_C)
    kcol_start = np.clip((qc // NA_QCB) * NA_QCB - (NA_KCB - NA_QCB) // 2, 0, w - NA_KCB)
    col_ok = (kc[None, :] >= qwin[:, None]) & (kc[None, :] < qwin[:, None] + NA_WIN_C)
    col_ok &= (kc[None, :] >= kcol_start[:, None]) & (kc[None, :] < kcol_start[:, None] + NA_KCB)
    edge = w - NA_WIN_C
    lead = rpb.shape[:-1]
    ext = jnp.concatenate([jnp.repeat(rpb[..., :1], edge, axis=-1), rpb, jnp.repeat(rpb[..., -1:], edge, axis=-1),
                           jnp.zeros(lead + (1,), F32)], axis=-1)
    skew = jnp.tile(ext, (1,) * len(lead) + (w,))[..., :w * (2 * w - 1)].reshape(lead + (w, 2 * w - 1))
    toep = jnp.where(col_ok, skew[..., w - 1:], NEG)
    toep = jnp.concatenate([toep, jnp.full(toep.shape[:-3] + (1, w, w), NEG, F32)], axis=-3)
    zeros = jnp.zeros_like(toep)
    return jnp.concatenate([toep, zeros], axis=-1), jnp.concatenate([zeros, toep], axis=-1)


def _na_bias_blocks(n_rows):
    rb, kw = NA_ROWS_PER_STEP, NA_KEY_ROWS
    rows = np.arange(n_rows)
    row_start = np.clip(rows - NA_WIN_R // 2, 0, n_rows - NA_WIN_R)
    n_steps = n_rows // rb
    blk = np.full((n_steps, rb, kw), 2 * NA_WIN_R - 1, np.int32)
    for s in range(n_steps):
        ws = min(int(np.clip(s * rb - NA_WIN_R // 2, 0, n_rows - NA_WIN_R)), n_rows - kw)
        for j in range(rb):
            r = s * rb + j
            for i in range(kw):
                if row_start[r] <= ws + i < row_start[r] + NA_WIN_R:
                    blk[s, j, i] = ws + i - r + (NA_WIN_R - 1)
    return blk.reshape(-1)


def _na_kernel(blk_ref, q_ref, k_ref, v_ref, kc_ref, vc_ref, lo_ref, hi_ref, gq_ref, gk_ref, o_ref,
               kn, vn, knc, vnc):
    step = pl.program_id(1)

    def bias(h):
        rows = []
        for j in range(NA_ROWS_PER_STEP):
            base = (step * NA_ROWS_PER_STEP + j) * NA_KEY_ROWS
            rows.append(jnp.concatenate(
                [lo_ref[h, blk_ref[base + i]] + hi_ref[h, blk_ref[base + i + 1]] for i in range(0, NA_KEY_ROWS, 2)],
                axis=-1))
        return jnp.concatenate(rows, axis=0)

    @pl.when(step == 0)
    def _():
        for h in range(N_HEADS):
            sl = slice(h * HEAD_DIM, (h + 1) * HEAD_DIM)
            kn[:, sl] = _rms(k_ref[0, :, sl], gk_ref[...]).astype(BF16)
            knc[:, sl] = _rms(kc_ref[0, :, sl], gk_ref[...]).astype(BF16)
        vn[...] = v_ref[0].astype(BF16)
        vnc[...] = vc_ref[0].astype(BF16)

    ws = _na_window_start(step * NA_ROWS_PER_STEP)
    start = pl.multiple_of(ws * GRID_W, GRID_W)
    nk = NA_KEY_ROWS * GRID_W
    scale = HEAD_DIM ** -0.5
    nt = (((1,), (1,)), ((), ()))
    for h in range(N_HEADS):
        sl = slice(h * HEAD_DIM, (h + 1) * HEAD_DIM)
        qn = (_rms(q_ref[0, :, sl], gq_ref[...]) * scale).astype(BF16)
        s_loc = lax.dot_general(qn, kn[pl.ds(start, nk), sl], nt, preferred_element_type=F32) + bias(h)
        s_ctx = lax.dot_general(qn, knc[:, sl], nt, preferred_element_type=F32)
        m = jnp.maximum(jnp.max(s_loc, axis=-1, keepdims=True), jnp.max(s_ctx, axis=-1, keepdims=True))
        p_loc = jnp.exp(s_loc - m)
        p_ctx = jnp.exp(s_ctx - m)
        l = jnp.sum(p_loc, axis=-1, keepdims=True) + jnp.sum(p_ctx, axis=-1, keepdims=True)
        o = jnp.dot(p_loc.astype(BF16), vn[pl.ds(start, nk), sl], preferred_element_type=F32)
        o = o + jnp.dot(p_ctx.astype(BF16), vnc[:, sl], preferred_element_type=F32)
        o_ref[0, :, sl] = (o * (1.0 / l)).astype(BF16)


def _na_attention(zx, zc, bias_lo, bias_hi, gq, gk):
    b, s, _ = zx.shape
    c = zc.shape[1]
    w = N_HEADS * HEAD_DIM
    tq = NA_ROWS_PER_STEP * GRID_W
    blk = jnp.asarray(_na_bias_blocks(s // GRID_W))
    table = pl.BlockSpec(bias_lo.shape, lambda bi, i, blk: (0, 0, 0, 0))
    grid_spec = pltpu.PrefetchScalarGridSpec(
        num_scalar_prefetch=1,
        grid=(b, s // tq),
        in_specs=[
            pl.BlockSpec((1, tq, w), lambda bi, i, blk: (bi, i, COL_NA_Q // w)),
            pl.BlockSpec((1, s, w), lambda bi, i, blk: (bi, 0, COL_NA_K // w)),
            pl.BlockSpec((1, s, w), lambda bi, i, blk: (bi, 0, COL_NA_V // w)),
            pl.BlockSpec((1, c, w), lambda bi, i, blk: (bi, 0, COL_NA_K // w)),
            pl.BlockSpec((1, c, w), lambda bi, i, blk: (bi, 0, COL_NA_V // w)),
            table, table,
            pl.BlockSpec((1, HEAD_DIM), lambda bi, i, blk: (0, 0)),
            pl.BlockSpec((1, HEAD_DIM), lambda bi, i, blk: (0, 0)),
        ],
        out_specs=pl.BlockSpec((1, tq, w), lambda bi, i, blk: (bi, i, 0)),
        scratch_shapes=[pltpu.VMEM((s, w), BF16), pltpu.VMEM((s, w), BF16),
                        pltpu.VMEM((c, w), BF16), pltpu.VMEM((c, w), BF16)],
    )
    return pl.pallas_call(
        _na_kernel,
        grid_spec=grid_spec,
        out_shape=jax.ShapeDtypeStruct((b, s, w), BF16),
        compiler_params=_cparams("parallel", "arbitrary"),
        name="na_attention",
    )(blk, zx, zx, zx, zc, zc, bias_lo, bias_hi, gq.reshape(1, HEAD_DIM), gk.reshape(1, HEAD_DIM))


def _na_ctx_kernel(q_ref, k_ref, v_ref, gq_ref, gk_ref, o_ref):
    scale = HEAD_DIM ** -0.5
    nt = (((1,), (1,)), ((), ()))
    for h in range(N_HEADS):
        sl = slice(h * HEAD_DIM, (h + 1) * HEAD_DIM)
        qn = (_rms(q_ref[0, :, sl], gq_ref[...]) * scale).astype(BF16)
        kn = _rms(k_ref[0, :, sl], gk_ref[...]).astype(BF16)
        s = lax.dot_general(qn, kn, nt, preferred_element_type=F32)
        p = jnp.exp(s - jnp.max(s, axis=-1, keepdims=True))
        l = jnp.sum(p, axis=-1, keepdims=True)
        o = jnp.dot(p.astype(BF16), v_ref[0, :, sl].astype(BF16), preferred_element_type=F32)
        o_ref[0, :, sl] = (o * (1.0 / l)).astype(BF16)


def _na_ctx_attention(zc, gq, gk):
    b, c, _ = zc.shape
    w = N_HEADS * HEAD_DIM
    return pl.pallas_call(
        _na_ctx_kernel,
        grid=(b,),
        in_specs=[
            pl.BlockSpec((1, c, w), lambda bi: (bi, 0, COL_NA_Q // w)),
            pl.BlockSpec((1, c, w), lambda bi: (bi, 0, COL_NA_K // w)),
            pl.BlockSpec((1, c, w), lambda bi: (bi, 0, COL_NA_V // w)),
            pl.BlockSpec((1, HEAD_DIM), lambda bi: (0, 0)),
            pl.BlockSpec((1, HEAD_DIM), lambda bi: (0, 0)),
        ],
        out_specs=pl.BlockSpec((1, c, w), lambda bi: (bi, 0, 0)),
        out_shape=jax.ShapeDtypeStruct((b, c, w), BF16),
        compiler_params=_cparams("parallel"),
        name="na_ctx_attention",
    )(zc, zc, zc, gq.reshape(1, HEAD_DIM), gk.reshape(1, HEAD_DIM))


def _rope_tables(n_tok):
    t = np.arange(n_tok)
    pos = np.stack([t // GRID_W, t % GRID_W], axis=0).astype(np.float32)
    n_freq = DF_DIM // 4
    inv = (np.float32(ROPE_BASE) ** (-np.arange(n_freq, dtype=np.float32) / n_freq)).astype(np.float32)
    ang = jnp.asarray(pos[:, :, None] * inv)
    cos, sin = jnp.cos(ang), jnp.sin(ang)
    cos64 = jnp.concatenate([cos[0], cos[0], cos[1], cos[1]], axis=-1)
    sin64 = jnp.concatenate([-sin[0], sin[0], -sin[1], sin[1]], axis=-1)
    return jnp.concatenate([cos64, cos64], axis=-1), jnp.concatenate([sin64, sin64], axis=-1)


def _df_kernel(lam_ref, q_ref, *refs, n_x, n_c, rope, out_scale):
    if n_x:
        kx_ref, vx_ref, kc_ref, vc_ref, cq_ref, sq_ref, ck_ref, sk_ref, gq_ref, gk_ref, gs_ref, o_ref, kn, vn = refs
    else:
        kc_ref, vc_ref, gq_ref, gk_ref, gs_ref, o_ref, kn, vn = refs
    lane = lax.broadcasted_iota(jnp.int32, (1, 2 * DF_DIM), 1)
    lo = lane < DF_DIM
    first = (lane % (DF_DIM // 2)) < (DF_DIM // 4)

    def norm64(x, g):
        x = x.astype(F32)
        x2 = x * x
        s0 = jnp.sum(jnp.where(lo, x2, 0.0), axis=-1, keepdims=True)
        s1 = jnp.sum(jnp.where(lo, 0.0, x2), axis=-1, keepdims=True)
        ms = jnp.where(lo, s0, s1) * (1.0 / DF_DIM)
        return x * lax.rsqrt(ms + EPS) * g

    def rot(x, cos, sin):
        partner = jnp.where(first, pltpu.roll(x, 2 * DF_DIM - DF_DIM // 4, 1), pltpu.roll(x, DF_DIM // 4, 1))
        return x * cos + partner * sin

    @pl.when(pl.program_id(2) == 0)
    def _():
        if n_x:
            kx = rot(norm64(kx_ref[0], gk_ref[...]), ck_ref[...], sk_ref[...])
            kn[0:n_x, :] = kx.astype(BF16)
            vn[0:n_x, :] = vx_ref[0].astype(BF16)
        kn[n_x:n_x + n_c, :] = norm64(kc_ref[0], gk_ref[...]).astype(BF16)
        vn[n_x:n_x + n_c, :] = vc_ref[0].astype(BF16)

    q = norm64(q_ref[0], gq_ref[...])
    if rope:
        q = rot(q, cq_ref[...], sq_ref[...])
    q = q * (DF_DIM ** -0.5 * LOG2_E)
    nt = (((1,), (1,)), ((), ()))

    n_k = n_x + n_c
    qs = (jnp.where(lo, q, 0.0).astype(BF16), jnp.where(lo, 0.0, q).astype(BF16))
    tq = q.shape[0]
    ms = [jnp.full((tq, 1), -jnp.inf, F32)] * 2
    ls = [jnp.zeros((tq, 1), F32)] * 2
    accs = [jnp.zeros((tq, 2 * DF_DIM), F32)] * 2
    for c0 in range(0, n_k, DF_KCHUNK):
        c1 = min(c0 + DF_KCHUNK, n_k)
        for mi in range(2):
            s = lax.dot_general(qs[mi], kn[c0:c1, :], nt, preferred_element_type=F32)
            m_new = jnp.maximum(ms[mi], jnp.max(s, axis=-1, keepdims=True))
            alpha = jnp.exp2(ms[mi] - m_new)
            p = jnp.exp2(s - m_new)
            ls[mi] = alpha * ls[mi] + jnp.sum(p, axis=-1, keepdims=True)
            accs[mi] = alpha * accs[mi] + jnp.dot(p.astype(BF16), vn[c0:c1, :], preferred_element_type=F32)
            ms[mi] = m_new
    o = accs[0] * (1.0 / ls[0]) - accs[1] * (lam_ref[0] / ls[1])
    o_ref[0] = (_rms(o, gs_ref[...]) * out_scale).astype(BF16)


def _df_attention(lam, zq, zx, zc, cos, sin, gq, gk, gs, out_scale):
    b, t, _ = zq.shape
    c = zc.shape[1]
    n_x = 0 if zx is None else zx.shape[1]
    w = 2 * DF_DIM
    tq = min(DF_QBLK, t)
    gq2 = jnp.concatenate([gq, gq]).reshape(1, w)
    gk2 = jnp.concatenate([gk, gk]).reshape(1, w)
    col = lambda base: (lambda bi, h, i: (bi, 0, base // w + h))
    vec = pl.BlockSpec((1, w), lambda bi, h, i: (0, 0))
    in_specs = [pl.BlockSpec(memory_space=pltpu.SMEM),
                pl.BlockSpec((1, tq, w), lambda bi, h, i: (bi, i, COL_DF_Q // w + h))]
    args = [lam.reshape(1), zq]
    if n_x:
        in_specs += [pl.BlockSpec((1, n_x, w), col(COL_DF_K)), pl.BlockSpec((1, n_x, w), col(COL_DF_V))]
        args += [zx, zx]
    in_specs += [pl.BlockSpec((1, c, w), col(COL_DF_K)), pl.BlockSpec((1, c, w), col(COL_DF_V))]
    args += [zc, zc]
    if n_x:
        in_specs += [pl.BlockSpec((tq, w), lambda bi, h, i: (i, 0)), pl.BlockSpec((tq, w), lambda bi, h, i: (i, 0)),
                     pl.BlockSpec((n_x, w), lambda bi, h, i: (0, 0)), pl.BlockSpec((n_x, w), lambda bi, h, i: (0, 0))]
        args += [cos, sin, cos, sin]
    in_specs += [vec, vec, vec]
    args += [gq2, gk2, gs.reshape(1, w)]
    return pl.pallas_call(
        functools.partial(_df_kernel, n_x=n_x, n_c=c, rope=bool(n_x), out_scale=out_scale),
        grid=(b, N_HEADS, t // tq),
        in_specs=in_specs,
        out_specs=pl.BlockSpec((1, tq, w), lambda bi, h, i: (bi, i, h)),
        out_shape=jax.ShapeDtypeStruct((b, t, N_HEADS * w), BF16),
        scratch_shapes=[pltpu.VMEM((n_x + c, w), BF16), pltpu.VMEM((n_x + c, w), BF16)],
        compiler_params=_cparams("parallel", "parallel", "arbitrary"),
        name="diff_attention" if n_x else "diff_ctx_attention",
    )(*args)


def _rg_kernel(*refs, n_x, n_c, ctx_out):
    if ctx_out:
        (ux_ref, uc_ref, gx_ref, gc_ref, cw_ref, cb_ref, wa_ref, ba_ref, wx_ref, bx_ref, lam_ref,
         ox_ref, oc_ref, a_s, b_s, p_s) = refs
    else:
        (ux_ref, uc_ref, gx_ref, cw_ref, cb_ref, wa_ref, ba_ref, wx_ref, bx_ref, lam_ref,
         ox_ref, a_s, b_s, p_s) = refs
    n_t = n_x + n_c
    clen = n_t // SCAN_CHUNKS
    cstride = clen + SCAN_PAD_ROWS

    def conv(z):
        z = z.astype(F32)
        n = z.shape[0]
        t = lax.broadcasted_iota(jnp.int32, (n, 1), 0)
        zm2 = jnp.where(t >= 2, pltpu.roll(z, 2, 0), 0.0)
        zm1 = jnp.where(t >= 1, pltpu.roll(z, 1, 0), 0.0)
        zp1 = jnp.where(t < n - 1, pltpu.roll(z, n - 1, 0), 0.0)
        return (zm2 * cw_ref[0:1, :] + zm1 * cw_ref[1:2, :] + z * cw_ref[2:3, :] + zp1 * cw_ref[3:4, :]
                + cb_ref[...])

    def pieces(t0, n):
        out, t = [], t0
        while t < t0 + n:
            ch = t // clen
            stop = min((ch + 1) * clen, t0 + n)
            out.append((t - t0, ch * cstride + (t - ch * clen), stop - t))
            t = stop
        return out

    def put(ref, d, t0, val):
        for off, row, ln in pieces(t0, val.shape[0]):
            ref[d, row:row + ln, :] = val[off:off + ln]

    def get(ref, d, t0, n):
        return jnp.concatenate([ref[d, row:row + ln, :] for _, row, ln in pieces(t0, n)], axis=0)

    def coeffs(u, d, t0):
        ub = u.astype(BF16)
        r = jax.nn.sigmoid(jnp.dot(ub, wa_ref[d, 0], preferred_element_type=F32) + ba_ref[d:d + 1, :])
        gi = jax.nn.sigmoid(jnp.dot(ub, wx_ref[d, 0], preferred_element_type=F32) + bx_ref[d:d + 1, :])
        log_a = (-RG_C) * r * jax.nn.softplus(-lam_ref[d:d + 1, :])
        a = jnp.exp(log_a)
        put(a_s, d, t0, a)
        put(b_s, d, t0, jnp.sqrt(-jnp.tanh(log_a) * (a * a + 1.0)) * (gi * u))

    u_c = conv(uc_ref[0])
    coeffs(u_c, 0, 0)
    coeffs(u_c, 1, n_x)
    u_x = conv(ux_ref[0])
    coeffs(u_x, 0, n_c)
    coeffs(u_x, 1, 0)

    def step(tau, carry):
        h_f, p_f, h_r, p_r = carry
        i_f = pl.ds(tau, SCAN_CHUNKS, stride=cstride)
        i_r = pl.ds(clen - 1 - tau, SCAN_CHUNKS, stride=cstride)
        a_f = a_s[0, i_f, :]
        a_r = a_s[1, i_r, :]
        h_f = a_f * h_f + b_s[0, i_f, :]
        h_r = a_r * h_r + b_s[1, i_r, :]
        p_f = a_f * p_f
        p_r = a_r * p_r
        b_s[0, i_f, :] = h_f
        b_s[1, i_r, :] = h_r
        p_s[0, i_f, :] = p_f
        p_s[1, i_r, :] = p_r
        return h_f, p_f, h_r, p_r

    zeros = jnp.zeros((SCAN_CHUNKS, RG_BW), F32)
    ones = jnp.ones((SCAN_CHUNKS, RG_BW), F32)
    h_f, p_f, h_r, p_r = lax.fori_loop(0, clen, step, (zeros, ones, zeros, ones), unroll=2)

    carry = jnp.zeros((1, RG_BW), F32)
    for ch in range(1, SCAN_CHUNKS):
        carry = p_f[ch - 1:ch, :] * carry + h_f[ch - 1:ch, :]
        rows = slice(ch * cstride, ch * cstride + clen)
        b_s[0, rows, :] = b_s[0, rows, :] + p_s[0, rows, :] * carry
    carry = jnp.zeros((1, RG_BW), F32)
    for ch in range(SCAN_CHUNKS - 2, -1, -1):
        carry = p_r[ch + 1:ch + 2, :] * carry + h_r[ch + 1:ch + 2, :]
        rows = slice(ch * cstride, ch * cstride + clen)
        b_s[1, rows, :] = b_s[1, rows, :] + p_s[1, rows, :] * carry

    gx = jax.nn.gelu(gx_ref[0].astype(F32), approximate=True)
    ox_ref[0] = ((get(b_s, 0, n_c, n_x) + get(b_s, 1, 0, n_x)) * gx).astype(BF16)
    if ctx_out:
        gc = jax.nn.gelu(gc_ref[0].astype(F32), approximate=True)
        oc_ref[0] = ((get(b_s, 0, 0, n_c) + get(b_s, 1, n_x, n_c)) * gc).astype(BF16)


def _rg_lru(zx, zc, conv_w, conv_b, w_a, b_a, w_x, b_x, lam, ctx_out):
    b, n_x, _ = zx.shape
    n_c = zc.shape[1]
    bw = RG_BW
    col = lambda base: (lambda bi, n: (bi, 0, base // bw + n))
    vec2 = pl.BlockSpec((2, bw), lambda bi, n: (0, n))
    wspec = pl.BlockSpec((2, 1, bw, bw), lambda bi, n: (0, n, 0, 0))
    in_specs = [pl.BlockSpec((1, n_x, bw), col(COL_RG_X)), pl.BlockSpec((1, n_c, bw), col(COL_RG_X)),
                pl.BlockSpec((1, n_x, bw), col(COL_RG_G))]
    args = [zx, zc, zx]
    if ctx_out:
        in_specs.append(pl.BlockSpec((1, n_c, bw), col(COL_RG_G)))
        args.append(zc)
    in_specs += [pl.BlockSpec((4, bw), lambda bi, n: (0, n)), pl.BlockSpec((1, bw), lambda bi, n: (0, n)),
                 wspec, vec2, wspec, vec2, vec2]
    args += [conv_w, conv_b.reshape(1, RG_WIDTH), w_a, b_a, w_x, b_x, lam]
    out_specs = [pl.BlockSpec((1, n_x, bw), lambda bi, n: (bi, 0, n))]
    out_shape = [jax.ShapeDtypeStruct((b, n_x, RG_WIDTH), BF16)]
    if ctx_out:
        out_specs.append(pl.BlockSpec((1, n_c, bw), lambda bi, n: (bi, 0, n)))
        out_shape.append(jax.ShapeDtypeStruct((b, n_c, RG_WIDTH), BF16))
    n_s = SCAN_CHUNKS * ((n_x + n_c) // SCAN_CHUNKS + SCAN_PAD_ROWS)
    outs = pl.pallas_call(
        functools.partial(_rg_kernel, n_x=n_x, n_c=n_c, ctx_out=ctx_out),
        grid=(b, RG_BLOCKS),
        in_specs=in_specs,
        out_specs=out_specs,
        out_shape=out_shape,
        scratch_shapes=[pltpu.VMEM((2, n_s, bw), F32), pltpu.VMEM((2, n_s, bw), F32), pltpu.VMEM((2, n_s, bw), F32)],
        compiler_params=_cparams("parallel", "parallel"),
        name="rg_lru",
    )(*args)
    return outs if ctx_out else (outs[0], None)


def _merge_kernel(ya_ref, yb_ref, yc_ref, ga_ref, gb_ref, gc_ref, wa_ref, wb_ref, wc_ref, o_ref):
    gate = lambda ref: jax.nn.sigmoid(ref[0].astype(F32))
    m = gate(ga_ref) * jnp.dot(ya_ref[0], wa_ref[...], preferred_element_type=F32)
    m = m + gate(gb_ref) * jnp.dot(yb_ref[0], wb_ref[...], preferred_element_type=F32)
    m = m + gate(gc_ref) * jnp.dot(yc_ref[0], wc_ref[...], preferred_element_type=F32)
    o_ref[0] = m.astype(BF16)


def _branch_merge(ya, yb, yc, z, w_branch, layer, tm, tn):
    b, t, _ = ya.shape
    d = D_MODEL
    gate = lambda k: (lambda bi, i, j: (bi, i, (COL_GATE + k * d) // tn + j))
    return pl.pallas_call(
        _merge_kernel,
        grid=(b, t // tm, d // tn),
        in_specs=[
            pl.BlockSpec((1, tm, 512), lambda bi, i, j: (bi, i, 0)),
            pl.BlockSpec((1, tm, 512), lambda bi, i, j: (bi, i, 0)),
            pl.BlockSpec((1, tm, 1024), lambda bi, i, j: (bi, i, 0)),
            pl.BlockSpec((1, tm, tn), gate(0)),
            pl.BlockSpec((1, tm, tn), gate(1)),
            pl.BlockSpec((1, tm, tn), gate(2)),
            pl.BlockSpec((None, 512, tn), lambda bi, i, j: (layer, 0, j)),
            pl.BlockSpec((None, 512, tn), lambda bi, i, j: (layer, 1, j)),
            pl.BlockSpec((None, 1024, tn), lambda bi, i, j: (layer, 1, j)),
        ],
        out_specs=pl.BlockSpec((1, tm, tn), lambda bi, i, j: (bi, i, j)),
        out_shape=jax.ShapeDtypeStruct((b, t, d), BF16),
        compiler_params=_cparams("parallel", "parallel", "arbitrary"),
        name="branch_merge",
    )(ya, yb, yc, z, z, z, w_branch, w_branch, w_branch)


def _resid_kernel(m_ref, w_ref, x_ref, gt_ref, o_ref):
    o_ref[0] = x_ref[0] + gt_ref[0] * jnp.dot(m_ref[0], w_ref[...], preferred_element_type=F32)


def _out_proj_residual(m, w_out, layer, x, gt, tm, tn):
    b, t, d = x.shape
    return pl.pallas_call(
        _resid_kernel,
        grid=(b, t // tm, d // tn),
        in_specs=[
            pl.BlockSpec((1, tm, d), lambda bi, i, j: (bi, i, 0)),
            pl.BlockSpec((None, d, tn), lambda bi, i, j: (layer, 0, j)),
            pl.BlockSpec((1, tm, tn), lambda bi, i, j: (bi, i, j)),
            pl.BlockSpec((1, 1, tn), lambda bi, i, j: (bi, 0, j)),
        ],
        out_specs=pl.BlockSpec((1, tm, tn), lambda bi, i, j: (bi, i, j)),
        out_shape=jax.ShapeDtypeStruct((b, t, d), F32),
        compiler_params=_cparams("parallel", "parallel", "arbitrary"),
        name="out_proj_residual",
    )(m, w_out, x, gt)


def _router_kernel(*refs, starts):
    n_s = len(starts) - 1
    g_ref, wh_ref, wl_ref, br_ref, h_ref, eid_ref, wt_ref = refs[3 * n_s:]
    i = pl.program_id(0)
    for k in range(n_s):
        x_ref, sc_ref, sh_ref = refs[3 * k:3 * k + 3]

        @pl.when((i >= starts[k]) & (i < starts[k + 1]))
        def _(x_ref=x_ref, sc_ref=sc_ref, sh_ref=sh_ref):
            _route_rows(x_ref, sc_ref, sh_ref, g_ref, wh_ref, wl_ref, br_ref, h_ref, eid_ref, wt_ref)


def _route_rows(x_ref, sc_ref, sh_ref, g_ref, wh_ref, wl_ref, br_ref, h_ref, eid_ref, wt_ref):
    h = _rms(x_ref[0], g_ref[...]) * (1.0 + sc_ref[0]) + sh_ref[0]
    h_ref[...] = h
    hh = h.astype(BF16)
    hl = (h - hh.astype(F32)).astype(BF16)
    logits = (jnp.dot(hh, wh_ref[...], preferred_element_type=F32)
              + jnp.dot(hl, wh_ref[...], preferred_element_type=F32)
              + jnp.dot(hh, wl_ref[...], preferred_element_type=F32)) + br_ref[...]
    lane = lax.broadcasted_iota(jnp.int32, logits.shape, 1)
    lane_f = lane.astype(F32)

    def first_argmax(v, valid):
        vm = jnp.where(valid, v, -jnp.inf)
        mx = jnp.max(vm, axis=-1, keepdims=True)
        idx = jnp.min(jnp.where(valid & (vm == mx), lane_f, 1e9), axis=-1, keepdims=True)
        return mx, idx.astype(jnp.int32)

    is_g = lane < N_GROUPS
    gmax, grp = first_argmax(logits, is_g)
    p_grp = 1.0 / jnp.sum(jnp.where(is_g, jnp.exp(logits - gmax), 0.0), axis=-1, keepdims=True)
    e_lo = N_GROUPS + grp * EXPERTS_PER_GROUP
    in_grp = (lane >= e_lo) & (lane < e_lo + EXPERTS_PER_GROUP)
    v0, i0 = first_argmax(logits, in_grp)
    v1, i1 = first_argmax(logits, in_grp & (lane != i0))
    e1 = jnp.exp(v1 - v0)
    w0 = p_grp / (1.0 + e1)
    w1 = p_grp * e1 / (1.0 + e1)
    eid_ref[...] = jnp.where(lane == 0, i0 - N_GROUPS, jnp.where(lane == 1, i1 - N_GROUPS, 0))
    wt_ref[...] = jnp.where(lane == 0, w0, jnp.where(lane == 1, w1, 0.0))


def _router(xs, scs, shs, g, wr_hi, wr_lo, br, tm):
    d = xs[0].shape[-1]
    starts = [0]
    in_specs, args = [], []
    for x, sc, sh in zip(xs, scs, shs):
        b, t, _ = x.shape
        n_i = t // tm
        s0 = starts[-1]
        starts.append(s0 + b * n_i)

        def local(i, s0=s0, n=b * n_i):
            return jnp.clip(i - s0, 0, n - 1)

        in_specs += [pl.BlockSpec((1, tm, d), lambda i, f=local, n_i=n_i: (f(i) // n_i, f(i) % n_i, 0)),
                     pl.BlockSpec((1, 1, d), lambda i, f=local, n_i=n_i: (f(i) // n_i, 0, 0)),
                     pl.BlockSpec((1, 1, d), lambda i, f=local, n_i=n_i: (f(i) // n_i, 0, 0))]
        args += [x, sc, sh]
    n_all = starts[-1] * tm
    in_specs += [pl.BlockSpec((1, d), lambda i: (0, 0)), pl.BlockSpec((d, 128), lambda i: (0, 0)),
                 pl.BlockSpec((d, 128), lambda i: (0, 0)), pl.BlockSpec((1, 128), lambda i: (0, 0))]
    args += [g.reshape(1, d), wr_hi, wr_lo, br]
    flat = lambda width: pl.BlockSpec((tm, width), lambda i: (i, 0))
    return pl.pallas_call(
        functools.partial(_router_kernel, starts=tuple(starts)),
        grid=(starts[-1],),
        in_specs=in_specs,
        out_specs=[flat(d), flat(128), flat(128)],
        out_shape=[jax.ShapeDtypeStruct((n_all, d), F32), jax.ShapeDtypeStruct((n_all, 128), jnp.int32),
                   jax.ShapeDtypeStruct((n_all, 128), F32)],
        compiler_params=_cparams("parallel"),
        name="moe_router",
    )(*args)


def _row_copy(src_hbm, dst_vmem, sem, src_row, dst_row):
    return pltpu.make_async_copy(src_hbm.at[pl.ds(src_row, 1)], dst_vmem.at[pl.ds(dst_row, 1)], sem)


def _dispatch_kernel(slot_ref, pad_ref, x_ref, o_hbm, stage0, stage1, zrow, sem):
    tm = x_ref.shape[0]
    n_fill = pad_ref.shape[2]
    i = pl.program_id(0)
    n = pl.num_programs(0)
    stages = (stage0, stage1)

    def put(src, src_row, dst_row, s):
        return pltpu.make_async_copy(src.at[pl.ds(src_row, 1)], o_hbm.at[pl.ds(dst_row, 1)], s)

    def wait_step(p):
        def wait(r, c):
            put(stages[p], 0, 0, sem.at[p]).wait()
            return c
        lax.fori_loop(0, 2 * tm + n_fill, wait, 0, unroll=8)

    @pl.when(i == 0)
    def _():
        zrow[...] = jnp.zeros(zrow.shape, zrow.dtype)

    def block(cur):
        stage = stages[cur]
        stage[...] = x_ref[...]
        for r in range(tm):
            put(stage, r, slot_ref[0, 0, 2 * r], sem.at[cur]).start(priority=0)
            put(stage, r, slot_ref[0, 0, 2 * r + 1], sem.at[cur]).start(priority=1)
        for r in range(n_fill):
            put(zrow, 0, pad_ref[0, 0, r], sem.at[cur]).start(priority=r % 2)

        @pl.when(i > 0)
        def _():
            wait_step(1 - cur)

        @pl.when(i == n - 1)
        def _():
            wait_step(cur)

    for parity in range(2):
        @pl.when(i % 2 == parity)
        def _(parity=parity):
            block(parity)


def _dispatch(slot, pad_slots, x_rows, n_out, tm):
    n_tok, d = x_rows.shape
    n = n_tok // tm
    n_fill = pad_slots.shape[0] // n
    return pl.pallas_call(
        _dispatch_kernel,
        grid=(n,),
        in_specs=[pl.BlockSpec((1, 1, 2 * tm), lambda i: (i, 0, 0), memory_space=pltpu.SMEM),
                  pl.BlockSpec((1, 1, n_fill), lambda i: (i, 0, 0), memory_space=pltpu.SMEM),
                  pl.BlockSpec((tm, d), lambda i: (i, 0))],
        out_specs=pl.BlockSpec(memory_space=pl.ANY),
        out_shape=jax.ShapeDtypeStruct((n_out, d), x_rows.dtype),
        scratch_shapes=[pltpu.VMEM((tm, d), x_rows.dtype), pltpu.VMEM((tm, d), x_rows.dtype),
                        pltpu.VMEM((8, d), x_rows.dtype), pltpu.SemaphoreType.DMA((2,))],
        compiler_params=_cparams("arbitrary"),
        name="moe_dispatch",
    )(slot.reshape(n, 1, 2 * tm), pad_slots.reshape(n, 1, n_fill), x_rows)


def _expert_kernel(blk_e_ref, x_ref, w1_ref, w3_ref, w2_ref, o_ref):
    del blk_e_ref
    xb = x_ref[...].astype(BF16)
    h1 = jnp.dot(xb, w1_ref[0], preferred_element_type=F32)
    h3 = jnp.dot(xb, w3_ref[0], preferred_element_type=F32)
    hid = (h1 * jax.nn.sigmoid(h1) * h3).astype(BF16)
    o_ref[...] = jnp.dot(hid, w2_ref[0], preferred_element_type=F32)


def _expert_blocks(blk_e, x_sorted, w1, w3, w2, layer):
    n_blk = blk_e.shape[0]
    tm = MOE_TM
    d = x_sorted.shape[1]
    grid_spec = pltpu.PrefetchScalarGridSpec(
        num_scalar_prefetch=1,
        grid=(n_blk,),
        in_specs=[
            pl.BlockSpec((tm, d), lambda i, e: (i, 0)),
            pl.BlockSpec((None, 1, d, D_EXPERT), lambda i, e: (layer, e[i], 0, 0)),
            pl.BlockSpec((None, 1, d, D_EXPERT), lambda i, e: (layer, e[i], 0, 0)),
            pl.BlockSpec((None, 1, D_EXPERT, d), lambda i, e: (layer, e[i], 0, 0)),
        ],
        out_specs=pl.BlockSpec((tm, d), lambda i, e: (i, 0)),
    )
    return pl.pallas_call(
        _expert_kernel,
        grid_spec=grid_spec,
        out_shape=jax.ShapeDtypeStruct((n_blk * tm, d), F32),
        compiler_params=_cparams("arbitrary"),
        name="moe_experts",
    )(blk_e, x_sorted, w1, w3, w2)


def _combine_kernel(slot_ref, slot_next_ref, y_hbm, x_ref, gt_ref, wt_ref, o_ref, a0, b0, a1, b1, sem):
    tm = a0.shape[0]
    i = pl.program_id(0)
    n = pl.num_programs(0)
    bufs = ((a0, b0), (a1, b1))

    def wait_rows(p):
        def wait(r, c):
            _row_copy(y_hbm, bufs[p][0], sem.at[p], 0, r).wait()
            _row_copy(y_hbm, bufs[p][1], sem.at[p], 0, r).wait()
            return c
        lax.fori_loop(0, tm, wait, 0, unroll=8)

    @pl.when(i == 0)
    def _():
        def issue(r, c):
            _row_copy(y_hbm, a0, sem.at[0], slot_ref[0, 0, 2 * r], r).start()
            _row_copy(y_hbm, b0, sem.at[0], slot_ref[0, 0, 2 * r + 1], r).start()
            return c
        lax.fori_loop(0, tm, issue, 0, unroll=8)

    def block(cur):
        wait_rows(cur)
        nxt_a, nxt_b = bufs[1 - cur]
        for r in range(tm):
            _row_copy(y_hbm, nxt_a, sem.at[1 - cur], slot_next_ref[0, 0, 2 * r], r).start(priority=0)
            _row_copy(y_hbm, nxt_b, sem.at[1 - cur], slot_next_ref[0, 0, 2 * r + 1], r).start(priority=1)
        wt = wt_ref[...]
        f = bufs[cur][0][...] * wt[:, 0:1] + bufs[cur][1][...] * wt[:, 1:2]
        o_ref[...] = x_ref[...] + gt_ref[0] * f

    for parity in range(2):
        @pl.when(i % 2 == parity)
        def _(parity=parity):
            block(parity)

        @pl.when((i == n - 1) & (i % 2 == parity))
        def _(parity=parity):
            wait_rows(1 - parity)


def _combine(slots, y, x, gt, wt, tok0, tm):
    b, t, d = x.shape
    n_i = t // tm
    n = b * n_i
    x2 = x.reshape(b * t, d)
    blk0 = tok0 // tm
    slots3 = slots.reshape(-1, 1, 2 * tm)
    out = pl.pallas_call(
        _combine_kernel,
        grid=(n,),
        in_specs=[
            pl.BlockSpec((1, 1, 2 * tm), lambda i: (blk0 + i, 0, 0), memory_space=pltpu.SMEM),
            pl.BlockSpec((1, 1, 2 * tm), lambda i: (blk0 + jnp.minimum(i + 1, n - 1), 0, 0),
                         memory_space=pltpu.SMEM),
            pl.BlockSpec(memory_space=pl.ANY),
            pl.BlockSpec((tm, d), lambda i: (i, 0)),
            pl.BlockSpec((1, 1, d), lambda i: (i // n_i, 0, 0)),
            pl.BlockSpec((tm, 128), lambda i: (blk0 + i, 0)),
        ],
        out_specs=pl.BlockSpec((tm, d), lambda i: (i, 0)),
        out_shape=jax.ShapeDtypeStruct((b * t, d), F32),
        scratch_shapes=[pltpu.VMEM((tm, d), F32)] * 4 + [pltpu.SemaphoreType.DMA((2,))],
        compiler_params=_cparams("arbitrary"),
        name="moe_combine",
    )(slots3, slots3, y, x2, gt, wt)
    return out.reshape(b, t, d)


def _routing_tables(eid, n_steps):
    m = eid.shape[0]
    tm = MOE_TM
    i32 = jnp.int32
    onehot = (eid[:, None] == jnp.arange(N_EXPERTS, dtype=i32)[None, :]).astype(i32)
    counts = jnp.sum(onehot, axis=0)
    pcounts = (counts + tm - 1) // tm * tm
    pends = jnp.cumsum(pcounts)
    pstarts = pends - pcounts
    rank = jnp.sum((jnp.cumsum(onehot, axis=0) - onehot) * onehot, axis=1)
    slot = rank + jnp.sum(onehot * pstarts[None, :], axis=1)
    n_blk = (m + N_EXPERTS * (tm - 1) + tm - 1) // tm
    n_pad = n_blk * tm
    blk_start = jnp.arange(n_blk, dtype=i32) * tm
    blk_e = jnp.minimum(jnp.sum((pends[None, :] <= blk_start[:, None]).astype(i32), axis=1), N_EXPERTS - 1)
    n_empty = n_pad - m
    n_fill = -(-n_empty // n_steps)
    n_fill = (n_fill + 7) // 8 * 8
    seg_len = jnp.concatenate([pcounts - counts, (n_pad - pends[-1:])])
    seg_first = jnp.concatenate([pstarts + counts, pends[-1:]])
    seg_end = jnp.cumsum(seg_len)
    seg_begin = seg_end - seg_len
    j = jnp.arange(n_empty, dtype=i32)
    in_seg = ((seg_begin[None, :] <= j[:, None]) & (j[:, None] < seg_end[None, :])).astype(i32)
    empty = j + jnp.sum(in_seg * (seg_first - seg_begin)[None, :], axis=1)
    n_spare = n_steps * n_fill - n_empty
    fill = jnp.concatenate([empty, n_pad + jnp.arange(n_spare, dtype=i32)])
    return blk_e.astype(i32), slot, fill, n_pad + n_spare


def _moe(xs, scs, shs, gts, norm_g, wr_hi, wr_lo, br, w1, w3, w2, layer):
    h_all, eid_all, wt_all = _router(xs, scs, shs, norm_g, wr_hi, wr_lo, br, 512)
    n_steps = h_all.shape[0] // MOE_DISPATCH_TOKENS
    blk_e, slot, fill, n_rows = _routing_tables(eid_all[:, :TOP_K].reshape(-1), n_steps)
    y = _expert_blocks(blk_e, _dispatch(slot, fill, h_all, n_rows, MOE_DISPATCH_TOKENS), w1, w3, w2, layer)
    outs = []
    tok0 = 0
    for x, gt in zip(xs, gts):
        outs.append(_combine(slot, y, x, gt, wt_all, tok0, tm=MOE_COMBINE_TOKENS))
        tok0 += x.shape[0] * x.shape[1]
    return outs


def kernel(x, c, ctx, c_ctx, w_mod, b_mod, norm1_g, norm2_g, w_in, na_q_g, na_k_g, na_rpb, df_q_g, df_k_g, df_lam, df_sub_g, rg_conv_w, rg_conv_b, rg_w_a, rg_b_a, rg_w_x, rg_b_x, rg_lam, w_branch, w_out, w_router_g, b_router_g, w_router_e, b_router_e, w1, w3, w2):
    B, S, D = x.shape
    C = ctx.shape[1]
    rope_cos, rope_sin = _rope_tables(S)
    na_lo, na_hi = _na_bias_tables(na_rpb)

    pad = (-(B + 1)) % 8
    rows = jnp.concatenate([c, c_ctx[None, :], jnp.zeros((pad, D), F32)], axis=0)
    mod_all = _modulation(rows, w_mod, b_mod)

    w_in_b, w_branch_b, w_out_b = w_in.astype(BF16), w_branch.astype(BF16), w_out.astype(BF16)
    w1_b, w3_b, w2_b = w1.astype(BF16), w3.astype(BF16), w2.astype(BF16)

    xc = ctx.reshape(1, B * C, D)
    for l in range(DEPTH):
        need_ctx = l < DEPTH - 1
        lam_init = 0.8 - 0.6 * float(np.exp(-0.3 * l))
        mod = mod_all[l]
        sh1, sc1, gt1, sh2, sc2, gt2 = [mod[:B, k * D:(k + 1) * D][:, None, :] for k in range(6)]
        csh1, csc1, cgt1, csh2, csc2, cgt2 = [mod[B:B + 1, k * D:(k + 1) * D][:, None, :] for k in range(6)]

        zx = _norm_mod_matmul(x, norm1_g[l], sc1, sh1, w_in_b, l, IN_COLS, tm=1024, tn=1024)
        n_cc = IN_COLS if need_ctx else KV_COLS
        zc = _norm_mod_matmul(xc, norm1_g[l], csc1, csh1, w_in_b, l, n_cc, tm=1024, tn=1024).reshape(B, C, n_cc)

        y_a = _na_attention(zx, zc, na_lo[l], na_hi[l], na_q_g[l], na_k_g[l])

        lp = df_lam[l]
        lam = jnp.exp(jnp.sum(lp[0] * lp[1])) - jnp.exp(jnp.sum(lp[2] * lp[3])) + lam_init
        y_b = _df_attention(lam, zx, zx, zc, rope_cos, rope_sin, df_q_g[l], df_k_g[l], df_sub_g[l], 1.0 - lam_init)

        y_c, y_cc = _rg_lru(zx, zc, rg_conv_w[l], rg_conv_b[l], rg_w_a[l].astype(BF16), rg_b_a[l],
                            rg_w_x[l].astype(BF16), rg_b_x[l], rg_lam[l], need_ctx)

        m_x = _branch_merge(y_a, y_b, y_c, zx, w_branch_b, l, tm=1024, tn=1024)
        x = _out_proj_residual(m_x, w_out_b, l, x, gt1, tm=1024, tn=1024)
        if need_ctx:
            y_ac = _na_ctx_attention(zc, na_q_g[l], na_k_g[l])
            y_bc = _df_attention(lam, zc, None, zc, None, None, df_q_g[l], df_k_g[l], df_sub_g[l], 1.0 - lam_init)
            m_c = _branch_merge(y_ac, y_bc, y_cc, zc, w_branch_b, l, tm=C, tn=1024)
            xc = _out_proj_residual(m_c.reshape(1, B * C, D), w_out_b, l, xc, cgt1, tm=1024, tn=1024)

        wr = jnp.concatenate([w_router_g[l], w_router_e[l],
                              jnp.zeros((D, 128 - N_GROUPS - N_EXPERTS), F32)], axis=1)
        wr_hi = wr.astype(BF16)
        wr_lo = (wr - wr_hi.astype(F32)).astype(BF16)
        br = jnp.concatenate([b_router_g[l], b_router_e[l],
                              jnp.zeros((128 - N_GROUPS - N_EXPERTS,), F32)]).reshape(1, 128)
        if need_ctx:
            xc, x = _moe([xc, x], [csc2, sc2], [csh2, sh2], [cgt2, gt2], norm2_g[l],
                         wr_hi, wr_lo, br, w1_b, w3_b, w2_b, l)
        else:
            (x,) = _moe([x], [sc2], [sh2], [gt2], norm2_g[l], wr_hi, wr_lo, br, w1_b, w3_b, w2_b, l)
    return x
```
